```python
import math
import jax, jax.numpy as jnp
from jax import lax
import numpy as np

D_MODEL = 1024
BATCH = 8
SEQ = 2048
DEPTH = 4
DEC_BATCH = 128
DEC_SEQ = 4
PAST_LEN = 16384
PAGE_SIZE = 128

N_META = 16
CHUNK = 64
EPS = 1e-6
SSM_EXPAND = 2
D_INNER = SSM_EXPAND * D_MODEL
SSM_HEAD_DIM = 64
SSM_HEADS = D_INNER // SSM_HEAD_DIM
SSM_GROUPS = 4
SSM_STATE = 128
SSM_CONV = 4
SSM_XBC = D_INNER + 2 * SSM_GROUPS * SSM_STATE
GDN_HEADS = 8
GDN_DK = 128
GDN_DV = 128
GDN_CONV = 4
GDN_QKV = GDN_HEADS * (2 * GDN_DK + GDN_DV)
D_FF = 2816
FFN_CONV = 3
IN_SIZES = (D_INNER, SSM_XBC, SSM_HEADS, GDN_QKV, GDN_HEADS, GDN_HEADS, GDN_HEADS * GDN_DV, D_MODEL, D_MODEL)
IN_DIM = sum(IN_SIZES)

kernel_name = "hybrid_ssd_gdn_convffn_meta_step"


def split_cols(t, sizes):
    out = []
    off = 0
    for s in sizes:
        out.append(t[..., off:off + s])
        off += s
    return out


def rmsnorm(x, w):
    xf = x.astype(jnp.float32)
    y = xf * lax.rsqrt(jnp.mean(xf * xf, axis=-1, keepdims=True) + EPS)
    return (y * w.astype(jnp.float32)).astype(x.dtype)


def l2norm(x):
    return x * lax.rsqrt(jnp.sum(x * x, axis=-1, keepdims=True) + EPS)


def causal_dwconv(x, buf, w):
    width = w.shape[0]
    L = x.shape[1]
    xp = jnp.concatenate([buf.astype(x.dtype), x], axis=1)
    y = xp[:, 0:L] * w[0]
    for j in range(1, width):
        y = y + xp[:, j:j + L] * w[j]
    return y, xp[:, L:]


def ssd_segment(x, Bm, Cm, dt, A, h0):
    b, L = x.shape[:2]
    c = math.gcd(L, CHUNK)
    nc = L // c
    hg = SSM_HEADS // SSM_GROUPS

    def to_chunks(t):
        return jnp.moveaxis(t.reshape((b, nc, c) + t.shape[2:]), 1, 0)

    xs = to_chunks(x.reshape(b, L, SSM_GROUPS, hg, SSM_HEAD_DIM))
    dts = to_chunks(dt.reshape(b, L, SSM_GROUPS, hg))
    Bs = to_chunks(Bm)
    Cs = to_chunks(Cm)
    Ag = A.reshape(SSM_GROUPS, hg)
    causal = jnp.tril(jnp.ones((c, c), dtype=bool))

    def step(h, inp):
        xc, Bc, Cc, dtc = inp
        acum = jnp.cumsum(dtc * Ag, axis=1)
        seg = acum[:, :, None] - acum[:, None, :]
        decay = jnp.exp(jnp.where(causal[None, :, :, None, None], seg, -jnp.inf))
        cb = jnp.einsum('bign,bjgn->bijg', Cc, Bc)
        wgt = decay * cb[..., None] * dtc[:, None]
        y = jnp.einsum('bijgh,bjghp->bighp', wgt, xc)
        y = y + jnp.einsum('bign,bghpn->bighp', Cc, h) * jnp.exp(acum)[..., None]
        last = acum[:, -1]
        wj = jnp.exp(last[:, None] - acum) * dtc
        h = h * jnp.exp(last)[..., None, None] + jnp.einsum('bjgh,bjghp,bjgn->bghpn', wj, xc, Bc)
        return h, y

    hG = h0.reshape(b, SSM_GROUPS, hg, SSM_HEAD_DIM, SSM_STATE)
    hG, ys = lax.scan(step, hG, (xs, Bs, Cs, dts))
    y = jnp.moveaxis(ys, 0, 1).reshape(b, L, SSM_HEADS, SSM_HEAD_DIM)
    return y, hG.reshape(b, SSM_HEADS, SSM_HEAD_DIM, SSM_STATE)


def gdn_segment(q, k, v, g, beta, S0):
    b, L = q.shape[:2]
    c = math.gcd(L, CHUNK)
    nc = L // c

    def to_chunks(t):
        t = jnp.moveaxis(t.reshape((b, nc, c) + t.shape[2:]), 1, 0)
        return jnp.moveaxis(t, 3, 2)

    qs, ks, vs, gs, bs = (to_chunks(t) for t in (q, k, v, g, beta))
    incl = jnp.tril(jnp.ones((c, c), dtype=bool))
    strict = jnp.tril(jnp.ones((c, c), dtype=bool), -1)
    eye = jnp.eye(c, dtype=jnp.float32)

    def step(S, inp):
        qc, kc, vc, gc, bc = inp
        gcum = jnp.cumsum(gc, axis=-1)
        seg = gcum[..., :, None] - gcum[..., None, :]
        decay = jnp.exp(jnp.where(incl, seg, -jnp.inf))
        kb = kc * bc[..., None]
        amat = jnp.where(strict, jnp.einsum('bhik,bhjk->bhij', kb, kc) * decay, 0.0)
        rhs = jnp.concatenate([vc * bc[..., None], kb * jnp.exp(gcum)[..., None]], axis=-1)
        sol = lax.linalg.triangular_solve(eye + amat, rhs, left_side=True, lower=True, unit_diagonal=True)
        u, w = sol[..., :GDN_DV], sol[..., GDN_DV:]
        v_new = u - jnp.einsum('bhik,bhkv->bhiv', w, S)
        attn = jnp.einsum('bhik,bhjk->bhij', qc, kc) * decay
        o = jnp.einsum('bhik,bhkv->bhiv', qc * jnp.exp(gcum)[..., None], S) + jnp.einsum('bhij,bhjv->bhiv', attn, v_new)
        last = gcum[..., -1]
        S = S * jnp.exp(last)[..., None, None] + jnp.einsum(
            'bhjk,bhjv->bhkv', kc * jnp.exp(last[..., None] - gcum)[..., None], v_new)
        return S, o

    S, os_ = lax.scan(step, S0, (qs, ks, vs, gs, bs))
    o = jnp.moveaxis(jnp.moveaxis(os_, 3, 2), 0, 1).reshape(b, L, GDN_HEADS, GDN_DV)
    return o, S


def mixer_block(x, s_ssm, s_ssm_conv, s_gdn, s_gdn_conv, segments,
                norm_w, w_in, ssm_conv_w, ssm_conv_b, ssm_dt_bias, ssm_a_log, ssm_d, ssm_norm_w,
                gdn_conv_w, gdn_dt_bias, gdn_a_log, gdn_norm_w, w_ssm_out, w_gdn_out, w_o):
    f32 = jnp.float32
    bsz, L, _ = x.shape
    h = rmsnorm(x, norm_w)
    z, xbc, dt_raw, qkv, a_raw, b_raw, gate, mg_ssm, mg_gdn = split_cols(h @ w_in, IN_SIZES)
    xbc, new_ssm_conv = causal_dwconv(xbc, s_ssm_conv, ssm_conv_w)
    xbc = jax.nn.silu(xbc + ssm_conv_b)
    xs, Bm, Cm = split_cols(xbc, (D_INNER, SSM_GROUPS * SSM_STATE, SSM_GROUPS * SSM_STATE))
    xs = xs.reshape(bsz, L, SSM_HEADS, SSM_HEAD_DIM).astype(f32)
    Bm = Bm.reshape(bsz, L, SSM_GROUPS, SSM_STATE).astype(f32)
    Cm = Cm.reshape(bsz, L, SSM_GROUPS, SSM_STATE).astype(f32)
    dt = jax.nn.softplus((dt_raw + ssm_dt_bias).astype(f32))
    A = -jnp.exp(ssm_a_log.astype(f32))
    qkv, new_gdn_conv = causal_dwconv(qkv, s_gdn_conv, gdn_conv_w)
    qkv = jax.nn.silu(qkv)
    q, k, v = split_cols(qkv, (GDN_HEADS * GDN_DK, GDN_HEADS * GDN_DK, GDN_HEADS * GDN_DV))
    q = l2norm(q.reshape(bsz, L, GDN_HEADS, GDN_DK).astype(f32)) * (GDN_DK ** -0.5)
    k = l2norm(k.reshape(bsz, L, GDN_HEADS, GDN_DK).astype(f32))
    v = v.reshape(bsz, L, GDN_HEADS, GDN_DV).astype(f32)
    g = -jnp.exp(gdn_a_log.astype(f32)) * jax.nn.softplus((a_raw + gdn_dt_bias).astype(f32))
    beta = jax.nn.sigmoid(b_raw.astype(f32))
    hs = s_ssm.astype(f32)
    S = s_gdn.astype(f32)
    y_ssm_parts, y_gdn_parts = [], []
    start = 0
    for n in segments:
        sl = slice(start, start + n)
        y1, hs = ssd_segment(xs[:, sl], Bm[:, sl], Cm[:, sl], dt[:, sl], A, hs)
        y2, S = gdn_segment(q[:, sl], k[:, sl], v[:, sl], g[:, sl], beta[:, sl], S)
        y_ssm_parts.append(y1)
        y_gdn_parts.append(y2)
        start += n
    y_ssm = jnp.concatenate(y_ssm_parts, axis=1) + ssm_d.astype(f32)[:, None] * xs
    y_gdn = jnp.concatenate(y_gdn_parts, axis=1)
    y_ssm = y_ssm.reshape(bsz, L, D_INNER) * jax.nn.silu(z.astype(f32))
    y_ssm = rmsnorm(y_ssm.reshape(bsz, L, SSM_GROUPS, D_INNER // SSM_GROUPS),
                    ssm_norm_w.reshape(SSM_GROUPS, D_INNER // SSM_GROUPS))
    y_ssm = y_ssm.reshape(bsz, L, D_INNER).astype(x.dtype) @ w_ssm_out
    y_gdn = rmsnorm(y_gdn, gdn_norm_w) * jax.nn.silu(gate.reshape(bsz, L, GDN_HEADS, GDN_DV).astype(f32))
    y_gdn = y_gdn.reshape(bsz, L, GDN_HEADS * GDN_DV).astype(x.dtype) @ w_gdn_out
    merged = jax.nn.sigmoid(mg_ssm) * y_ssm + jax.nn.sigmoid(mg_gdn) * y_gdn
    x = x + merged @ w_o
    return x, hs.astype(x.dtype), new_ssm_conv, S.astype(x.dtype), new_gdn_conv


def conv_ffn(x, s_conv, norm_w, w_up, conv_w, conv_b, w_down):
    h = rmsnorm(x, norm_w)
    gt, up = split_cols(h @ w_up, (D_FF, D_FF))
    gt, new_conv = causal_dwconv(gt, s_conv, conv_w)
    return x + (jax.nn.silu(gt + conv_b) * up) @ w_down, new_conv


def run_trunk(x, s_ssm, s_ssm_conv, s_gdn, s_gdn_conv, s_ffn_conv, segments,
              norm_mix_w, w_in, ssm_conv_w, ssm_conv_b, ssm_dt_bias, ssm_a_log, ssm_d, ssm_norm_w,
              gdn_conv_w, gdn_dt_bias, gdn_a_log, gdn_norm_w, w_ssm_out, w_gdn_out, w_o,
              norm_ffn_w, w_up, ffn_conv_w, ffn_conv_b, w_down, norm_f_w):
    o_ssm, o_ssm_conv, o_gdn, o_gdn_conv, o_ffn_conv = [], [], [], [], []
    for l in range(DEPTH):
        x, a1, a2, a3, a4 = mixer_block(
            x, s_ssm[l], s_ssm_conv[l], s_gdn[l], s_gdn_conv[l], segments,
            norm_mix_w[l], w_in[l], ssm_conv_w[l], ssm_conv_b[l], ssm_dt_bias[l], ssm_a_log[l], ssm_d[l],
            ssm_norm_w[l], gdn_conv_w[l], gdn_dt_bias[l], gdn_a_log[l], gdn_norm_w[l],
            w_ssm_out[l], w_gdn_out[l], w_o[l])
        x, a5 = conv_ffn(x, s_ffn_conv[l], norm_ffn_w[l], w_up[l], ffn_conv_w[l], ffn_conv_b[l], w_down[l])
        o_ssm.append(a1); o_ssm_conv.append(a2); o_gdn.append(a3); o_gdn_conv.append(a4); o_ffn_conv.append(a5)
    y = rmsnorm(x, norm_f_w)
    return (y, jnp.stack(o_ssm), jnp.stack(o_ssm_conv), jnp.stack(o_gdn),
            jnp.stack(o_gdn_conv), jnp.stack(o_ffn_conv))


def setup_inputs(seed: int = 0) -> dict:
    key = jax.random.key(seed)
    ks = iter(jax.random.split(key, 48))
    f32 = jnp.float32

    def nrm(shape, scale):
        return scale * jax.random.normal(next(ks), shape, f32)

    def dt_bias(n):
        dt = jnp.exp(jax.random.uniform(next(ks), (DEPTH, n), f32, math.log(1e-3), math.log(1e-1)))
        return dt + jnp.log(-jnp.expm1(-dt))

    def a_log(n):
        return jnp.log(jax.random.uniform(next(ks), (DEPTH, n), f32, 1.0, 16.0))

    return {
        "x_prompt": nrm((BATCH, SEQ, D_MODEL), 1.0),
        "x_sample": nrm((DEC_BATCH, DEC_SEQ, D_MODEL), 1.0),
        "state_ssm": nrm((DEPTH, DEC_BATCH, SSM_HEADS, SSM_HEAD_DIM, SSM_STATE), 0.1),
        "state_ssm_conv": nrm((DEPTH, DEC_BATCH, SSM_CONV - 1, SSM_XBC), 1.0),
        "state_gdn": nrm((DEPTH, DEC_BATCH, GDN_HEADS, GDN_DK, GDN_DV), 0.1),
        "state_gdn_conv": nrm((DEPTH, DEC_BATCH, GDN_CONV - 1, GDN_QKV), 1.0),
        "state_ffn_conv": nrm((DEPTH, DEC_BATCH, FFN_CONV - 1, D_FF), 1.0),
        "meta_tokens": nrm((N_META, D_MODEL), 1.0),
        "norm_mix_w": 1.0 + nrm((DEPTH, D_MODEL), 0.02),
        "w_in": nrm((DEPTH, D_MODEL, IN_DIM), D_MODEL ** -0.5),
        "ssm_conv_w": nrm((DEPTH, SSM_CONV, SSM_XBC), SSM_CONV ** -0.5),
        "ssm_conv_b": nrm((DEPTH, SSM_XBC), 0.01),
        "ssm_dt_bias": dt_bias(SSM_HEADS),
        "ssm_a_log": a_log(SSM_HEADS),
        "ssm_d": 1.0 + nrm((DEPTH, SSM_HEADS), 0.1),
        "ssm_norm_w": 1.0 + nrm((DEPTH, D_INNER), 0.02),
        "gdn_conv_w": nrm((DEPTH, GDN_CONV, GDN_QKV), GDN_CONV ** -0.5),
        "gdn_dt_bias": dt_bias(GDN_HEADS),
        "gdn_a_log": a_log(GDN_HEADS),
        "gdn_norm_w": 1.0 + nrm((DEPTH, GDN_DV), 0.02),
        "w_ssm_out": nrm((DEPTH, D_INNER, D_MODEL), D_INNER ** -0.5),
        "w_gdn_out": nrm((DEPTH, GDN_HEADS * GDN_DV, D_MODEL), (GDN_HEADS * GDN_DV) ** -0.5),
        "w_o": nrm((DEPTH, D_MODEL, D_MODEL), D_MODEL ** -0.5),
        "norm_ffn_w": 1.0 + nrm((DEPTH, D_MODEL), 0.02),
        "w_up": nrm((DEPTH, D_MODEL, 2 * D_FF), D_MODEL ** -0.5),
        "ffn_conv_w": nrm((DEPTH, FFN_CONV, D_FF), FFN_CONV ** -0.5),
        "ffn_conv_b": nrm((DEPTH, D_FF), 0.01),
        "w_down": nrm((DEPTH, D_FF, D_MODEL), D_FF ** -0.5),
        "norm_f_w": 1.0 + nrm((D_MODEL,), 0.02),
    }


def reference(x_prompt, x_sample, state_ssm, state_ssm_conv, state_gdn, state_gdn_conv, state_ffn_conv,
              meta_tokens, norm_mix_w, w_in, ssm_conv_w, ssm_conv_b, ssm_dt_bias, ssm_a_log, ssm_d, ssm_norm_w,
              gdn_conv_w, gdn_dt_bias, gdn_a_log, gdn_norm_w, w_ssm_out, w_gdn_out, w_o,
              norm_ffn_w, w_up, ffn_conv_w, ffn_conv_b, w_down, norm_f_w):
    weights = (norm_mix_w, w_in, ssm_conv_w, ssm_conv_b, ssm_dt_bias, ssm_a_log, ssm_d, ssm_norm_w,
               gdn_conv_w, gdn_dt_bias, gdn_a_log, gdn_norm_w, w_ssm_out, w_gdn_out, w_o,
               norm_ffn_w, w_up, ffn_conv_w, ffn_conv_b, w_down, norm_f_w)
    bp, sp, _ = x_prompt.shape
    dtp = x_prompt.dtype
    xp = jnp.concatenate([jnp.broadcast_to(meta_tokens.astype(dtp), (bp, N_META, D_MODEL)), x_prompt], axis=1)
    z_ssm = jnp.zeros((DEPTH, bp, SSM_HEADS, SSM_HEAD_DIM, SSM_STATE), dtp)
    z_ssm_conv = jnp.zeros((DEPTH, bp, SSM_CONV - 1, SSM_XBC), dtp)
    z_gdn = jnp.zeros((DEPTH, bp, GDN_HEADS, GDN_DK, GDN_DV), dtp)
    z_gdn_conv = jnp.zeros((DEPTH, bp, GDN_CONV - 1, GDN_QKV), dtp)
    z_ffn_conv = jnp.zeros((DEPTH, bp, FFN_CONV - 1, D_FF), dtp)
    yp, p_ssm, p_ssm_conv, p_gdn, p_gdn_conv, p_ffn_conv = run_trunk(
        xp, z_ssm, z_ssm_conv, z_gdn, z_gdn_conv, z_ffn_conv, (N_META, sp), *weights)
    y_prompt = yp[:, N_META:]
    y_sample, s_ssm, s_ssm_conv, s_gdn, s_gdn_conv, s_ffn_conv = run_trunk(
        x_sample, state_ssm, state_ssm_conv, state_gdn, state_gdn_conv, state_ffn_conv,
        (x_sample.shape[1],), *weights)
    return (y_prompt, y_sample, p_ssm, p_ssm_conv, p_gdn, p_gdn_conv, p_ffn_conv,
            s_ssm, s_ssm_conv, s_gdn, s_gdn_conv, s_ffn_conv)
```

```python
import functools

import jax
import jax.numpy as jnp
from jax import lax
from jax.experimental import pallas as pl
from jax.experimental.pallas import tpu as pltpu

F32 = jnp.float32
BF16 = jnp.bfloat16
HIGHEST = lax.Precision.HIGHEST

EPS = 1e-6
N_META = 16
D_MODEL = 1024
SSM_HEADS = 32
SSM_HEAD_DIM = 64
SSM_GROUPS = 4
SSM_STATE = 128
SSM_CONV = 4
D_INNER = SSM_HEADS * SSM_HEAD_DIM
SSM_BC = SSM_GROUPS * SSM_STATE
SSM_XBC = D_INNER + 2 * SSM_BC
SSM_PAIRS = SSM_HEADS // 2
PAIRS_PER_GROUP = SSM_PAIRS // SSM_GROUPS
GROUP_COLS = D_INNER // SSM_GROUPS
GDN_HEADS = 8
GDN_DK = 128
GDN_DV = 128
GDN_CONV = 4
GDN_QK = GDN_HEADS * GDN_DK
GDN_QKV = 2 * GDN_QK + GDN_HEADS * GDN_DV
D_FF = 2816
FFN_CONV = 3
IN_SIZES = (D_INNER, SSM_XBC, SSM_HEADS, GDN_QKV, GDN_HEADS, GDN_HEADS, GDN_HEADS * GDN_DV, D_MODEL, D_MODEL)
PROJ_COLS = SSM_XBC + GDN_QKV + D_INNER + 3 * D_MODEL
Z_BLOCK = (SSM_XBC + GDN_QKV) // D_INNER
GATE_BLOCK = (SSM_XBC + GDN_QKV + D_INNER) // D_MODEL
SLAB = 128
SLAB_GA = SSM_HEADS
SLAB_GB = SSM_HEADS + GDN_HEADS

LANES = 128
SUBLANES = 8
MXU_DIM = 256
SCAN_CHUNK = 64
VMEM_LIMIT = 56 * 1024 * 1024


def _pick_tile(total, cap, mult):
    best = None
    for t in range(mult, min(total, cap) + 1, mult):
        if total % t == 0:
            best = t
    assert best is not None, (total, cap, mult)
    return best


def _sigmoid(x):
    return 1.0 / (1.0 + jnp.exp(-x))


def _silu(x):
    return x * _sigmoid(x)


def _softplus(x):
    return jnp.maximum(x, 0.0) + jnp.log1p(jnp.exp(-jnp.abs(x)))


def _rms(x, w):
    return x * lax.rsqrt(jnp.mean(x * x, axis=-1, keepdims=True) + EPS) * w


def _dot(a, b):
    return jnp.dot(a.astype(BF16), b.astype(BF16), preferred_element_type=F32)


def _dot_nt(a, b):
    return lax.dot_general(a.astype(BF16), b.astype(BF16), (((1,), (1,)), ((), ())), preferred_element_type=F32)


def _dot_tn(a, b):
    return lax.dot_general(a.astype(BF16), b.astype(BF16), (((0,), (0,)), ((), ())), preferred_element_type=F32)


def _cumsum_rows(x):
    c = x.shape[0]
    tril = (lax.broadcasted_iota(jnp.int32, (c, c), 0) >= lax.broadcasted_iota(jnp.int32, (c, c), 1)).astype(F32)
    return jnp.dot(tril, x, precision=HIGHEST, preferred_element_type=F32)


def _causal_conv(buf, w_ref, width, rows):
    lead = (slice(None),) * (len(buf.shape) - 2)
    acc = None
    for j in range(width):
        off = SUBLANES - (width - 1) + j
        term = buf[lead + (slice(off, off + rows), slice(None))] * w_ref[j:j + 1, :]
        acc = term if acc is None else acc + term
    return acc


def _norm_kernel(x_ref, w_ref, o_ref):
    o_ref[...] = _rms(x_ref[...], w_ref[...]).astype(o_ref.dtype)


def _norm(x, w, layer, out_dtype):
    t, d = x.shape
    tm = _pick_tile(t, 1024, 16)
    return pl.pallas_call(
        _norm_kernel,
        grid=(t // tm,),
        in_specs=[pl.BlockSpec((tm, d), lambda i: (i, 0)),
                  pl.BlockSpec((None, 1, d), lambda i: (layer, 0, 0))],
        out_specs=pl.BlockSpec((tm, d), lambda i: (i, 0)),
        out_shape=jax.ShapeDtypeStruct((t, d), out_dtype),
        compiler_params=pltpu.CompilerParams(dimension_semantics=("parallel",)),
        name="rmsnorm",
    )(x, w)


def _inproj_kernel(h_ref, w_ref, ws_ref, o_ref, os_ref):
    h = h_ref[...]
    o_ref[...] = jnp.dot(h, w_ref[...], preferred_element_type=F32)

    @pl.when(pl.program_id(1) == 0)
    def _():
        os_ref[...] = jnp.dot(h, ws_ref[...], preferred_element_type=F32)


def _inproj(h, w_main, w_slab, layer):
    t, d = h.shape
    ncols = w_main.shape[-1]
    tm = _pick_tile(t, 1536, 16)
    tn = _pick_tile(ncols, 1024, LANES)
    return pl.pallas_call(
        _inproj_kernel,
        grid=(t // tm, ncols // tn),
        in_specs=[pl.BlockSpec((tm, d), lambda i, j: (i, 0)),
                  pl.BlockSpec((None, d, tn), lambda i, j: (layer, 0, j)),
                  pl.BlockSpec((None, d, SLAB), lambda i, j: (layer, 0, 0))],
        out_specs=[pl.BlockSpec((tm, tn), lambda i, j: (i, j)),
                   pl.BlockSpec((tm, SLAB), lambda i, j: (i, 0))],
        out_shape=[jax.ShapeDtypeStruct((t, ncols), F32), jax.ShapeDtypeStruct((t, SLAB), F32)],
        compiler_params=pltpu.CompilerParams(dimension_semantics=("parallel", "arbitrary"),
                                             vmem_limit_bytes=VMEM_LIMIT),
        name="inproj",
    )(h, w_main, w_slab)


def _pair_expand(arr, p, width, half):
    r = arr.shape[0]
    even = jnp.broadcast_to(arr[:, p:p + 1], (r, width))
    odd = jnp.broadcast_to(arr[:, SSM_PAIRS + p:SSM_PAIRS + p + 1], (r, width))
    return jnp.where(half, even, odd)


def _ssd_kernel(*refs, c, v0, v1, vend, has_state):
    if has_state:
        (xbc_ref, z_ref, sm_ref, st_ref, cst_ref, cw_ref, cb_ref, bias_ref, alog_ref, dexp_ref, nw_ref,
         y_ref, sto_ref, csto_ref, buf, ht) = refs
    else:
        (xbc_ref, z_ref, sm_ref, cw_ref, cb_ref, bias_ref, alog_ref, dexp_ref, nw_ref,
         y_ref, sto_ref, csto_ref, buf, ht) = refs
    ci = pl.program_id(1)
    last_chunk = pl.num_programs(1) - 1
    hist = SSM_CONV - 1

    @pl.when(ci == 0)
    def _():
        buf[0:SUBLANES, :] = jnp.zeros((SUBLANES, SSM_XBC), F32)
        if has_state:
            buf[SUBLANES - hist:SUBLANES, :] = cst_ref[...]
            for p in range(SSM_PAIRS):
                ht[:, p * LANES:(p + 1) * LANES] = st_ref[p * LANES:(p + 1) * LANES, :].T
        else:
            ht[...] = jnp.zeros(ht.shape, F32)

    buf[SUBLANES:SUBLANES + c, :] = xbc_ref[...]
    xbc = _silu(_causal_conv(buf, cw_ref, SSM_CONV, c) + cb_ref[...])

    @pl.when(ci == last_chunk)
    def _():
        csto_ref[...] = buf[SUBLANES + vend - hist:SUBLANES + vend, :]

    buf[0:SUBLANES, :] = buf[c:c + SUBLANES, :]

    row = lax.broadcasted_iota(jnp.int32, (c, SLAB), 0) + ci * c
    valid = jnp.logical_and(row >= v0, row < v1)
    dt = jnp.where(valid, _softplus(sm_ref[...] + bias_ref[...]), 0.0)
    neg_a = -jnp.exp(alog_ref[...])
    acum = _cumsum_rows(dt * neg_a)
    e_acum = jnp.exp(acum)
    last = acum[c - 1:c, :]
    e_last = jnp.exp(last)
    w_last = jnp.exp(last - acum) * dt

    w2 = 2 * c
    lane2 = lax.broadcasted_iota(jnp.int32, (c, w2), 1)
    row2 = lax.broadcasted_iota(jnp.int32, (c, w2), 0)
    half2 = lane2 < c
    j2 = jnp.where(half2, lane2, lane2 - c)
    causal2 = j2 <= row2
    upto2 = row2 <= j2
    eye2 = j2 == row2
    half_p = lax.broadcasted_iota(jnp.int32, (c, LANES), 1) < SSM_HEAD_DIM

    for g in range(SSM_GROUPS):
        b_g = xbc[:, D_INNER + g * SSM_STATE:D_INNER + (g + 1) * SSM_STATE]
        c_g = xbc[:, D_INNER + SSM_BC + g * SSM_STATE:D_INNER + SSM_BC + (g + 1) * SSM_STATE]
        gcols = slice(g * GROUP_COLS, (g + 1) * GROUP_COLS)
        cb = _dot_nt(c_g, b_g)
        cb2 = jnp.concatenate([cb, cb], axis=1)
        y_off = _dot(c_g, ht[:, gcols])
        ys, xws, els = [], [], []
        for q in range(PAIRS_PER_GROUP):
            p = g * PAIRS_PER_GROUP + q
            dx = _pair_expand(dt, p, w2, half2)
            ax = dx * _pair_expand(neg_a, p, w2, half2[0:1])
            row_cum = jnp.sum(jnp.where(upto2, ax, 0.0), axis=0, keepdims=True)
            dt_row = jnp.sum(jnp.where(eye2, dx, 0.0), axis=0, keepdims=True)
            col_cum = _pair_expand(acum, p, w2, half2)
            decay = jnp.where(causal2, jnp.exp(col_cum - row_cum), 0.0)
            wgt = decay * cb2 * dt_row
            xp = xbc[:, p * LANES:(p + 1) * LANES]
            rhs = jnp.concatenate([jnp.where(half_p, xp, 0.0), jnp.where(half_p, 0.0, xp)], axis=0)
            y = _dot(wgt, rhs)
            y = y + y_off[:, q * LANES:(q + 1) * LANES] * _pair_expand(e_acum, p, LANES, half_p)
            y = y + dexp_ref[:, p * LANES:(p + 1) * LANES] * xp
            ys.append(y)
            xws.append(xp * _pair_expand(w_last, p, LANES, half_p))
            els.append(_pair_expand(e_last, p, LANES, half_p[0:1]))
        upd = _dot_tn(b_g, jnp.concatenate(xws, axis=1))
        ht[:, gcols] = ht[:, gcols] * jnp.concatenate(els, axis=1) + upd
        zg = z_ref[:, gcols]
        yz = jnp.concatenate(ys, axis=1) * _silu(zg)
        y_ref[:, gcols] = _rms(yz, nw_ref[:, gcols]).astype(y_ref.dtype)

    @pl.when(ci == last_chunk)
    def _():
        for p in range(SSM_PAIRS):
            sto_ref[p * LANES:(p + 1) * LANES, :] = ht[:, p * LANES:(p + 1) * LANES].T


def _ssd(proj3, slab3, st, cst, prm, layer, *, c, v0, v1):
    b, lp, _ = proj3.shape
    nc = lp // c
    vend = v1 - (nc - 1) * c
    assert SSM_CONV - 1 <= vend <= c
    has_state = st is not None
    lyr = lambda bi, i: (layer, 0, 0)
    per_b = lambda bi, i: (bi, 0, 0)
    in_specs = [pl.BlockSpec((None, c, SSM_XBC), lambda bi, i: (bi, i, 0)),
                pl.BlockSpec((None, c, D_INNER), lambda bi, i: (bi, i, Z_BLOCK)),
                pl.BlockSpec((None, c, SLAB), lambda bi, i: (bi, i, 0))]
    args = [proj3, proj3, slab3]
    if has_state:
        in_specs += [pl.BlockSpec((None, D_INNER, SSM_STATE), per_b),
                     pl.BlockSpec((None, SSM_CONV - 1, SSM_XBC), per_b)]
        args += [st, cst]
    in_specs += [pl.BlockSpec((None, SSM_CONV, SSM_XBC), lyr),
                 pl.BlockSpec((None, 1, SSM_XBC), lyr),
                 pl.BlockSpec((None, 1, SLAB), lyr),
                 pl.BlockSpec((None, 1, SLAB), lyr),
                 pl.BlockSpec((None, 1, D_INNER), lyr),
                 pl.BlockSpec((None, 1, D_INNER), lyr)]
    args += [prm["ssm_conv_w"], prm["ssm_conv_b"], prm["slab_bias"], prm["slab_alog"], prm["ssm_d_cols"],
             prm["ssm_norm_w"]]
    return pl.pallas_call(
        functools.partial(_ssd_kernel, c=c, v0=v0, v1=v1, vend=vend, has_state=has_state),
        grid=(b, nc),
        in_specs=in_specs,
        out_specs=[pl.BlockSpec((None, c, D_INNER), lambda bi, i: (bi, i, 0)),
                   pl.BlockSpec((None, D_INNER, SSM_STATE), per_b),
                   pl.BlockSpec((None, SSM_CONV - 1, SSM_XBC), per_b)],
        out_shape=[jax.ShapeDtypeStruct((b, lp, D_INNER), BF16),
                   jax.ShapeDtypeStruct((b, D_INNER, SSM_STATE), F32),
                   jax.ShapeDtypeStruct((b, SSM_CONV - 1, SSM_XBC), F32)],
        scratch_shapes=[pltpu.VMEM((SUBLANES + c, SSM_XBC), F32),
                        pltpu.VMEM((SSM_STATE, D_INNER), F32)],
        compiler_params=pltpu.CompilerParams(dimension_semantics=("parallel", "arbitrary"),
                                             vmem_limit_bytes=VMEM_LIMIT),
        name="ssd_scan",
    )(*args)


def _split_bf16(x):
    hi = x.astype(BF16)
    return hi, (x - hi.astype(F32)).astype(BF16)


def _head_products(x, y, c):
    per_op = min(GDN_HEADS, MXU_DIM // c)
    width = per_op * c
    lane = lax.broadcasted_iota(jnp.int32, (c, width), 1)
    outs = []
    for s in range(GDN_HEADS // per_op):
        xs = x[:, s * width:(s + 1) * width]
        ys = y[:, s * width:(s + 1) * width]
        diag = jnp.concatenate(
            [jnp.where(lane >= r * c, jnp.where(lane < (r + 1) * c, ys, 0.0), 0.0) for r in range(per_op)], axis=0)
        xh, xl = _split_bf16(xs)
        dh, dl = _split_bf16(diag)
        outs.append(jnp.dot(xh, dh, preferred_element_type=F32)
                    + (jnp.dot(xh, dl, preferred_element_type=F32) + jnp.dot(xl, dh, preferred_element_type=F32)))
    return outs[0] if len(outs) == 1 else jnp.concatenate(outs, axis=1)


def _unit_lower_inverse(a, eye, c):
    n = -a
    t = eye + n
    m = n
    covered = 2
    while covered < c:
        m = _head_products(m, m, c)
        t = t + _head_products(t, m, c)
        covered *= 2
    return t


def _gdn_kernel(*refs, c, v0, v1, vend, has_state):
    if has_state:
        (qkv_ref, gate_ref, sm_ref, st_ref, cst_ref, cw_ref, bias_ref, alog_ref, nw_ref,
         y_ref, sto_ref, csto_ref, buf) = refs
    else:
        (qkv_ref, gate_ref, sm_ref, cw_ref, bias_ref, alog_ref, nw_ref,
         y_ref, sto_ref, csto_ref, buf) = refs
    ci = pl.program_id(1)
    last_chunk = pl.num_programs(1) - 1
    hist = GDN_CONV - 1

    @pl.when(ci == 0)
    def _():
        buf[0:SUBLANES, :] = jnp.zeros((SUBLANES, GDN_QKV), F32)
        if has_state:
            buf[SUBLANES - hist:SUBLANES, :] = cst_ref[...]
            sto_ref[...] = st_ref[...]
        else:
            sto_ref[...] = jnp.zeros(sto_ref.shape, F32)

    buf[SUBLANES:SUBLANES + c, :] = qkv_ref[...]
    qkv = _silu(_causal_conv(buf, cw_ref, GDN_CONV, c))

    @pl.when(ci == last_chunk)
    def _():
        csto_ref[...] = buf[SUBLANES + vend - hist:SUBLANES + vend, :]

    buf[0:SUBLANES, :] = buf[c:c + SUBLANES, :]

    row = lax.broadcasted_iota(jnp.int32, (c, SLAB), 0) + ci * c
    valid = jnp.logical_and(row >= v0, row < v1)
    sm = sm_ref[...]
    g = jnp.where(valid, -jnp.exp(alog_ref[...]) * _softplus(sm + bias_ref[...]), 0.0)
    beta = jnp.where(valid, _sigmoid(sm), 0.0)
    gcum = _cumsum_rows(g)
    e_g = jnp.exp(gcum)
    g_last = gcum[c - 1:c, :]
    e_last = jnp.exp(g_last)
    k_dec = jnp.exp(g_last - gcum)

    w8 = GDN_HEADS * c
    lane8 = lax.broadcasted_iota(jnp.int32, (c, w8), 1)
    row8 = lax.broadcasted_iota(jnp.int32, (c, w8), 0)
    j8 = lane8 & (c - 1)
    causal8 = j8 <= row8
    strict8 = j8 < row8
    eye8 = jnp.where(j8 == row8, 1.0, 0.0)

    def heads_on_lanes(arr):
        return jnp.concatenate(
            [jnp.broadcast_to(arr[:, SLAB_GA + h:SLAB_GA + h + 1], (c, c)) for h in range(GDN_HEADS)], axis=1)

    row_cum = jnp.sum(jnp.where(row8 <= j8, heads_on_lanes(g), 0.0), axis=0, keepdims=True)
    decay = jnp.where(causal8, jnp.exp(heads_on_lanes(gcum) - row_cum), 0.0)

    qs, ks, kbs = [], [], []
    for h in range(GDN_HEADS):
        qh = qkv[:, h * GDN_DK:(h + 1) * GDN_DK]
        kh = qkv[:, GDN_QK + h * GDN_DK:GDN_QK + (h + 1) * GDN_DK]
        qs.append(qh * lax.rsqrt(jnp.sum(qh * qh, axis=-1, keepdims=True) + EPS) * (GDN_DK ** -0.5))
        kn = kh * lax.rsqrt(jnp.sum(kh * kh, axis=-1, keepdims=True) + EPS)
        ks.append(kn)
        kbs.append(kn * beta[:, SLAB_GB + h:SLAB_GB + h + 1])

    lane_k = lax.broadcasted_iota(jnp.int32, (c, 2 * GDN_DK), 1) < GDN_DK
    kk_parts, qk_parts = [], []
    for hp in range(GDN_HEADS // 2):
        k2 = jnp.concatenate([ks[2 * hp], ks[2 * hp + 1]], axis=1)
        rhs_t = jnp.concatenate([jnp.where(lane_k, k2, 0.0), jnp.where(lane_k, 0.0, k2)], axis=0)
        lhs = jnp.concatenate([jnp.concatenate([kbs[2 * hp], kbs[2 * hp + 1]], axis=1),
                               jnp.concatenate([qs[2 * hp], qs[2 * hp + 1]], axis=1)], axis=0)
        res = _dot_nt(lhs, rhs_t)
        kk_parts.append(res[0:c, :])
        qk_parts.append(res[c:2 * c, :])
    a_mat = jnp.where(strict8, jnp.concatenate(kk_parts, axis=1) * decay, 0.0)
    attn = jnp.concatenate(qk_parts, axis=1) * decay
    t_mat = _unit_lower_inverse(a_mat, eye8, c)

    for h in range(GDN_HEADS):
        beta_h = beta[:, SLAB_GB + h:SLAB_GB + h + 1]
        eg_h = e_g[:, SLAB_GA + h:SLAB_GA + h + 1]
        vh = qkv[:, 2 * GDN_QK + h * GDN_DV:2 * GDN_QK + (h + 1) * GDN_DV]
        rhs = jnp.concatenate([vh * beta_h, kbs[h] * eg_h], axis=1)
        sol = _dot(t_mat[:, h * c:(h + 1) * c], rhs)
        s_old = sto_ref[h]
        v_new = sol[:, 0:GDN_DV] - _dot(sol[:, GDN_DV:], s_old)
        o = _dot(qs[h] * eg_h, s_old) + _dot(attn[:, h * c:(h + 1) * c], v_new)
        sto_ref[h] = (s_old * e_last[:, SLAB_GA + h:SLAB_GA + h + 1]
                      + _dot_tn(ks[h] * k_dec[:, SLAB_GA + h:SLAB_GA + h + 1], v_new))
        gt = gate_ref[:, h * GDN_DV:(h + 1) * GDN_DV]
        y_ref[:, h * GDN_DV:(h + 1) * GDN_DV] = (_rms(o, nw_ref[...]) * _silu(gt)).astype(y_ref.dtype)


def _gdn(proj3, slab3, st, cst, prm, layer, *, c, v0, v1):
    b, lp, _ = proj3.shape
    nc = lp // c
    vend = v1 - (nc - 1) * c
    assert GDN_CONV - 1 <= vend <= c
    has_state = st is not None
    lyr = lambda bi, i: (layer, 0, 0)
    per_b = lambda bi, i: (bi, 0, 0)
    per_b4 = lambda bi, i: (bi, 0, 0, 0)
    gd = GDN_HEADS * GDN_DV
    in_specs = [pl.BlockSpec((None, c, GDN_QKV), lambda bi, i: (bi, i, 1)),
                pl.BlockSpec((None, c, gd), lambda bi, i: (bi, i, GATE_BLOCK)),
                pl.BlockSpec((None, c, SLAB), lambda bi, i: (bi, i, 0))]
    args = [proj3, proj3, slab3]
    if has_state:
        in_specs += [pl.BlockSpec((None, GDN_HEADS, GDN_DK, GDN_DV), per_b4),
                     pl.BlockSpec((None, GDN_CONV - 1, GDN_QKV), per_b)]
        args += [st, cst]
    in_specs += [pl.BlockSpec((None, GDN_CONV, GDN_QKV), lyr),
                 pl.BlockSpec((None, 1, SLAB), lyr),
                 pl.BlockSpec((None, 1, SLAB), lyr),
                 pl.BlockSpec((None, 1, GDN_DV), lyr)]
    args += [prm["gdn_conv_w"], prm["slab_bias"], prm["slab_alog"], prm["gdn_norm_w"]]
    return pl.pallas_call(
        functools.partial(_gdn_kernel, c=c, v0=v0, v1=v1, vend=vend, has_state=has_state),
        grid=(b, nc),
        in_specs=in_specs,
        out_specs=[pl.BlockSpec((None, c, gd), lambda bi, i: (bi, i, 0)),
                   pl.BlockSpec((None, GDN_HEADS, GDN_DK, GDN_DV), per_b4),
                   pl.BlockSpec((None, GDN_CONV - 1, GDN_QKV), per_b)],
        out_shape=[jax.ShapeDtypeStruct((b, lp, gd), BF16),
                   jax.ShapeDtypeStruct((b, GDN_HEADS, GDN_DK, GDN_DV), F32),
                   jax.ShapeDtypeStruct((b, GDN_CONV - 1, GDN_QKV), F32)],
        scratch_shapes=[pltpu.VMEM((SUBLANES + c, GDN_QKV), F32)],
        compiler_params=pltpu.CompilerParams(dimension_semantics=("parallel", "arbitrary"),
                                             vmem_limit_bytes=VMEM_LIMIT),
        name="gdn_scan",
    )(*args)


def _mix_kernel(x_ref, ys_ref, yg_ref, ms_ref, mg_ref, wso_ref, wgo_ref, wo_ref, o_ref):
    y_ssm = jnp.dot(ys_ref[...], wso_ref[...], preferred_element_type=F32)
    y_gdn = jnp.dot(yg_ref[...], wgo_ref[...], preferred_element_type=F32)
    merged = _sigmoid(ms_ref[...]) * y_ssm + _sigmoid(mg_ref[...]) * y_gdn
    o_ref[...] = x_ref[...] + _dot(merged, wo_ref[...])


def _mix(x, y_ssm, y_gdn, proj, prm, layer):
    t, d = x.shape
    tm = _pick_tile(t, 512, 16)
    lyr = lambda i: (layer, 0, 0)
    return pl.pallas_call(
        _mix_kernel,
        grid=(t // tm,),
        in_specs=[pl.BlockSpec((tm, d), lambda i: (i, 0)),
                  pl.BlockSpec((tm, D_INNER), lambda i: (i, 0)),
                  pl.BlockSpec((tm, GDN_HEADS * GDN_DV), lambda i: (i, 0)),
                  pl.BlockSpec((tm, d), lambda i: (i, GATE_BLOCK + 1)),
                  pl.BlockSpec((tm, d), lambda i: (i, GATE_BLOCK + 2)),
                  pl.BlockSpec((None, D_INNER, d), lyr),
                  pl.BlockSpec((None, GDN_HEADS * GDN_DV, d), lyr),
                  pl.BlockSpec((None, d, d), lyr)],
        out_specs=pl.BlockSpec((tm, d), lambda i: (i, 0)),
        out_shape=jax.ShapeDtypeStruct((t, d), F32),
        compiler_params=pltpu.CompilerParams(dimension_semantics=("parallel",), vmem_limit_bytes=VMEM_LIMIT),
        name="mix_out",
    )(x, y_ssm, y_gdn, proj, proj, prm["w_ssm_out"], prm["w_gdn_out"], prm["w_o"])


def _ffn_kernel(*refs, nb, rows, vend, has_state):
    if has_state:
        (x_ref, cst_ref, nw_ref, wup_ref, cw_ref, cb_ref, wdn_ref, nnw_ref, xo_ref, hn_ref, csto_ref, buf) = refs
    else:
        (x_ref, nw_ref, wup_ref, cw_ref, cb_ref, wdn_ref, nnw_ref, xo_ref, hn_ref, csto_ref, buf) = refs
    ti = pl.program_id(1)
    last_tile = pl.num_programs(1) - 1
    hist = FFN_CONV - 1

    @pl.when(ti == 0)
    def _():
        buf[:, 0:SUBLANES, :] = jnp.zeros((nb, SUBLANES, D_FF), F32)
        if has_state:
            buf[:, SUBLANES - hist:SUBLANES, :] = cst_ref[...]

    x = x_ref[...]
    gu = _dot(_rms(x, nw_ref[...]), wup_ref[...])
    buf[:, SUBLANES:SUBLANES + rows, :] = gu[:, 0:D_FF].reshape(nb, rows, D_FF)
    conv = _causal_conv(buf, cw_ref, FFN_CONV, rows)

    @pl.when(ti == last_tile)
    def _():
        csto_ref[...] = buf[:, SUBLANES + vend - hist:SUBLANES + vend, :]

    buf[:, 0:SUBLANES, :] = buf[:, rows:rows + SUBLANES, :]
    act = _silu(conv + cb_ref[...]).reshape(nb * rows, D_FF) * gu[:, D_FF:]
    out = x + _dot(act, wdn_ref[...])
    xo_ref[...] = out
    hn_ref[...] = _rms(out, nnw_ref[...]).astype(hn_ref.dtype)


def _ffn(x, cst, prm, layer, next_norm_w, next_layer, hn_dtype, *, b, lp, v1):
    t, d = x.shape
    has_state = cst is not None
    if lp <= 64:
        rows, nb = lp, _pick_tile(b, max(1, 256 // lp), 1)
    else:
        rows, nb = _pick_tile(lp, 384, 16), 1
    nt = lp // rows
    vend = v1 - (nt - 1) * rows
    assert FFN_CONV - 1 <= vend <= rows
    lyr = lambda bi, i: (layer, 0, 0)
    flat = lambda bi, i: (bi * nt + i, 0)
    per_b = lambda bi, i: (bi, 0, 0)
    in_specs = [pl.BlockSpec((nb * rows, d), flat)]
    args = [x]
    if has_state:
        in_specs.append(pl.BlockSpec((nb, FFN_CONV - 1, D_FF), per_b))
        args.append(cst)
    in_specs += [pl.BlockSpec((None, 1, d), lyr),
                 pl.BlockSpec((None, d, 2 * D_FF), lyr),
                 pl.BlockSpec((None, FFN_CONV, D_FF), lyr),
                 pl.BlockSpec((None, 1, D_FF), lyr),
                 pl.BlockSpec((None, D_FF, d), lyr),
                 pl.BlockSpec((None, 1, d), lambda bi, i: (next_layer, 0, 0))]
    args += [prm["norm_ffn_w"], prm["w_up"], prm["ffn_conv_w"], prm["ffn_conv_b"], prm["w_down"], next_norm_w]
    return pl.pallas_call(
        functools.partial(_ffn_kernel, nb=nb, rows=rows, vend=vend, has_state=has_state),
        grid=(b // nb, nt),
        in_specs=in_specs,
        out_specs=[pl.BlockSpec((nb * rows, d), flat),
                   pl.BlockSpec((nb * rows, d), flat),
                   pl.BlockSpec((nb, FFN_CONV - 1, D_FF), per_b)],
        out_shape=[jax.ShapeDtypeStruct((t, d), F32),
                   jax.ShapeDtypeStruct((t, d), hn_dtype),
                   jax.ShapeDtypeStruct((b, FFN_CONV - 1, D_FF), F32)],
        scratch_shapes=[pltpu.VMEM((nb, SUBLANES + rows, D_FF), F32)],
        compiler_params=pltpu.CompilerParams(dimension_semantics=("parallel", "arbitrary"),
                                             vmem_limit_bytes=VMEM_LIMIT),
        name="conv_ffn",
    )(*args)


def _prepare_params(norm_mix_w, w_in, ssm_conv_w, ssm_conv_b, ssm_dt_bias, ssm_a_log, ssm_d, ssm_norm_w,
                    gdn_conv_w, gdn_dt_bias, gdn_a_log, gdn_norm_w, w_ssm_out, w_gdn_out, w_o,
                    norm_ffn_w, w_up, ffn_conv_w, ffn_conv_b, w_down, norm_f_w):
    depth = w_in.shape[0]
    offs = [0]
    for s in IN_SIZES:
        offs.append(offs[-1] + s)
    col = lambda k: w_in[:, :, offs[k]:offs[k + 1]]
    w_main = jnp.concatenate([col(1), col(3), col(0), col(6), col(7), col(8)], axis=-1).astype(BF16)
    dt_cols = col(2)
    pad = jnp.zeros((depth, w_in.shape[1], SLAB - SSM_HEADS - 2 * GDN_HEADS), w_in.dtype)
    w_slab = jnp.concatenate([dt_cols[..., 0::2], dt_cols[..., 1::2], col(4), col(5), pad], axis=-1).astype(BF16)

    def slab(ssm_vec, gdn_vec):
        z = jnp.zeros((depth, SLAB - SSM_HEADS - GDN_HEADS), F32)
        return jnp.concatenate([ssm_vec[:, 0::2], ssm_vec[:, 1::2], gdn_vec, z], axis=-1)[:, None, :]

    return {
        "norm_mix_w": norm_mix_w[:, None, :],
        "w_main": w_main,
        "w_slab": w_slab,
        "ssm_conv_w": ssm_conv_w,
        "ssm_conv_b": ssm_conv_b[:, None, :],
        "slab_bias": slab(ssm_dt_bias, gdn_dt_bias),
        "slab_alog": slab(ssm_a_log, gdn_a_log),
        "ssm_d_cols": jnp.repeat(ssm_d, SSM_HEAD_DIM, axis=-1)[:, None, :],
        "ssm_norm_w": ssm_norm_w[:, None, :],
        "gdn_conv_w": gdn_conv_w,
        "gdn_norm_w": gdn_norm_w[:, None, :],
        "w_ssm_out": w_ssm_out.astype(BF16),
        "w_gdn_out": w_gdn_out.astype(BF16),
        "w_o": w_o.astype(BF16),
        "norm_ffn_w": norm_ffn_w[:, None, :],
        "w_up": w_up.astype(BF16),
        "ffn_conv_w": ffn_conv_w,
        "ffn_conv_b": ffn_conv_b[:, None, :],
        "w_down": w_down.astype(BF16),
        "norm_f_w": norm_f_w[None, None, :],
    }


def _run_trunk(x3, states, prm, *, c, v0, v1):
    b, lp, d = x3.shape
    depth = prm["w_main"].shape[0]
    x = x3.reshape(b * lp, d)
    h = _norm(x, prm["norm_mix_w"], 0, BF16)
    outs = [[] for _ in range(5)]
    for l in range(depth):
        proj, slab = _inproj(h, prm["w_main"], prm["w_slab"], l)
        proj3 = proj.reshape(b, lp, PROJ_COLS)
        slab3 = slab.reshape(b, lp, SLAB)
        if states is None:
            st_ssm = cst_ssm = st_gdn = cst_gdn = cst_ffn = None
        else:
            st_ssm = states[0][l].reshape(b, D_INNER, SSM_STATE)
            cst_ssm, st_gdn, cst_gdn, cst_ffn = states[1][l], states[2][l], states[3][l], states[4][l]
        y_ssm, o_ssm, o_ssm_conv = _ssd(proj3, slab3, st_ssm, cst_ssm, prm, l, c=c, v0=v0, v1=v1)
        y_gdn, o_gdn, o_gdn_conv = _gdn(proj3, slab3, st_gdn, cst_gdn, prm, l, c=c, v0=v0, v1=v1)
        x = _mix(x, y_ssm.reshape(b * lp, D_INNER), y_gdn.reshape(b * lp, GDN_HEADS * GDN_DV), proj, prm, l)
        if l + 1 < depth:
            x, h, o_ffn_conv = _ffn(x, cst_ffn, prm, l, prm["norm_mix_w"], l + 1, BF16, b=b, lp=lp, v1=v1)
        else:
            x, h, o_ffn_conv = _ffn(x, cst_ffn, prm, l, prm["norm_f_w"], 0, F32, b=b, lp=lp, v1=v1)
        outs[0].append(o_ssm.reshape(b, SSM_HEADS, SSM_HEAD_DIM, SSM_STATE))
        outs[1].append(o_ssm_conv)
        outs[2].append(o_gdn)
        outs[3].append(o_gdn_conv)
        outs[4].append(o_ffn_conv)
    return (h.reshape(b, lp, d),) + tuple(jnp.stack(o) for o in outs)


def kernel(x_prompt, x_sample, state_ssm, state_ssm_conv, state_gdn, state_gdn_conv, state_ffn_conv, meta_tokens, norm_mix_w, w_in, ssm_conv_w, ssm_conv_b, ssm_dt_bias, ssm_a_log, ssm_d, ssm_norm_w, gdn_conv_w, gdn_dt_bias, gdn_a_log, gdn_norm_w, w_ssm_out, w_gdn_out, w_o, norm_ffn_w, w_up, ffn_conv_w, ffn_conv_b, w_down, norm_f_w):
    prm = _prepare_params(norm_mix_w, w_in, ssm_conv_w, ssm_conv_b, ssm_dt_bias, ssm_a_log, ssm_d, ssm_norm_w,
                          gdn_conv_w, gdn_dt_bias, gdn_a_log, gdn_norm_w, w_ssm_out, w_gdn_out, w_o,
                          norm_ffn_w, w_up, ffn_conv_w, ffn_conv_b, w_down, norm_f_w)
    bp, sp, d = x_prompt.shape
    bs, ls, _ = x_sample.shape
    c_p = SCAN_CHUNK
    lp = -(-(N_META + sp) // c_p) * c_p
    pad = lp - N_META - sp
    xp = jnp.concatenate([jnp.zeros((bp, pad, d), x_prompt.dtype),
                          jnp.broadcast_to(meta_tokens.astype(x_prompt.dtype), (bp, N_META, d)), x_prompt], axis=1)
    p_out = _run_trunk(xp, None, prm, c=c_p, v0=pad, v1=lp)
    y_prompt = p_out[0][:, pad + N_META:]
    c_s = -(-ls // SUBLANES) * SUBLANES
    xs = jnp.concatenate([x_sample, jnp.zeros((bs, c_s - ls, d), x_sample.dtype)], axis=1)
    s_out = _run_trunk(xs, (state_ssm, state_ssm_conv, state_gdn, state_gdn_conv, state_ffn_conv), prm,
                       c=c_s, v0=0, v1=ls)
    y_sample = s_out[0][:, :ls]
    return (y_prompt,) + (y_sample,) + p_out[1:] + s_out[1:]
```

```python
import functools

import jax
import jax.numpy as jnp
from jax import lax
from jax.experimental import pallas as pl
from jax.experimental.pallas import tpu as pltpu

F32 = jnp.float32
BF16 = jnp.bfloat16
HIGHEST = lax.Precision.HIGHEST

EPS = 1e-6
N_META = 16
D_MODEL = 1024
SSM_HEADS = 32
SSM_HEAD_DIM = 64
SSM_GROUPS = 4
SSM_STATE = 128
SSM_CONV = 4
D_INNER = SSM_HEADS * SSM_HEAD_DIM
SSM_BC = SSM_GROUPS * SSM_STATE
SSM_XBC = D_INNER + 2 * SSM_BC
SSM_PAIRS = SSM_HEADS // 2
PAIRS_PER_GROUP = SSM_PAIRS // SSM_GROUPS
GROUP_COLS = D_INNER // SSM_GROUPS
GDN_HEADS = 8
GDN_DK = 128
GDN_DV = 128
GDN_CONV = 4
GDN_QK = GDN_HEADS * GDN_DK
GDN_QKV = 2 * GDN_QK + GDN_HEADS * GDN_DV
D_FF = 2816
FFN_CONV = 3
IN_SIZES = (D_INNER, SSM_XBC, SSM_HEADS, GDN_QKV, GDN_HEADS, GDN_HEADS, GDN_HEADS * GDN_DV, D_MODEL, D_MODEL)
PROJ_COLS = SSM_XBC + GDN_QKV + D_INNER + 3 * D_MODEL
Z_BLOCK = (SSM_XBC + GDN_QKV) // D_INNER
GATE_BLOCK = (SSM_XBC + GDN_QKV + D_INNER) // D_MODEL
SLAB = 128
SLAB_GA = SSM_HEADS
SLAB_GB = SSM_HEADS + GDN_HEADS

LANES = 128
SUBLANES = 8
MXU_DIM = 256
SCAN_CHUNK = 64
VMEM_LIMIT = 56 * 1024 * 1024
SCAN_SEQS_LONG = 2
SCAN_SEQS_SHORT = 4


def _pick_tile(total, cap, mult):
    best = None
    for t in range(mult, min(total, cap) + 1, mult):
        if total % t == 0:
            best = t
    assert best is not None, (total, cap, mult)
    return best


def _sigmoid(x):
    return 1.0 / (1.0 + jnp.exp(-x))


def _silu(x):
    return x * _sigmoid(x)


def _softplus(x):
    return jnp.maximum(x, 0.0) + jnp.log1p(jnp.exp(-jnp.abs(x)))


def _rms(x, w):
    return x * lax.rsqrt(jnp.mean(x * x, axis=-1, keepdims=True) + EPS) * w


def _dot(a, b):
    return jnp.dot(a.astype(BF16), b.astype(BF16), preferred_element_type=F32)


def _dot_nt(a, b):
    return lax.dot_general(a.astype(BF16), b.astype(BF16), (((1,), (1,)), ((), ())), preferred_element_type=F32)


def _dot_tn(a, b):
    return lax.dot_general(a.astype(BF16), b.astype(BF16), (((0,), (0,)), ((), ())), preferred_element_type=F32)


def _cumsum_rows(x):
    c = x.shape[0]
    tril = (lax.broadcasted_iota(jnp.int32, (c, c), 0) >= lax.broadcasted_iota(jnp.int32, (c, c), 1)).astype(F32)
    return jnp.dot(tril, x, precision=HIGHEST, preferred_element_type=F32)


def _causal_conv(buf, w_ref, width, rows):
    lead = (slice(None),) * (len(buf.shape) - 2)
    acc = None
    for j in range(width):
        off = SUBLANES - (width - 1) + j
        term = buf[lead + (slice(off, off + rows), slice(None))] * w_ref[j:j + 1, :]
        acc = term if acc is None else acc + term
    return acc


def _norm_kernel(x_ref, w_ref, o_ref):
    o_ref[...] = _rms(x_ref[...], w_ref[...]).astype(o_ref.dtype)


def _norm(x, w, layer, out_dtype):
    t, d = x.shape
    tm = _pick_tile(t, 1024, 16)
    return pl.pallas_call(
        _norm_kernel,
        grid=(t // tm,),
        in_specs=[pl.BlockSpec((tm, d), lambda i: (i, 0)),
                  pl.BlockSpec((None, 1, d), lambda i: (layer, 0, 0))],
        out_specs=pl.BlockSpec((tm, d), lambda i: (i, 0)),
        out_shape=jax.ShapeDtypeStruct((t, d), out_dtype),
        compiler_params=pltpu.CompilerParams(dimension_semantics=("parallel",)),
        name="rmsnorm",
    )(x, w)


def _inproj_kernel(h_ref, w_ref, ws_ref, o_ref, os_ref):
    h = h_ref[...]
    o_ref[...] = jnp.dot(h, w_ref[...], preferred_element_type=F32)

    @pl.when(pl.program_id(1) == 0)
    def _():
        os_ref[...] = jnp.dot(h, ws_ref[...], preferred_element_type=F32)


def _inproj(h, w_main, w_slab, layer):
    t, d = h.shape
    ncols = w_main.shape[-1]
    tm = _pick_tile(t, 1536, 16)
    tn = _pick_tile(ncols, 1024, LANES)
    return pl.pallas_call(
        _inproj_kernel,
        grid=(t // tm, ncols // tn),
        in_specs=[pl.BlockSpec((tm, d), lambda i, j: (i, 0)),
                  pl.BlockSpec((None, d, tn), lambda i, j: (layer, 0, j)),
                  pl.BlockSpec((None, d, SLAB), lambda i, j: (layer, 0, 0))],
        out_specs=[pl.BlockSpec((tm, tn), lambda i, j: (i, j)),
                   pl.BlockSpec((tm, SLAB), lambda i, j: (i, 0))],
        out_shape=[jax.ShapeDtypeStruct((t, ncols), F32), jax.ShapeDtypeStruct((t, SLAB), F32)],
        compiler_params=pltpu.CompilerParams(dimension_semantics=("parallel", "arbitrary"),
                                             vmem_limit_bytes=VMEM_LIMIT),
        name="inproj",
    )(h, w_main, w_slab)


def _scan_call(body, name, proj3, slab3, col_blocks, state_in, conv_in, params, prev_state, out_cols,
               state_block, conv_cols, conv_width, scratch, layer, depth, *, c, v0, v1):
    b, lp, _ = proj3.shape
    nc = lp // c
    nb = _pick_tile(b, SCAN_SEQS_LONG if nc > 1 else SCAN_SEQS_SHORT, 1)
    vend = v1 - (nc - 1) * c
    hist = conv_width - 1
    assert hist <= vend <= c
    has_state = state_in is not None
    has_prev = prev_state is not None
    zeros = (0,) * len(state_block)
    lyr = lambda bi, i: (layer, 0, 0)
    in_specs = [pl.BlockSpec((nb, c, width), functools.partial(lambda blk, bi, i: (bi, i, blk), blk))
                for width, blk in col_blocks]
    in_specs.append(pl.BlockSpec((nb, c, SLAB), lambda bi, i: (bi, i, 0)))
    args = [proj3] * len(col_blocks) + [slab3]
    if has_state:
        in_specs += [pl.BlockSpec((None, nb) + state_block, lambda bi, i: (layer, bi) + zeros),
                     pl.BlockSpec((None, nb, hist, conv_cols), lambda bi, i: (layer, bi, 0, 0))]
        args += [state_in, conv_in]
    for p in params:
        in_specs.append(pl.BlockSpec((None,) + p.shape[1:], lyr))
        args.append(p)
    aliases = {}
    if has_prev:
        aliases[len(args)] = 1
        in_specs.append(pl.BlockSpec(memory_space=pl.ANY))
        args.append(prev_state)
    return pl.pallas_call(
        functools.partial(body, nb=nb, c=c, v0=v0, v1=v1, vend=vend, has_state=has_state, has_prev=has_prev),
        grid=(b // nb, nc),
        in_specs=in_specs,
        out_specs=[pl.BlockSpec((nb, c, out_cols), lambda bi, i: (bi, i, 0)),
                   pl.BlockSpec((None, nb) + state_block, lambda bi, i: (layer, bi) + zeros),
                   pl.BlockSpec((nb, hist, conv_cols), lambda bi, i: (bi, 0, 0))],
        out_shape=[jax.ShapeDtypeStruct((b, lp, out_cols), BF16),
                   jax.ShapeDtypeStruct((depth, b) + state_block, F32),
                   jax.ShapeDtypeStruct((b, hist, conv_cols), F32)],
        scratch_shapes=[pltpu.VMEM((nb, SUBLANES + c, conv_cols), F32)] + [s(nb) for s in scratch],
        input_output_aliases=aliases,
        compiler_params=pltpu.CompilerParams(dimension_semantics=("parallel", "arbitrary"),
                                             vmem_limit_bytes=VMEM_LIMIT),
        name=name,
    )(*args)


def _conv_step(buf, x_ref, cst_out, w_ref, width, c, vend):
    buf[SUBLANES:SUBLANES + c, :] = x_ref[...]
    out = _causal_conv(buf, w_ref, width, c)
    cst_out[...] = buf[SUBLANES + vend - (width - 1):SUBLANES + vend, :]
    buf[0:SUBLANES, :] = buf[c:c + SUBLANES, :]
    return out


def _conv_init(buf, cst_ref, width, cols):
    buf[0:SUBLANES, :] = jnp.zeros((SUBLANES, cols), F32)
    if cst_ref is not None:
        buf[SUBLANES - (width - 1):SUBLANES, :] = cst_ref[...]


def _pair_expand(arr, p, width, half):
    r = arr.shape[0]
    even = jnp.broadcast_to(arr[:, p:p + 1], (r, width))
    odd = jnp.broadcast_to(arr[:, SSM_PAIRS + p:SSM_PAIRS + p + 1], (r, width))
    return jnp.where(half, even, odd)


def _ssd_kernel(*refs, nb, c, v0, v1, vend, has_state, has_prev):
    refs = list(refs)
    xbc_ref, z_ref, sm_ref = refs[:3]
    k = 3
    st_ref = cst_ref = None
    if has_state:
        st_ref, cst_ref = refs[k:k + 2]
        k += 2
    cw_ref, cb_ref, bias_ref, alog_ref, dexp_ref, nw_ref = refs[k:k + 6]
    k += 6 + (1 if has_prev else 0)
    y_ref, sto_ref, csto_ref, buf, ht = refs[k:]
    ci = pl.program_id(1)

    @pl.when(ci == 0)
    def _():
        for s in range(nb):
            _conv_init(buf.at[s], cst_ref.at[s] if has_state else None, SSM_CONV, SSM_XBC)
            if has_state:
                for p in range(SSM_PAIRS):
                    ht[s, :, p * LANES:(p + 1) * LANES] = st_ref[s, p * LANES:(p + 1) * LANES, :].T
            else:
                ht[s] = jnp.zeros(ht.shape[1:], F32)

    row = lax.broadcasted_iota(jnp.int32, (c, SLAB), 0) + ci * c
    valid = jnp.logical_and(row >= v0, row < v1)
    neg_a = -jnp.exp(alog_ref[...])
    w2 = 2 * c
    lane2 = lax.broadcasted_iota(jnp.int32, (c, w2), 1)
    row2 = lax.broadcasted_iota(jnp.int32, (c, w2), 0)
    half2 = lane2 < c
    j2 = jnp.where(half2, lane2, lane2 - c)
    causal2 = j2 <= row2
    upto2 = row2 <= j2
    eye2 = j2 == row2
    half_p = lax.broadcasted_iota(jnp.int32, (c, LANES), 1) < SSM_HEAD_DIM

    for s in range(nb):
        xbc = _silu(_conv_step(buf.at[s], xbc_ref.at[s], csto_ref.at[s], cw_ref, SSM_CONV, c, vend) + cb_ref[...])
        dt = jnp.where(valid, _softplus(sm_ref[s] + bias_ref[...]), 0.0)
        acum = _cumsum_rows(dt * neg_a)
        e_acum = jnp.exp(acum)
        last = acum[c - 1:c, :]
        e_last = jnp.exp(last)
        w_last = jnp.exp(last - acum) * dt
        for g in range(SSM_GROUPS):
            b_g = xbc[:, D_INNER + g * SSM_STATE:D_INNER + (g + 1) * SSM_STATE]
            c_g = xbc[:, D_INNER + SSM_BC + g * SSM_STATE:D_INNER + SSM_BC + (g + 1) * SSM_STATE]
            gcols = slice(g * GROUP_COLS, (g + 1) * GROUP_COLS)
            cb = _dot_nt(c_g, b_g)
            cb2 = jnp.concatenate([cb, cb], axis=1)
            y_off = _dot(c_g, ht[s, :, gcols])
            ys, xws, els = [], [], []
            for q in range(PAIRS_PER_GROUP):
                p = g * PAIRS_PER_GROUP + q
                dx = _pair_expand(dt, p, w2, half2)
                ax = dx * _pair_expand(neg_a, p, w2, half2[0:1])
                row_cum = jnp.sum(jnp.where(upto2, ax, 0.0), axis=0, keepdims=True)
                dt_row = jnp.sum(jnp.where(eye2, dx, 0.0), axis=0, keepdims=True)
                col_cum = _pair_expand(acum, p, w2, half2)
                decay = jnp.where(causal2, jnp.exp(col_cum - row_cum), 0.0)
                wgt = decay * cb2 * dt_row
                xp = xbc[:, p * LANES:(p + 1) * LANES]
                rhs = jnp.concatenate([jnp.where(half_p, xp, 0.0), jnp.where(half_p, 0.0, xp)], axis=0)
                y = _dot(wgt, rhs)
                y = y + y_off[:, q * LANES:(q + 1) * LANES] * _pair_expand(e_acum, p, LANES, half_p)
                y = y + dexp_ref[:, p * LANES:(p + 1) * LANES] * xp
                ys.append(y)
                xws.append(xp * _pair_expand(w_last, p, LANES, half_p))
                els.append(_pair_expand(e_last, p, LANES, half_p[0:1]))
            upd = _dot_tn(b_g, jnp.concatenate(xws, axis=1))
            ht[s, :, gcols] = ht[s, :, gcols] * jnp.concatenate(els, axis=1) + upd
            yz = jnp.concatenate(ys, axis=1) * _silu(z_ref[s, :, gcols])
            y_ref[s, :, gcols] = _rms(yz, nw_ref[:, gcols]).astype(y_ref.dtype)

    @pl.when(ci == pl.num_programs(1) - 1)
    def _():
        for s in range(nb):
            for p in range(SSM_PAIRS):
                sto_ref[s, p * LANES:(p + 1) * LANES, :] = ht[s, :, p * LANES:(p + 1) * LANES].T


def _ssd(proj3, slab3, st, cst, prev, prm, layer, depth, *, c, v0, v1):
    params = [prm["ssm_conv_w"], prm["ssm_conv_b"], prm["slab_bias"], prm["slab_alog"], prm["ssm_d_cols"],
              prm["ssm_norm_w"]]
    return _scan_call(_ssd_kernel, "ssd_scan", proj3, slab3, [(SSM_XBC, 0), (D_INNER, Z_BLOCK)], st, cst, params,
                      prev, D_INNER, (D_INNER, SSM_STATE), SSM_XBC, SSM_CONV,
                      [lambda nb: pltpu.VMEM((nb, SSM_STATE, D_INNER), F32)], layer, depth, c=c, v0=v0, v1=v1)


def _head_products(x, y, c):
    per_op = min(GDN_HEADS, MXU_DIM // c)
    width = per_op * c
    lane = lax.broadcasted_iota(jnp.int32, (c, width), 1)
    outs = []
    for s in range(GDN_HEADS // per_op):
        ys = y[:, s * width:(s + 1) * width]
        diag = jnp.concatenate(
            [jnp.where(lane >= r * c, jnp.where(lane < (r + 1) * c, ys, 0.0), 0.0) for r in range(per_op)], axis=0)
        outs.append(_dot(x[:, s * width:(s + 1) * width], diag))
    return outs[0] if len(outs) == 1 else jnp.concatenate(outs, axis=1)


def _unit_lower_inverse(a, eye, c):
    n = -a
    t = eye + n
    m = n
    covered = 2
    while covered < c:
        m = _head_products(m, m, c)
        t = t + _head_products(t, m, c)
        covered *= 2
    return t


def _gdn_kernel(*refs, nb, c, v0, v1, vend, has_state, has_prev):
    refs = list(refs)
    qkv_ref, gate_ref, sm_ref = refs[:3]
    k = 3
    st_ref = cst_ref = None
    if has_state:
        st_ref, cst_ref = refs[k:k + 2]
        k += 2
    cw_ref, bias_ref, alog_ref, nw_ref = refs[k:k + 4]
    k += 4 + (1 if has_prev else 0)
    y_ref, sto_ref, csto_ref, buf = refs[k:]
    ci = pl.program_id(1)

    @pl.when(ci == 0)
    def _():
        for s in range(nb):
            _conv_init(buf.at[s], cst_ref.at[s] if has_state else None, GDN_CONV, GDN_QKV)
            sto_ref[s] = st_ref[s] if has_state else jnp.zeros(sto_ref.shape[1:], F32)

    row = lax.broadcasted_iota(jnp.int32, (c, SLAB), 0) + ci * c
    valid = jnp.logical_and(row >= v0, row < v1)
    neg_a = -jnp.exp(alog_ref[...])
    w8 = GDN_HEADS * c
    lane8 = lax.broadcasted_iota(jnp.int32, (c, w8), 1)
    row8 = lax.broadcasted_iota(jnp.int32, (c, w8), 0)
    j8 = lane8 & (c - 1)
    causal8 = j8 <= row8
    strict8 = j8 < row8
    upto8 = row8 <= j8
    eye8 = jnp.where(j8 == row8, 1.0, 0.0)
    lane_k = lax.broadcasted_iota(jnp.int32, (c, 2 * GDN_DK), 1) < GDN_DK

    def heads_on_lanes(arr):
        return jnp.concatenate(
            [jnp.broadcast_to(arr[:, SLAB_GA + h:SLAB_GA + h + 1], (c, c)) for h in range(GDN_HEADS)], axis=1)

    for s in range(nb):
        qkv = _silu(_conv_step(buf.at[s], qkv_ref.at[s], csto_ref.at[s], cw_ref, GDN_CONV, c, vend))
        sm = sm_ref[s]
        g = jnp.where(valid, neg_a * _softplus(sm + bias_ref[...]), 0.0)
        beta = jnp.where(valid, _sigmoid(sm), 0.0)
        gcum = _cumsum_rows(g)
        e_g = jnp.exp(gcum)
        g_last = gcum[c - 1:c, :]
        e_last = jnp.exp(g_last)
        k_dec = jnp.exp(g_last - gcum)
        row_cum = jnp.sum(jnp.where(upto8, heads_on_lanes(g), 0.0), axis=0, keepdims=True)
        decay = jnp.where(causal8, jnp.exp(heads_on_lanes(gcum) - row_cum), 0.0)

        qs, ks, kbs = [], [], []
        for h in range(GDN_HEADS):
            qh = qkv[:, h * GDN_DK:(h + 1) * GDN_DK]
            kh = qkv[:, GDN_QK + h * GDN_DK:GDN_QK + (h + 1) * GDN_DK]
            qs.append(qh * lax.rsqrt(jnp.sum(qh * qh, axis=-1, keepdims=True) + EPS) * (GDN_DK ** -0.5))
            kn = kh * lax.rsqrt(jnp.sum(kh * kh, axis=-1, keepdims=True) + EPS)
            ks.append(kn)
            kbs.append(kn * beta[:, SLAB_GB + h:SLAB_GB + h + 1])

        kk_parts, qk_parts = [], []
        for hp in range(GDN_HEADS // 2):
            k2 = jnp.concatenate([ks[2 * hp], ks[2 * hp + 1]], axis=1)
            rhs_t = jnp.concatenate([jnp.where(lane_k, k2, 0.0), jnp.where(lane_k, 0.0, k2)], axis=0)
            lhs = jnp.concatenate([jnp.concatenate([kbs[2 * hp], kbs[2 * hp + 1]], axis=1),
                                   jnp.concatenate([qs[2 * hp], qs[2 * hp + 1]], axis=1)], axis=0)
            res = _dot_nt(lhs, rhs_t)
            kk_parts.append(res[0:c, :])
            qk_parts.append(res[c:2 * c, :])
        a_mat = jnp.where(strict8, jnp.concatenate(kk_parts, axis=1) * decay, 0.0)
        attn = jnp.concatenate(qk_parts, axis=1) * decay
        t_mat = _unit_lower_inverse(a_mat, eye8, c)

        for h in range(GDN_HEADS):
            beta_h = beta[:, SLAB_GB + h:SLAB_GB + h + 1]
            eg_h = e_g[:, SLAB_GA + h:SLAB_GA + h + 1]
            vh = qkv[:, 2 * GDN_QK + h * GDN_DV:2 * GDN_QK + (h + 1) * GDN_DV]
            rhs = jnp.concatenate([vh * beta_h, kbs[h] * eg_h], axis=1)
            sol = _dot(t_mat[:, h * c:(h + 1) * c], rhs)
            s_old = sto_ref[s, h]
            v_new = sol[:, 0:GDN_DV] - _dot(sol[:, GDN_DV:], s_old)
            o = _dot(qs[h] * eg_h, s_old) + _dot(attn[:, h * c:(h + 1) * c], v_new)
            sto_ref[s, h] = (s_old * e_last[:, SLAB_GA + h:SLAB_GA + h + 1]
                             + _dot_tn(ks[h] * k_dec[:, SLAB_GA + h:SLAB_GA + h + 1], v_new))
            gt = gate_ref[s, :, h * GDN_DV:(h + 1) * GDN_DV]
            y_ref[s, :, h * GDN_DV:(h + 1) * GDN_DV] = (_rms(o, nw_ref[...]) * _silu(gt)).astype(y_ref.dtype)


def _gdn(proj3, slab3, st, cst, prev, prm, layer, depth, *, c, v0, v1):
    params = [prm["gdn_conv_w"], prm["slab_bias"], prm["slab_alog"], prm["gdn_norm_w"]]
    return _scan_call(_gdn_kernel, "gdn_scan", proj3, slab3, [(GDN_QKV, 1), (GDN_HEADS * GDN_DV, GATE_BLOCK)],
                      st, cst, params, prev, GDN_HEADS * GDN_DV, (GDN_HEADS, GDN_DK, GDN_DV), GDN_QKV, GDN_CONV,
                      [], layer, depth, c=c, v0=v0, v1=v1)


def _mix_kernel(x_ref, ys_ref, yg_ref, ms_ref, mg_ref, wso_ref, wgo_ref, wo_ref, o_ref):
    y_ssm = jnp.dot(ys_ref[...], wso_ref[...], preferred_element_type=F32)
    y_gdn = jnp.dot(yg_ref[...], wgo_ref[...], preferred_element_type=F32)
    merged = _sigmoid(ms_ref[...]) * y_ssm + _sigmoid(mg_ref[...]) * y_gdn
    o_ref[...] = x_ref[...] + _dot(merged, wo_ref[...])


def _mix(x, y_ssm, y_gdn, proj, prm, layer):
    t, d = x.shape
    tm = _pick_tile(t, 512, 16)
    lyr = lambda i: (layer, 0, 0)
    return pl.pallas_call(
        _mix_kernel,
        grid=(t // tm,),
        in_specs=[pl.BlockSpec((tm, d), lambda i: (i, 0)),
                  pl.BlockSpec((tm, D_INNER), lambda i: (i, 0)),
                  pl.BlockSpec((tm, GDN_HEADS * GDN_DV), lambda i: (i, 0)),
                  pl.BlockSpec((tm, d), lambda i: (i, GATE_BLOCK + 1)),
                  pl.BlockSpec((tm, d), lambda i: (i, GATE_BLOCK + 2)),
                  pl.BlockSpec((None, D_INNER, d), lyr),
                  pl.BlockSpec((None, GDN_HEADS * GDN_DV, d), lyr),
                  pl.BlockSpec((None, d, d), lyr)],
        out_specs=pl.BlockSpec((tm, d), lambda i: (i, 0)),
        out_shape=jax.ShapeDtypeStruct((t, d), F32),
        compiler_params=pltpu.CompilerParams(dimension_semantics=("parallel",), vmem_limit_bytes=VMEM_LIMIT),
        name="mix_out",
    )(x, y_ssm, y_gdn, proj, proj, prm["w_ssm_out"], prm["w_gdn_out"], prm["w_o"])


def _ffn_kernel(*refs, nb, rows, vend, has_state):
    if has_state:
        (x_ref, cst_ref, nw_ref, wup_ref, cw_ref, cb_ref, wdn_ref, nnw_ref, xo_ref, hn_ref, csto_ref, buf) = refs
    else:
        (x_ref, nw_ref, wup_ref, cw_ref, cb_ref, wdn_ref, nnw_ref, xo_ref, hn_ref, csto_ref, buf) = refs
    ti = pl.program_id(1)
    last_tile = pl.num_programs(1) - 1
    hist = FFN_CONV - 1

    @pl.when(ti == 0)
    def _():
        buf[:, 0:SUBLANES, :] = jnp.zeros((nb, SUBLANES, D_FF), F32)
        if has_state:
            buf[:, SUBLANES - hist:SUBLANES, :] = cst_ref[...]

    x = x_ref[...]
    gu = _dot(_rms(x, nw_ref[...]), wup_ref[...])
    buf[:, SUBLANES:SUBLANES + rows, :] = gu[:, 0:D_FF].reshape(nb, rows, D_FF)
    conv = _causal_conv(buf, cw_ref, FFN_CONV, rows)

    @pl.when(ti == last_tile)
    def _():
        csto_ref[...] = buf[:, SUBLANES + vend - hist:SUBLANES + vend, :]

    buf[:, 0:SUBLANES, :] = buf[:, rows:rows + SUBLANES, :]
    act = _silu(conv + cb_ref[...]).reshape(nb * rows, D_FF) * gu[:, D_FF:]
    out = x + _dot(act, wdn_ref[...])
    xo_ref[...] = out
    hn_ref[...] = _rms(out, nnw_ref[...]).astype(hn_ref.dtype)


def _ffn(x, cst, prm, layer, next_norm_w, next_layer, hn_dtype, *, b, lp, v1):
    t, d = x.shape
    has_state = cst is not None
    if lp <= 64:
        rows, nb = lp, _pick_tile(b, max(1, 256 // lp), 1)
    else:
        rows, nb = _pick_tile(lp, 384, 16), 1
    nt = lp // rows
    vend = v1 - (nt - 1) * rows
    assert FFN_CONV - 1 <= vend <= rows
    lyr = lambda bi, i: (layer, 0, 0)
    flat = lambda bi, i: (bi * nt + i, 0)
    per_b = lambda bi, i: (bi, 0, 0)
    in_specs = [pl.BlockSpec((nb * rows, d), flat)]
    args = [x]
    if has_state:
        in_specs.append(pl.BlockSpec((None, nb, FFN_CONV - 1, D_FF), lambda bi, i: (layer, bi, 0, 0)))
        args.append(cst)
    in_specs += [pl.BlockSpec((None, 1, d), lyr),
                 pl.BlockSpec((None, d, 2 * D_FF), lyr),
                 pl.BlockSpec((None, FFN_CONV, D_FF), lyr),
                 pl.BlockSpec((None, 1, D_FF), lyr),
                 pl.BlockSpec((None, D_FF, d), lyr),
                 pl.BlockSpec((None, 1, d), lambda bi, i: (next_layer, 0, 0))]
    args += [prm["norm_ffn_w"], prm["w_up"], prm["ffn_conv_w"], prm["ffn_conv_b"], prm["w_down"], next_norm_w]
    return pl.pallas_call(
        functools.partial(_ffn_kernel, nb=nb, rows=rows, vend=vend, has_state=has_state),
        grid=(b // nb, nt),
        in_specs=in_specs,
        out_specs=[pl.BlockSpec((nb * rows, d), flat),
                   pl.BlockSpec((nb * rows, d), flat),
                   pl.BlockSpec((nb, FFN_CONV - 1, D_FF), per_b)],
        out_shape=[jax.ShapeDtypeStruct((t, d), F32),
                   jax.ShapeDtypeStruct((t, d), hn_dtype),
                   jax.ShapeDtypeStruct((b, FFN_CONV - 1, D_FF), F32)],
        scratch_shapes=[pltpu.VMEM((nb, SUBLANES + rows, D_FF), F32)],
        compiler_params=pltpu.CompilerParams(dimension_semantics=("parallel", "arbitrary"),
                                             vmem_limit_bytes=VMEM_LIMIT),
        name="conv_ffn",
    )(*args)


def _prepare_params(norm_mix_w, w_in, ssm_conv_w, ssm_conv_b, ssm_dt_bias, ssm_a_log, ssm_d, ssm_norm_w,
                    gdn_conv_w, gdn_dt_bias, gdn_a_log, gdn_norm_w, w_ssm_out, w_gdn_out, w_o,
                    norm_ffn_w, w_up, ffn_conv_w, ffn_conv_b, w_down, norm_f_w):
    depth = w_in.shape[0]
    offs = [0]
    for s in IN_SIZES:
        offs.append(offs[-1] + s)
    col = lambda k: w_in[:, :, offs[k]:offs[k + 1]]
    w_main = jnp.concatenate([col(1), col(3), col(0), col(6), col(7), col(8)], axis=-1).astype(BF16)
    dt_cols = col(2)
    pad = jnp.zeros((depth, w_in.shape[1], SLAB - SSM_HEADS - 2 * GDN_HEADS), w_in.dtype)
    w_slab = jnp.concatenate([dt_cols[..., 0::2], dt_cols[..., 1::2], col(4), col(5), pad], axis=-1).astype(BF16)

    def slab(ssm_vec, gdn_vec):
        z = jnp.zeros((depth, SLAB - SSM_HEADS - GDN_HEADS), F32)
        return jnp.concatenate([ssm_vec[:, 0::2], ssm_vec[:, 1::2], gdn_vec, z], axis=-1)[:, None, :]

    return {
        "norm_mix_w": norm_mix_w[:, None, :],
        "w_main": w_main,
        "w_slab": w_slab,
        "ssm_conv_w": ssm_conv_w,
        "ssm_conv_b": ssm_conv_b[:, None, :],
        "slab_bias": slab(ssm_dt_bias, gdn_dt_bias),
        "slab_alog": slab(ssm_a_log, gdn_a_log),
        "ssm_d_cols": jnp.repeat(ssm_d, SSM_HEAD_DIM, axis=-1)[:, None, :],
        "ssm_norm_w": ssm_norm_w[:, None, :],
        "gdn_conv_w": gdn_conv_w,
        "gdn_norm_w": gdn_norm_w[:, None, :],
        "w_ssm_out": w_ssm_out.astype(BF16),
        "w_gdn_out": w_gdn_out.astype(BF16),
        "w_o": w_o.astype(BF16),
        "norm_ffn_w": norm_ffn_w[:, None, :],
        "w_up": w_up.astype(BF16),
        "ffn_conv_w": ffn_conv_w,
        "ffn_conv_b": ffn_conv_b[:, None, :],
        "w_down": w_down.astype(BF16),
        "norm_f_w": norm_f_w[None, None, :],
    }


def _run_trunk(x3, states, prm, *, c, v0, v1):
    b, lp, d = x3.shape
    depth = prm["w_main"].shape[0]
    x = x3.reshape(b * lp, d)
    h = _norm(x, prm["norm_mix_w"], 0, BF16)
    if states is None:
        st_ssm = cst_ssm = st_gdn = cst_gdn = cst_ffn = None
    else:
        st_ssm = states[0].reshape(depth, b, D_INNER, SSM_STATE)
        cst_ssm, st_gdn, cst_gdn, cst_ffn = states[1:]
    o_ssm = o_gdn = None
    conv_outs = [[] for _ in range(3)]
    for l in range(depth):
        proj, slab = _inproj(h, prm["w_main"], prm["w_slab"], l)
        proj3 = proj.reshape(b, lp, PROJ_COLS)
        slab3 = slab.reshape(b, lp, SLAB)
        y_ssm, o_ssm, o_ssm_conv = _ssd(proj3, slab3, st_ssm, cst_ssm, o_ssm, prm, l, depth, c=c, v0=v0, v1=v1)
        y_gdn, o_gdn, o_gdn_conv = _gdn(proj3, slab3, st_gdn, cst_gdn, o_gdn, prm, l, depth, c=c, v0=v0, v1=v1)
        x = _mix(x, y_ssm.reshape(b * lp, D_INNER), y_gdn.reshape(b * lp, GDN_HEADS * GDN_DV), proj, prm, l)
        if l + 1 < depth:
            x, h, o_ffn_conv = _ffn(x, cst_ffn, prm, l, prm["norm_mix_w"], l + 1, BF16, b=b, lp=lp, v1=v1)
        else:
            x, h, o_ffn_conv = _ffn(x, cst_ffn, prm, l, prm["norm_f_w"], 0, F32, b=b, lp=lp, v1=v1)
        conv_outs[0].append(o_ssm_conv)
        conv_outs[1].append(o_gdn_conv)
        conv_outs[2].append(o_ffn_conv)
    return (h.reshape(b, lp, d), o_ssm.reshape(depth, b, SSM_HEADS, SSM_HEAD_DIM, SSM_STATE),
            jnp.stack(conv_outs[0]), o_gdn, jnp.stack(conv_outs[1]), jnp.stack(conv_outs[2]))


def kernel(x_prompt, x_sample, state_ssm, state_ssm_conv, state_gdn, state_gdn_conv, state_ffn_conv, meta_tokens, norm_mix_w, w_in, ssm_conv_w, ssm_conv_b, ssm_dt_bias, ssm_a_log, ssm_d, ssm_norm_w, gdn_conv_w, gdn_dt_bias, gdn_a_log, gdn_norm_w, w_ssm_out, w_gdn_out, w_o, norm_ffn_w, w_up, ffn_conv_w, ffn_conv_b, w_down, norm_f_w):
    prm = _prepare_params(norm_mix_w, w_in, ssm_conv_w, ssm_conv_b, ssm_dt_bias, ssm_a_log, ssm_d, ssm_norm_w,
                          gdn_conv_w, gdn_dt_bias, gdn_a_log, gdn_norm_w, w_ssm_out, w_gdn_out, w_o,
                          norm_ffn_w, w_up, ffn_conv_w, ffn_conv_b, w_down, norm_f_w)
    bp, sp, d = x_prompt.shape
    bs, ls, _ = x_sample.shape
    c_p = SCAN_CHUNK
    lp = -(-(N_META + sp) // c_p) * c_p
    pad = lp - N_META - sp
    xp = jnp.concatenate([jnp.zeros((bp, pad, d), x_prompt.dtype),
                          jnp.broadcast_to(meta_tokens.astype(x_prompt.dtype), (bp, N_META, d)), x_prompt], axis=1)
    p_out = _run_trunk(xp, None, prm, c=c_p, v0=pad, v1=lp)
    y_prompt = p_out[0][:, pad + N_META:]
    c_s = -(-ls // SUBLANES) * SUBLANES
    xs = jnp.concatenate([x_sample, jnp.zeros((bs, c_s - ls, d), x_sample.dtype)], axis=1)
    s_out = _run_trunk(xs, (state_ssm, state_ssm_conv, state_gdn, state_gdn_conv, state_ffn_conv), prm,
                       c=c_s, v0=0, v1=ls)
    y_sample = s_out[0][:, :ls]
    return (y_prompt,) + (y_sample,) + p_out[1:] + s_out[1:]
```

```python
import functools

import jax
import jax.numpy as jnp
from jax import lax
from jax.experimental import pallas as pl
from jax.experimental.pallas import tpu as pltpu

F32 = jnp.float32
BF16 = jnp.bfloat16
HIGHEST = lax.Precision.HIGHEST

EPS = 1e-6
N_META = 16
D_MODEL = 1024
SSM_HEADS = 32
SSM_HEAD_DIM = 64
SSM_GROUPS = 4
SSM_STATE = 128
SSM_CONV = 4
D_INNER = SSM_HEADS * SSM_HEAD_DIM
SSM_BC = SSM_GROUPS * SSM_STATE
SSM_XBC = D_INNER + 2 * SSM_BC
SSM_PAIRS = SSM_HEADS // 2
PAIRS_PER_GROUP = SSM_PAIRS // SSM_GROUPS
GROUP_COLS = D_INNER // SSM_GROUPS
GDN_HEADS = 8
GDN_DK = 128
GDN_DV = 128
GDN_CONV = 4
GDN_QK = GDN_HEADS * GDN_DK
GDN_QKV = 2 * GDN_QK + GDN_HEADS * GDN_DV
D_FF = 2816
FFN_CONV = 3
IN_SIZES = (D_INNER, SSM_XBC, SSM_HEADS, GDN_QKV, GDN_HEADS, GDN_HEADS, GDN_HEADS * GDN_DV, D_MODEL, D_MODEL)
PROJ_COLS = SSM_XBC + GDN_QKV + D_INNER + 3 * D_MODEL
Z_BLOCK = (SSM_XBC + GDN_QKV) // D_INNER
GATE_BLOCK = (SSM_XBC + GDN_QKV + D_INNER) // D_MODEL
SLAB = 128
SLAB_GA = SSM_HEADS
SLAB_GB = SSM_HEADS + GDN_HEADS

LANES = 128
SUBLANES = 8
MXU_DIM = 256
SSD_CHUNK = 64
GDN_CHUNK = 128
INVERSE_BLOCK = 64
VMEM_LIMIT = 56 * 1024 * 1024
SCAN_SEQS_LONG = 2
SCAN_SEQS_SHORT = 4


def _pick_tile(total, cap, mult):
    best = None
    for t in range(mult, min(total, cap) + 1, mult):
        if total % t == 0:
            best = t
    assert best is not None, (total, cap, mult)
    return best


def _sigmoid(x):
    return 1.0 / (1.0 + jnp.exp(-x))


def _silu(x):
    return x * _sigmoid(x)


def _softplus(x):
    return jnp.maximum(x, 0.0) + jnp.log1p(jnp.exp(-jnp.abs(x)))


def _rms(x, w):
    return x * lax.rsqrt(jnp.mean(x * x, axis=-1, keepdims=True) + EPS) * w


def _dot(a, b):
    return jnp.dot(a.astype(BF16), b.astype(BF16), preferred_element_type=F32)


def _dot_nt(a, b):
    return lax.dot_general(a.astype(BF16), b.astype(BF16), (((1,), (1,)), ((), ())), preferred_element_type=F32)


def _dot_tn(a, b):
    return lax.dot_general(a.astype(BF16), b.astype(BF16), (((0,), (0,)), ((), ())), preferred_element_type=F32)


def _cumsum_rows(x):
    c = x.shape[0]
    tril = (lax.broadcasted_iota(jnp.int32, (c, c), 0) >= lax.broadcasted_iota(jnp.int32, (c, c), 1)).astype(F32)
    return jnp.dot(tril, x, precision=HIGHEST, preferred_element_type=F32)


def _causal_conv(buf, w_ref, width, rows):
    acc = None
    if len(buf.shape) == 2:
        xin = buf[0:SUBLANES + rows, :]
        for j in range(width):
            shift = width - 1 - j
            xs = pltpu.roll(xin, shift, axis=0) if shift else xin
            term = xs[SUBLANES:SUBLANES + rows, :] * w_ref[j:j + 1, :]
            acc = term if acc is None else acc + term
        return acc
    for j in range(width):
        off = SUBLANES - (width - 1) + j
        term = buf[:, off:off + rows, :] * w_ref[j:j + 1, :]
        acc = term if acc is None else acc + term
    return acc


def _norm_kernel(x_ref, w_ref, o_ref):
    o_ref[...] = _rms(x_ref[...], w_ref[...]).astype(o_ref.dtype)


def _norm(x, w, layer, out_dtype):
    t, d = x.shape
    tm = _pick_tile(t, 1024, 16)
    return pl.pallas_call(
        _norm_kernel,
        grid=(t // tm,),
        in_specs=[pl.BlockSpec((tm, d), lambda i: (i, 0)),
                  pl.BlockSpec((None, 1, d), lambda i: (layer, 0, 0))],
        out_specs=pl.BlockSpec((tm, d), lambda i: (i, 0)),
        out_shape=jax.ShapeDtypeStruct((t, d), out_dtype),
        compiler_params=pltpu.CompilerParams(dimension_semantics=("parallel",)),
        name="rmsnorm",
    )(x, w)


def _inproj_kernel(h_ref, w_ref, ws_ref, o_ref, os_ref):
    h = h_ref[...]
    o_ref[...] = jnp.dot(h, w_ref[...], preferred_element_type=F32)

    @pl.when(pl.program_id(1) == 0)
    def _():
        os_ref[...] = jnp.dot(h, ws_ref[...], preferred_element_type=F32)


def _inproj(h, w_main, w_slab, layer):
    t, d = h.shape
    ncols = w_main.shape[-1]
    tm = _pick_tile(t, 1536, 16)
    tn = _pick_tile(ncols, 1024, LANES)
    return pl.pallas_call(
        _inproj_kernel,
        grid=(t // tm, ncols // tn),
        in_specs=[pl.BlockSpec((tm, d), lambda i, j: (i, 0)),
                  pl.BlockSpec((None, d, tn), lambda i, j: (layer, 0, j)),
                  pl.BlockSpec((None, d, SLAB), lambda i, j: (layer, 0, 0))],
        out_specs=[pl.BlockSpec((tm, tn), lambda i, j: (i, j)),
                   pl.BlockSpec((tm, SLAB), lambda i, j: (i, 0))],
        out_shape=[jax.ShapeDtypeStruct((t, ncols), F32), jax.ShapeDtypeStruct((t, SLAB), F32)],
        compiler_params=pltpu.CompilerParams(dimension_semantics=("parallel", "arbitrary"),
                                             vmem_limit_bytes=VMEM_LIMIT),
        name="inproj",
    )(h, w_main, w_slab)


def _scan_call(body, name, proj3, slab3, col_blocks, state_in, conv_in, params, prev_state, out_cols,
               state_block, conv_cols, conv_width, scratch, layer, depth, *, c, v0, v1):
    b, lp, _ = proj3.shape
    nc = lp // c
    nb = _pick_tile(b, SCAN_SEQS_LONG if nc > 1 else SCAN_SEQS_SHORT, 1)
    vend = v1 - (nc - 1) * c
    hist = conv_width - 1
    assert hist <= vend <= c
    has_state = state_in is not None
    has_prev = prev_state is not None
    zeros = (0,) * len(state_block)
    lyr = lambda bi, i: (layer, 0, 0)
    in_specs = [pl.BlockSpec((nb, c, width), functools.partial(lambda blk, bi, i: (bi, i, blk), blk))
                for width, blk in col_blocks]
    in_specs.append(pl.BlockSpec((nb, c, SLAB), lambda bi, i: (bi, i, 0)))
    args = [proj3] * len(col_blocks) + [slab3]
    if has_state:
        in_specs += [pl.BlockSpec((None, nb) + state_block, lambda bi, i: (layer, bi) + zeros),
                     pl.BlockSpec((None, nb, hist, conv_cols), lambda bi, i: (layer, bi, 0, 0))]
        args += [state_in, conv_in]
    for p in params:
        in_specs.append(pl.BlockSpec((None,) + p.shape[1:], lyr))
        args.append(p)
    aliases = {}
    if has_prev:
        aliases[len(args)] = 1
        in_specs.append(pl.BlockSpec(memory_space=pl.ANY))
        args.append(prev_state)
    return pl.pallas_call(
        functools.partial(body, nb=nb, c=c, v0=v0, v1=v1, vend=vend, has_state=has_state, has_prev=has_prev),
        grid=(b // nb, nc),
        in_specs=in_specs,
        out_specs=[pl.BlockSpec((nb, c, out_cols), lambda bi, i: (bi, i, 0)),
                   pl.BlockSpec((None, nb) + state_block, lambda bi, i: (layer, bi) + zeros),
                   pl.BlockSpec((nb, hist, conv_cols), lambda bi, i: (bi, 0, 0))],
        out_shape=[jax.ShapeDtypeStruct((b, lp, out_cols), BF16),
                   jax.ShapeDtypeStruct((depth, b) + state_block, F32),
                   jax.ShapeDtypeStruct((b, hist, conv_cols), F32)],
        scratch_shapes=[pltpu.VMEM((nb, SUBLANES + c, conv_cols), F32)] + [s(nb) for s in scratch],
        input_output_aliases=aliases,
        compiler_params=pltpu.CompilerParams(dimension_semantics=("parallel", "arbitrary"),
                                             vmem_limit_bytes=VMEM_LIMIT),
        name=name,
    )(*args)


def _conv_step(buf, x_ref, cst_out, w_ref, width, c, vend):
    buf[SUBLANES:SUBLANES + c, :] = x_ref[...]
    out = _causal_conv(buf, w_ref, width, c)
    cst_out[...] = buf[SUBLANES + vend - (width - 1):SUBLANES + vend, :]
    buf[0:SUBLANES, :] = buf[c:c + SUBLANES, :]
    return out


def _conv_init(buf, cst_ref, width, cols):
    buf[0:SUBLANES, :] = jnp.zeros((SUBLANES, cols), F32)
    if cst_ref is not None:
        buf[SUBLANES - (width - 1):SUBLANES, :] = cst_ref[...]


def _pair_expand(arr, p, width, half):
    r = arr.shape[0]
    even = jnp.broadcast_to(arr[:, p:p + 1], (r, width))
    odd = jnp.broadcast_to(arr[:, SSM_PAIRS + p:SSM_PAIRS + p + 1], (r, width))
    return jnp.where(half, even, odd)


def _ssd_kernel(*refs, nb, c, v0, v1, vend, has_state, has_prev):
    refs = list(refs)
    xbc_ref, z_ref, sm_ref = refs[:3]
    k = 3
    st_ref = cst_ref = None
    if has_state:
        st_ref, cst_ref = refs[k:k + 2]
        k += 2
    cw_ref, cb_ref, bias_ref, alog_ref, dexp_ref, nw_ref = refs[k:k + 6]
    k += 6 + (1 if has_prev else 0)
    y_ref, sto_ref, csto_ref, buf, ht = refs[k:]
    ci = pl.program_id(1)

    @pl.when(ci == 0)
    def _():
        for s in range(nb):
            _conv_init(buf.at[s], cst_ref.at[s] if has_state else None, SSM_CONV, SSM_XBC)
            if has_state:
                for p in range(SSM_PAIRS):
                    ht[s, :, p * LANES:(p + 1) * LANES] = st_ref[s, p * LANES:(p + 1) * LANES, :].T
            else:
                ht[s] = jnp.zeros(ht.shape[1:], F32)

    row = lax.broadcasted_iota(jnp.int32, (c, SLAB), 0) + ci * c
    valid = jnp.logical_and(row >= v0, row < v1)
    neg_a = -jnp.exp(alog_ref[...])
    w2 = 2 * c
    lane2 = lax.broadcasted_iota(jnp.int32, (c, w2), 1)
    row2 = lax.broadcasted_iota(jnp.int32, (c, w2), 0)
    half2 = lane2 < c
    j2 = jnp.where(half2, lane2, lane2 - c)
    causal2 = j2 <= row2
    upto2 = row2 <= j2
    eye2 = j2 == row2
    half_p = lax.broadcasted_iota(jnp.int32, (c, LANES), 1) < SSM_HEAD_DIM

    for s in range(nb):
        xbc = _silu(_conv_step(buf.at[s], xbc_ref.at[s], csto_ref.at[s], cw_ref, SSM_CONV, c, vend) + cb_ref[...])
        dt = jnp.where(valid, _softplus(sm_ref[s] + bias_ref[...]), 0.0)
        acum = _cumsum_rows(dt * neg_a)
        e_acum = jnp.exp(acum)
        last = acum[c - 1:c, :]
        e_last = jnp.exp(last)
        w_last = jnp.exp(last - acum) * dt
        for g in range(SSM_GROUPS):
            b_g = xbc[:, D_INNER + g * SSM_STATE:D_INNER + (g + 1) * SSM_STATE]
            c_g = xbc[:, D_INNER + SSM_BC + g * SSM_STATE:D_INNER + SSM_BC + (g + 1) * SSM_STATE]
            gcols = slice(g * GROUP_COLS, (g + 1) * GROUP_COLS)
            cb = _dot_nt(c_g, b_g)
            cb2 = jnp.concatenate([cb, cb], axis=1)
            y_off = _dot(c_g, ht[s, :, gcols])
            ys, xws, els = [], [], []
            for q in range(PAIRS_PER_GROUP):
                p = g * PAIRS_PER_GROUP + q
                dx = _pair_expand(dt, p, w2, half2)
                ax = dx * _pair_expand(neg_a, p, w2, half2[0:1])
                row_cum = jnp.sum(jnp.where(upto2, ax, 0.0), axis=0, keepdims=True)
                dt_row = jnp.sum(jnp.where(eye2, dx, 0.0), axis=0, keepdims=True)
                col_cum = _pair_expand(acum, p, w2, half2)
                decay = jnp.where(causal2, jnp.exp(col_cum - row_cum), 0.0)
                wgt = decay * cb2 * dt_row
                xp = xbc[:, p * LANES:(p + 1) * LANES]
                rhs = jnp.concatenate([jnp.where(half_p, xp, 0.0), jnp.where(half_p, 0.0, xp)], axis=0)
                y = _dot(wgt, rhs)
                y = y + y_off[:, q * LANES:(q + 1) * LANES] * _pair_expand(e_acum, p, LANES, half_p)
                y = y + dexp_ref[:, p * LANES:(p + 1) * LANES] * xp
                ys.append(y)
                xws.append(xp * _pair_expand(w_last, p, LANES, half_p))
                els.append(_pair_expand(e_last, p, LANES, half_p[0:1]))
            upd = _dot_tn(b_g, jnp.concatenate(xws, axis=1))
            ht[s, :, gcols] = ht[s, :, gcols] * jnp.concatenate(els, axis=1) + upd
            yz = jnp.concatenate(ys, axis=1) * _silu(z_ref[s, :, gcols])
            y_ref[s, :, gcols] = _rms(yz, nw_ref[:, gcols]).astype(y_ref.dtype)

    @pl.when(ci == pl.num_programs(1) - 1)
    def _():
        for s in range(nb):
            for p in range(SSM_PAIRS):
                sto_ref[s, p * LANES:(p + 1) * LANES, :] = ht[s, :, p * LANES:(p + 1) * LANES].T


def _ssd(proj3, slab3, st, cst, prev, prm, layer, depth, *, c, v0, v1):
    params = [prm["ssm_conv_w"], prm["ssm_conv_b"], prm["slab_bias"], prm["slab_alog"], prm["ssm_d_cols"],
              prm["ssm_norm_w"]]
    return _scan_call(_ssd_kernel, "ssd_scan", proj3, slab3, [(SSM_XBC, 0), (D_INNER, Z_BLOCK)], st, cst, params,
                      prev, D_INNER, (D_INNER, SSM_STATE), SSM_XBC, SSM_CONV,
                      [lambda nb: pltpu.VMEM((nb, SSM_STATE, D_INNER), F32)], layer, depth, c=c, v0=v0, v1=v1)


def _head_products(x, y, c):
    per_op = min(GDN_HEADS, MXU_DIM // c)
    width = per_op * c
    lane = lax.broadcasted_iota(jnp.int32, (c, width), 1)
    outs = []
    head = lane // c if per_op > 1 else None
    for s in range(GDN_HEADS // per_op):
        ys = y[:, s * width:(s + 1) * width]
        diag = ys if per_op == 1 else jnp.concatenate(
            [jnp.where(head == r, ys, 0.0) for r in range(per_op)], axis=0)
        outs.append(_dot(x[:, s * width:(s + 1) * width], diag))
    return outs[0] if len(outs) == 1 else jnp.concatenate(outs, axis=1)


def _unit_lower_inverse(a, eye, same_half, c):
    blk = min(c, INVERSE_BLOCK)
    assert c in (blk, 2 * blk)
    a_d = a if blk == c else jnp.where(same_half, a, 0.0)
    n = -a_d
    t = eye + n
    m = _head_products(n, n, c)
    covered = 2
    while 2 * covered < blk:
        both = _head_products(jnp.concatenate([t, m], axis=0), m, c)
        t = t + both[0:c]
        m = both[c:2 * c]
        covered *= 2
    d = t + _head_products(t, m, c)
    if blk == c:
        return d
    a_o = jnp.where(same_half, 0.0, a)
    return d - _head_products(_head_products(d, a_o, c), d, c)


def _gdn_kernel(*refs, nb, c, v0, v1, vend, has_state, has_prev):
    refs = list(refs)
    qkv_ref, gate_ref, sm_ref = refs[:3]
    k = 3
    st_ref = cst_ref = None
    if has_state:
        st_ref, cst_ref = refs[k:k + 2]
        k += 2
    cw_ref, bias_ref, alog_ref, nw_ref = refs[k:k + 4]
    k += 4 + (1 if has_prev else 0)
    y_ref, sto_ref, csto_ref, buf = refs[k:]
    ci = pl.program_id(1)

    @pl.when(ci == 0)
    def _():
        for s in range(nb):
            _conv_init(buf.at[s], cst_ref.at[s] if has_state else None, GDN_CONV, GDN_QKV)
            sto_ref[s] = st_ref[s] if has_state else jnp.zeros(sto_ref.shape[1:], F32)

    row = lax.broadcasted_iota(jnp.int32, (c, SLAB), 0) + ci * c
    valid = jnp.logical_and(row >= v0, row < v1)
    neg_a = -jnp.exp(alog_ref[...])
    w8 = GDN_HEADS * c
    lane8 = lax.broadcasted_iota(jnp.int32, (c, w8), 1)
    row8 = lax.broadcasted_iota(jnp.int32, (c, w8), 0)
    j8 = lane8 & (c - 1)
    causal8 = j8 <= row8
    strict8 = j8 < row8
    upto8 = row8 <= j8
    eye8 = jnp.where(j8 == row8, 1.0, 0.0)
    same_half8 = (j8 // INVERSE_BLOCK) == (row8 // INVERSE_BLOCK)
    lane_k =lax.broadcasted_iota(jnp.int32, (c, 2 * GDN_DK), 1) < GDN_DK

    def heads_on_lanes(arr):
        return jnp.concatenate(
            [jnp.broadcast_to(arr[:, SLAB_GA + h:SLAB_GA + h + 1], (c, c)) for h in range(GDN_HEADS)], axis=1)

    for s in range(nb):
        qkv = _silu(_conv_step(buf.at[s], qkv_ref.at[s], csto_ref.at[s], cw_ref, GDN_CONV, c, vend))
        sm = sm_ref[s]
        g = jnp.where(valid, neg_a * _softplus(sm + bias_ref[...]), 0.0)
        beta = jnp.where(valid, _sigmoid(sm), 0.0)
        gcum = _cumsum_rows(g)
        e_g = jnp.exp(gcum)
        g_last = gcum[c - 1:c, :]
        e_last = jnp.exp(g_last)
        k_dec = jnp.exp(g_last - gcum)
        row_cum = jnp.sum(jnp.where(upto8, heads_on_lanes(g), 0.0), axis=0, keepdims=True)
        decay = jnp.where(causal8, jnp.exp(heads_on_lanes(gcum) - row_cum), 0.0)

        qs, ks, kbs = [], [], []
        for h in range(GDN_HEADS):
            qh = qkv[:, h * GDN_DK:(h + 1) * GDN_DK]
            kh = qkv[:, GDN_QK + h * GDN_DK:GDN_QK + (h + 1) * GDN_DK]
            qs.append(qh * lax.rsqrt(jnp.sum(qh * qh, axis=-1, keepdims=True) + EPS) * (GDN_DK ** -0.5))
            kn = kh * lax.rsqrt(jnp.sum(kh * kh, axis=-1, keepdims=True) + EPS)
            ks.append(kn)
            kbs.append(kn * beta[:, SLAB_GB + h:SLAB_GB + h + 1])

        kk_parts, qk_parts = [], []
        for hp in range(GDN_HEADS // 2):
            k2 = jnp.concatenate([ks[2 * hp], ks[2 * hp + 1]], axis=1)
            rhs_t = jnp.concatenate([jnp.where(lane_k, k2, 0.0), jnp.where(lane_k, 0.0, k2)], axis=0)
            lhs = jnp.concatenate([jnp.concatenate([kbs[2 * hp], kbs[2 * hp + 1]], axis=1),
                                   jnp.concatenate([qs[2 * hp], qs[2 * hp + 1]], axis=1)], axis=0)
            res = _dot_nt(lhs, rhs_t)
            kk_parts.append(res[0:c, :])
            qk_parts.append(res[c:2 * c, :])
        a_mat = jnp.where(strict8, jnp.concatenate(kk_parts, axis=1) * decay, 0.0)
        attn = jnp.concatenate(qk_parts, axis=1) * decay
        t_mat = _unit_lower_inverse(a_mat, eye8, same_half8, c)

        for h in range(GDN_HEADS):
            beta_h = beta[:, SLAB_GB + h:SLAB_GB + h + 1]
            eg_h = e_g[:, SLAB_GA + h:SLAB_GA + h + 1]
            vh = qkv[:, 2 * GDN_QK + h * GDN_DV:2 * GDN_QK + (h + 1) * GDN_DV]
            rhs = jnp.concatenate([vh * beta_h, kbs[h] * eg_h], axis=1)
            sol = _dot(t_mat[:, h * c:(h + 1) * c], rhs)
            attn_h = attn[:, h * c:(h + 1) * c]
            k_dec_h = ks[h] * k_dec[:, SLAB_GA + h:SLAB_GA + h + 1]
            if c % LANES == 0:
                both = _dot(jnp.concatenate([attn_h, k_dec_h.T], axis=0), sol)
                au_aw, ku_kw = both[0:c], both[c:]
            else:
                au_aw, ku_kw = _dot(attn_h, sol), _dot_tn(k_dec_h, sol)
            q_eff = qs[h] * eg_h - au_aw[:, GDN_DV:]
            s_old = sto_ref[s, h]
            through = _dot(jnp.concatenate([q_eff, ku_kw[:, GDN_DV:]], axis=0), s_old)
            o = through[0:c] + au_aw[:, 0:GDN_DV]
            sto_ref[s, h] = (s_old * e_last[:, SLAB_GA + h:SLAB_GA + h + 1] - through[c:] + ku_kw[:, 0:GDN_DV])
            gt = gate_ref[s, :, h * GDN_DV:(h + 1) * GDN_DV]
            y_ref[s, :, h * GDN_DV:(h + 1) * GDN_DV] = (_rms(o, nw_ref[...]) * _silu(gt)).astype(y_ref.dtype)


def _gdn(proj3, slab3, st, cst, prev, prm, layer, depth, *, c, v0, v1):
    params = [prm["gdn_conv_w"], prm["slab_bias"], prm["slab_alog"], prm["gdn_norm_w"]]
    return _scan_call(_gdn_kernel, "gdn_scan", proj3, slab3, [(GDN_QKV, 1), (GDN_HEADS * GDN_DV, GATE_BLOCK)],
                      st, cst, params, prev, GDN_HEADS * GDN_DV, (GDN_HEADS, GDN_DK, GDN_DV), GDN_QKV, GDN_CONV,
                      [], layer, depth, c=c, v0=v0, v1=v1)


def _mix_kernel(x_ref, ys_ref, yg_ref, ms_ref, mg_ref, wso_ref, wgo_ref, wo_ref, o_ref):
    y_ssm = jnp.dot(ys_ref[...], wso_ref[...], preferred_element_type=F32)
    y_gdn = jnp.dot(yg_ref[...], wgo_ref[...], preferred_element_type=F32)
    merged = _sigmoid(ms_ref[...]) * y_ssm + _sigmoid(mg_ref[...]) * y_gdn
    o_ref[...] = x_ref[...] + _dot(merged, wo_ref[...])


def _mix(x, y_ssm, y_gdn, proj, prm, layer):
    t, d = x.shape
    tm = _pick_tile(t, 512, 16)
    lyr = lambda i: (layer, 0, 0)
    return pl.pallas_call(
        _mix_kernel,
        grid=(t // tm,),
        in_specs=[pl.BlockSpec((tm, d), lambda i: (i, 0)),
                  pl.BlockSpec((tm, D_INNER), lambda i: (i, 0)),
                  pl.BlockSpec((tm, GDN_HEADS * GDN_DV), lambda i: (i, 0)),
                  pl.BlockSpec((tm, d), lambda i: (i, GATE_BLOCK + 1)),
                  pl.BlockSpec((tm, d), lambda i: (i, GATE_BLOCK + 2)),
                  pl.BlockSpec((None, D_INNER, d), lyr),
                  pl.BlockSpec((None, GDN_HEADS * GDN_DV, d), lyr),
                  pl.BlockSpec((None, d, d), lyr)],
        out_specs=pl.BlockSpec((tm, d), lambda i: (i, 0)),
        out_shape=jax.ShapeDtypeStruct((t, d), F32),
        compiler_params=pltpu.CompilerParams(dimension_semantics=("parallel",), vmem_limit_bytes=VMEM_LIMIT),
        name="mix_out",
    )(x, y_ssm, y_gdn, proj, proj, prm["w_ssm_out"], prm["w_gdn_out"], prm["w_o"])


def _ffn_kernel(*refs, nb, rows, vend, has_state):
    if has_state:
        (x_ref, cst_ref, nw_ref, wup_ref, cw_ref, cb_ref, wdn_ref, nnw_ref, xo_ref, hn_ref, csto_ref, buf) = refs
    else:
        (x_ref, nw_ref, wup_ref, cw_ref, cb_ref, wdn_ref, nnw_ref, xo_ref, hn_ref, csto_ref, buf) = refs
    ti = pl.program_id(1)
    last_tile = pl.num_programs(1) - 1
    hist = FFN_CONV - 1

    @pl.when(ti == 0)
    def _():
        buf[:, 0:SUBLANES, :] = jnp.zeros((nb, SUBLANES, D_FF), F32)
        if has_state:
            buf[:, SUBLANES - hist:SUBLANES, :] = cst_ref[...]

    x = x_ref[...]
    gu = _dot(_rms(x, nw_ref[...]), wup_ref[...])
    buf[:, SUBLANES:SUBLANES + rows, :] = gu[:, 0:D_FF].reshape(nb, rows, D_FF)
    conv = _causal_conv(buf.at[0] if nb == 1 else buf, cw_ref, FFN_CONV, rows)

    @pl.when(ti == last_tile)
    def _():
        csto_ref[...] = buf[:, SUBLANES + vend - hist:SUBLANES + vend, :]

    buf[:, 0:SUBLANES, :] = buf[:, rows:rows + SUBLANES, :]
    act = _silu(conv + cb_ref[...]).reshape(nb * rows, D_FF) * gu[:, D_FF:]
    out = x + _dot(act, wdn_ref[...])
    xo_ref[...] = out
    hn_ref[...] = _rms(out, nnw_ref[...]).astype(hn_ref.dtype)


def _ffn(x, cst, prm, layer, next_norm_w, next_layer, hn_dtype, *, b, lp, v1):
    t, d = x.shape
    has_state = cst is not None
    if lp <= 64:
        rows, nb = lp, _pick_tile(b, max(1, 256 // lp), 1)
    else:
        rows, nb = _pick_tile(lp, 384, 16), 1
    nt = lp // rows
    vend = v1 - (nt - 1) * rows
    assert FFN_CONV - 1 <= vend <= rows
    lyr = lambda bi, i: (layer, 0, 0)
    flat = lambda bi, i: (bi * nt + i, 0)
    per_b = lambda bi, i: (bi, 0, 0)
    in_specs = [pl.BlockSpec((nb * rows, d), flat)]
    args = [x]
    if has_state:
        in_specs.append(pl.BlockSpec((None, nb, FFN_CONV - 1, D_FF), lambda bi, i: (layer, bi, 0, 0)))
        args.append(cst)
    in_specs += [pl.BlockSpec((None, 1, d), lyr),
                 pl.BlockSpec((None, d, 2 * D_FF), lyr),
                 pl.BlockSpec((None, FFN_CONV, D_FF), lyr),
                 pl.BlockSpec((None, 1, D_FF), lyr),
                 pl.BlockSpec((None, D_FF, d), lyr),
                 pl.BlockSpec((None, 1, d), lambda bi, i: (next_layer, 0, 0))]
    args += [prm["norm_ffn_w"], prm["w_up"], prm["ffn_conv_w"], prm["ffn_conv_b"], prm["w_down"], next_norm_w]
    return pl.pallas_call(
        functools.partial(_ffn_kernel, nb=nb, rows=rows, vend=vend, has_state=has_state),
        grid=(b // nb, nt),
        in_specs=in_specs,
        out_specs=[pl.BlockSpec((nb * rows, d), flat),
                   pl.BlockSpec((nb * rows, d), flat),
                   pl.BlockSpec((nb, FFN_CONV - 1, D_FF), per_b)],
        out_shape=[jax.ShapeDtypeStruct((t, d), F32),
                   jax.ShapeDtypeStruct((t, d), hn_dtype),
                   jax.ShapeDtypeStruct((b, FFN_CONV - 1, D_FF), F32)],
        scratch_shapes=[pltpu.VMEM((nb, SUBLANES + rows, D_FF), F32)],
        compiler_params=pltpu.CompilerParams(dimension_semantics=("parallel", "arbitrary"),
                                             vmem_limit_bytes=VMEM_LIMIT),
        name="conv_ffn",
    )(*args)


def _prepare_params(norm_mix_w, w_in, ssm_conv_w, ssm_conv_b, ssm_dt_bias, ssm_a_log, ssm_d, ssm_norm_w,
                    gdn_conv_w, gdn_dt_bias, gdn_a_log, gdn_norm_w, w_ssm_out, w_gdn_out, w_o,
                    norm_ffn_w, w_up, ffn_conv_w, ffn_conv_b, w_down, norm_f_w):
    depth = w_in.shape[0]
    offs = [0]
    for s in IN_SIZES:
        offs.append(offs[-1] + s)
    col = lambda k: w_in[:, :, offs[k]:offs[k + 1]]
    w_main = jnp.concatenate([col(1), col(3), col(0), col(6), col(7), col(8)], axis=-1).astype(BF16)
    dt_cols = col(2)
    pad = jnp.zeros((depth, w_in.shape[1], SLAB - SSM_HEADS - 2 * GDN_HEADS), w_in.dtype)
    w_slab = jnp.concatenate([dt_cols[..., 0::2], dt_cols[..., 1::2], col(4), col(5), pad], axis=-1).astype(BF16)

    def slab(ssm_vec, gdn_vec):
        z = jnp.zeros((depth, SLAB - SSM_HEADS - GDN_HEADS), F32)
        return jnp.concatenate([ssm_vec[:, 0::2], ssm_vec[:, 1::2], gdn_vec, z], axis=-1)[:, None, :]

    return {
        "norm_mix_w": norm_mix_w[:, None, :],
        "w_main": w_main,
        "w_slab": w_slab,
        "ssm_conv_w": ssm_conv_w,
        "ssm_conv_b": ssm_conv_b[:, None, :],
        "slab_bias": slab(ssm_dt_bias, gdn_dt_bias),
        "slab_alog": slab(ssm_a_log, gdn_a_log),
        "ssm_d_cols": jnp.repeat(ssm_d, SSM_HEAD_DIM, axis=-1)[:, None, :],
        "ssm_norm_w": ssm_norm_w[:, None, :],
        "gdn_conv_w": gdn_conv_w,
        "gdn_norm_w": gdn_norm_w[:, None, :],
        "w_ssm_out": w_ssm_out.astype(BF16),
        "w_gdn_out": w_gdn_out.astype(BF16),
        "w_o": w_o.astype(BF16),
        "norm_ffn_w": norm_ffn_w[:, None, :],
        "w_up": w_up.astype(BF16),
        "ffn_conv_w": ffn_conv_w,
        "ffn_conv_b": ffn_conv_b[:, None, :],
        "w_down": w_down.astype(BF16),
        "norm_f_w": norm_f_w[None, None, :],
    }


def _run_trunk(x3, states, prm, *, c_ssd, c_gdn, v0, v1):
    b, lp, d = x3.shape
    depth = prm["w_main"].shape[0]
    x = x3.reshape(b * lp, d)
    h = _norm(x, prm["norm_mix_w"], 0, BF16)
    if states is None:
        st_ssm = cst_ssm = st_gdn = cst_gdn = cst_ffn = None
    else:
        st_ssm = states[0].reshape(depth, b, D_INNER, SSM_STATE)
        cst_ssm, st_gdn, cst_gdn, cst_ffn = states[1:]
    o_ssm = o_gdn = None
    conv_outs = [[] for _ in range(3)]
    for l in range(depth):
        proj, slab = _inproj(h, prm["w_main"], prm["w_slab"], l)
        proj3 = proj.reshape(b, lp, PROJ_COLS)
        slab3 = slab.reshape(b, lp, SLAB)
        y_ssm, o_ssm, o_ssm_conv = _ssd(proj3, slab3, st_ssm, cst_ssm, o_ssm, prm, l, depth, c=c_ssd, v0=v0, v1=v1)
        y_gdn, o_gdn, o_gdn_conv = _gdn(proj3, slab3, st_gdn, cst_gdn, o_gdn, prm, l, depth, c=c_gdn, v0=v0, v1=v1)
        x = _mix(x, y_ssm.reshape(b * lp, D_INNER), y_gdn.reshape(b * lp, GDN_HEADS * GDN_DV), proj, prm, l)
        if l + 1 < depth:
            x, h, o_ffn_conv = _ffn(x, cst_ffn, prm, l, prm["norm_mix_w"], l + 1, BF16, b=b, lp=lp, v1=v1)
        else:
            x, h, o_ffn_conv = _ffn(x, cst_ffn, prm, l, prm["norm_f_w"], 0, F32, b=b, lp=lp, v1=v1)
        conv_outs[0].append(o_ssm_conv)
        conv_outs[1].append(o_gdn_conv)
        conv_outs[2].append(o_ffn_conv)
    return (h.reshape(b, lp, d), o_ssm.reshape(depth, b, SSM_HEADS, SSM_HEAD_DIM, SSM_STATE),
            jnp.stack(conv_outs[0]), o_gdn, jnp.stack(conv_outs[1]), jnp.stack(conv_outs[2]))


def kernel(x_prompt, x_sample, state_ssm, state_ssm_conv, state_gdn, state_gdn_conv, state_ffn_conv, meta_tokens, norm_mix_w, w_in, ssm_conv_w, ssm_conv_b, ssm_dt_bias, ssm_a_log, ssm_d, ssm_norm_w, gdn_conv_w, gdn_dt_bias, gdn_a_log, gdn_norm_w, w_ssm_out, w_gdn_out, w_o, norm_ffn_w, w_up, ffn_conv_w, ffn_conv_b, w_down, norm_f_w):
    prm = _prepare_params(norm_mix_w, w_in, ssm_conv_w, ssm_conv_b, ssm_dt_bias, ssm_a_log, ssm_d, ssm_norm_w,
                          gdn_conv_w, gdn_dt_bias, gdn_a_log, gdn_norm_w, w_ssm_out, w_gdn_out, w_o,
                          norm_ffn_w, w_up, ffn_conv_w, ffn_conv_b, w_down, norm_f_w)
    bp, sp, d = x_prompt.shape
    bs, ls, _ = x_sample.shape
    c_p = max(SSD_CHUNK, GDN_CHUNK)
    assert c_p % SSD_CHUNK == 0 and c_p % GDN_CHUNK == 0
    lp = -(-(N_META + sp) // c_p) * c_p
    pad = lp - N_META - sp
    xp = jnp.concatenate([jnp.zeros((bp, pad, d), x_prompt.dtype),
                          jnp.broadcast_to(meta_tokens.astype(x_prompt.dtype), (bp, N_META, d)), x_prompt], axis=1)
    p_out = _run_trunk(xp, None, prm, c_ssd=SSD_CHUNK, c_gdn=GDN_CHUNK, v0=pad, v1=lp)
    y_prompt = p_out[0][:, pad + N_META:]
    c_s = -(-ls // SUBLANES) * SUBLANES
    xs = jnp.concatenate([x_sample, jnp.zeros((bs, c_s - ls, d), x_sample.dtype)], axis=1)
    s_out = _run_trunk(xs, (state_ssm, state_ssm_conv, state_gdn, state_gdn_conv, state_ffn_conv), prm,
                       c_ssd=c_s, c_gdn=c_s, v0=0, v1=ls)
    y_sample = s_out[0][:, :ls]
    return (y_prompt,) + (y_sample,) + p_out[1:] + s_out[1:]
```

```python
import functools

import jax
import jax.numpy as jnp
from jax import lax
from jax.experimental import pallas as pl
from jax.experimental.pallas import tpu as pltpu

F32 = jnp.float32
BF16 = jnp.bfloat16
HIGHEST = lax.Precision.HIGHEST

EPS = 1e-6
N_META = 16
D_MODEL = 1024
SSM_HEADS = 32
SSM_HEAD_DIM = 64
SSM_GROUPS = 4
SSM_STATE = 128
SSM_CONV = 4
D_INNER = SSM_HEADS * SSM_HEAD_DIM
SSM_BC = SSM_GROUPS * SSM_STATE
SSM_XBC = D_INNER + 2 * SSM_BC
SSM_PAIRS = SSM_HEADS // 2
PAIRS_PER_GROUP = SSM_PAIRS // SSM_GROUPS
GROUP_COLS = D_INNER // SSM_GROUPS
GDN_HEADS = 8
GDN_DK = 128
GDN_DV = 128
GDN_CONV = 4
GDN_QK = GDN_HEADS * GDN_DK
GDN_QKV = 2 * GDN_QK + GDN_HEADS * GDN_DV
D_FF = 2816
FFN_CONV = 3
IN_SIZES = (D_INNER, SSM_XBC, SSM_HEADS, GDN_QKV, GDN_HEADS, GDN_HEADS, GDN_HEADS * GDN_DV, D_MODEL, D_MODEL)
PROJ_COLS = SSM_XBC + GDN_QKV + D_INNER + 3 * D_MODEL
Z_BLOCK = (SSM_XBC + GDN_QKV) // D_INNER
GATE_BLOCK = (SSM_XBC + GDN_QKV + D_INNER) // D_MODEL
SLAB = 128
SLAB_GA = SSM_HEADS
SLAB_GB = SSM_HEADS + GDN_HEADS

LANES = 128
SUBLANES = 8
MXU_DIM = 256
SSD_CHUNK = 64
GDN_CHUNK = 128
INVERSE_BLOCK = 64
VMEM_LIMIT = 56 * 1024 * 1024
SCAN_SEQS_LONG = 2
SCAN_SEQS_SHORT = 4


def _pick_tile(total, cap, mult):
    best = None
    for t in range(mult, min(total, cap) + 1, mult):
        if total % t == 0:
            best = t
    assert best is not None, (total, cap, mult)
    return best


def _sigmoid(x):
    return 0.5 * jnp.tanh(0.5 * x) + 0.5


def _silu(x):
    half = 0.5 * x
    return half * jnp.tanh(half) + half


def _softplus(x):
    return jnp.maximum(x, 0.0) + jnp.log1p(jnp.exp(-jnp.abs(x)))


def _rms(x, w):
    return x * lax.rsqrt(jnp.mean(x * x, axis=-1, keepdims=True) + EPS) * w


def _dot(a, b):
    return jnp.dot(a.astype(BF16), b.astype(BF16), preferred_element_type=F32)


def _dot_nt(a, b):
    return lax.dot_general(a.astype(BF16), b.astype(BF16), (((1,), (1,)), ((), ())), preferred_element_type=F32)


def _dot_tn(a, b):
    return lax.dot_general(a.astype(BF16), b.astype(BF16), (((0,), (0,)), ((), ())), preferred_element_type=F32)


def _cumsum_rows(x):
    c = x.shape[0]
    tril = (lax.broadcasted_iota(jnp.int32, (c, c), 0) >= lax.broadcasted_iota(jnp.int32, (c, c), 1)).astype(F32)
    return jnp.dot(tril, x, precision=HIGHEST, preferred_element_type=F32)


def _causal_conv(buf, w_ref, width, rows):
    acc = None
    if len(buf.shape) == 2:
        xin = buf[0:SUBLANES + rows, :]
        for j in range(width):
            shift = width - 1 - j
            xs = pltpu.roll(xin, shift, axis=0) if shift else xin
            term = xs[SUBLANES:SUBLANES + rows, :] * w_ref[j:j + 1, :]
            acc = term if acc is None else acc + term
        return acc
    for j in range(width):
        off = SUBLANES - (width - 1) + j
        term = buf[:, off:off + rows, :] * w_ref[j:j + 1, :]
        acc = term if acc is None else acc + term
    return acc


def _norm_kernel(x_ref, w_ref, o_ref):
    o_ref[...] = _rms(x_ref[...], w_ref[...]).astype(o_ref.dtype)


def _norm(x, w, layer, out_dtype):
    t, d = x.shape
    tm = _pick_tile(t, 1024, 16)
    return pl.pallas_call(
        _norm_kernel,
        grid=(t // tm,),
        in_specs=[pl.BlockSpec((tm, d), lambda i: (i, 0)),
                  pl.BlockSpec((None, 1, d), lambda i: (layer, 0, 0))],
        out_specs=pl.BlockSpec((tm, d), lambda i: (i, 0)),
        out_shape=jax.ShapeDtypeStruct((t, d), out_dtype),
        compiler_params=pltpu.CompilerParams(dimension_semantics=("parallel",)),
        name="rmsnorm",
    )(x, w)


def _inproj_kernel(h_ref, w_ref, ws_ref, o_ref, os_ref):
    h = h_ref[...]
    o_ref[...] = jnp.dot(h, w_ref[...], preferred_element_type=F32)

    @pl.when(pl.program_id(1) == 0)
    def _():
        os_ref[...] = jnp.dot(h, ws_ref[...], preferred_element_type=F32)


def _inproj(h, w_main, w_slab, layer):
    t, d = h.shape
    ncols = w_main.shape[-1]
    tm = _pick_tile(t, 1536, 16)
    tn = _pick_tile(ncols, 1024, LANES)
    return pl.pallas_call(
        _inproj_kernel,
        grid=(t // tm, ncols // tn),
        in_specs=[pl.BlockSpec((tm, d), lambda i, j: (i, 0)),
                  pl.BlockSpec((None, d, tn), lambda i, j: (layer, 0, j)),
                  pl.BlockSpec((None, d, SLAB), lambda i, j: (layer, 0, 0))],
        out_specs=[pl.BlockSpec((tm, tn), lambda i, j: (i, j)),
                   pl.BlockSpec((tm, SLAB), lambda i, j: (i, 0))],
        out_shape=[jax.ShapeDtypeStruct((t, ncols), F32), jax.ShapeDtypeStruct((t, SLAB), F32)],
        compiler_params=pltpu.CompilerParams(dimension_semantics=("parallel", "arbitrary"),
                                             vmem_limit_bytes=VMEM_LIMIT),
        name="inproj",
    )(h, w_main, w_slab)


def _scan_call(body, name, proj3, slab3, col_blocks, state_in, conv_in, params, prev_state, out_cols,
               state_block, conv_cols, conv_width, scratch, layer, depth, *, c, v0, v1):
    b, lp, _ = proj3.shape
    nc = lp // c
    nb = _pick_tile(b, SCAN_SEQS_LONG if nc > 1 else SCAN_SEQS_SHORT, 1)
    vend = v1 - (nc - 1) * c
    hist = conv_width - 1
    assert hist <= vend <= c
    has_state = state_in is not None
    has_prev = prev_state is not None
    zeros = (0,) * len(state_block)
    lyr = lambda bi, i: (layer, 0, 0)
    in_specs = [pl.BlockSpec((nb, c, width), functools.partial(lambda blk, bi, i: (bi, i, blk), blk))
                for width, blk in col_blocks]
    in_specs.append(pl.BlockSpec((nb, c, SLAB), lambda bi, i: (bi, i, 0)))
    args = [proj3] * len(col_blocks) + [slab3]
    if has_state:
        in_specs += [pl.BlockSpec((None, nb) + state_block, lambda bi, i: (layer, bi) + zeros),
                     pl.BlockSpec((None, nb, hist, conv_cols), lambda bi, i: (layer, bi, 0, 0))]
        args += [state_in, conv_in]
    for p in params:
        in_specs.append(pl.BlockSpec((None,) + p.shape[1:], lyr))
        args.append(p)
    aliases = {}
    if has_prev:
        aliases[len(args)] = 1
        in_specs.append(pl.BlockSpec(memory_space=pl.ANY))
        args.append(prev_state)
    return pl.pallas_call(
        functools.partial(body, nb=nb, c=c, v0=v0, v1=v1, vend=vend, has_state=has_state, has_prev=has_prev),
        grid=(b // nb, nc),
        in_specs=in_specs,
        out_specs=[pl.BlockSpec((nb, c, out_cols), lambda bi, i: (bi, i, 0)),
                   pl.BlockSpec((None, nb) + state_block, lambda bi, i: (layer, bi) + zeros),
                   pl.BlockSpec((nb, hist, conv_cols), lambda bi, i: (bi, 0, 0))],
        out_shape=[jax.ShapeDtypeStruct((b, lp, out_cols), BF16),
                   jax.ShapeDtypeStruct((depth, b) + state_block, F32),
                   jax.ShapeDtypeStruct((b, hist, conv_cols), F32)],
        scratch_shapes=[pltpu.VMEM((nb, SUBLANES + c, conv_cols), F32)] + [s(nb) for s in scratch],
        input_output_aliases=aliases,
        compiler_params=pltpu.CompilerParams(dimension_semantics=("parallel", "arbitrary"),
                                             vmem_limit_bytes=VMEM_LIMIT),
        name=name,
    )(*args)


def _conv_step(buf, x_ref, cst_out, w_ref, width, c, vend):
    buf[SUBLANES:SUBLANES + c, :] = x_ref[...]
    out = _causal_conv(buf, w_ref, width, c)
    cst_out[...] = buf[SUBLANES + vend - (width - 1):SUBLANES + vend, :]
    buf[0:SUBLANES, :] = buf[c:c + SUBLANES, :]
    return out


def _conv_init(buf, cst_ref, width, cols):
    buf[0:SUBLANES, :] = jnp.zeros((SUBLANES, cols), F32)
    if cst_ref is not None:
        buf[SUBLANES - (width - 1):SUBLANES, :] = cst_ref[...]


def _pair_expand(arr, p, width, half):
    r = arr.shape[0]
    even = jnp.broadcast_to(arr[:, p:p + 1], (r, width))
    odd = jnp.broadcast_to(arr[:, SSM_PAIRS + p:SSM_PAIRS + p + 1], (r, width))
    return jnp.where(half, even, odd)


def _ssd_kernel(*refs, nb, c, v0, v1, vend, has_state, has_prev):
    refs = list(refs)
    xbc_ref, z_ref, sm_ref = refs[:3]
    k = 3
    st_ref = cst_ref = None
    if has_state:
        st_ref, cst_ref = refs[k:k + 2]
        k += 2
    cw_ref, cb_ref, bias_ref, alog_ref, dexp_ref, nw_ref = refs[k:k + 6]
    k += 6 + (1 if has_prev else 0)
    y_ref, sto_ref, csto_ref, buf = refs[k:k + 4]
    ht = None if has_state else refs[k + 4]
    ci = pl.program_id(1)

    @pl.when(ci == 0)
    def _():
        for s in range(nb):
            _conv_init(buf.at[s], cst_ref.at[s] if has_state else None, SSM_CONV, SSM_XBC)
            if has_state:
                sto_ref[s] = st_ref[s]
            else:
                ht[s] = jnp.zeros(ht.shape[1:], F32)

    row = lax.broadcasted_iota(jnp.int32, (c, SLAB), 0) + ci * c
    valid = jnp.logical_and(row >= v0, row < v1)
    neg_a = -jnp.exp(alog_ref[...])
    w2 = 2 * c
    lane2 = lax.broadcasted_iota(jnp.int32, (c, w2), 1)
    row2 = lax.broadcasted_iota(jnp.int32, (c, w2), 0)
    half2 = lane2 < c
    j2 = jnp.where(half2, lane2, lane2 - c)
    causal2 = j2 <= row2
    upto2 = row2 <= j2
    eye2 = j2 == row2
    half_p = lax.broadcasted_iota(jnp.int32, (c, LANES), 1) < SSM_HEAD_DIM
    even_rows = lax.broadcasted_iota(jnp.int32, (LANES, SSM_STATE), 0) < SSM_HEAD_DIM

    for s in range(nb):
        xbc = _silu(_conv_step(buf.at[s], xbc_ref.at[s], csto_ref.at[s], cw_ref, SSM_CONV, c, vend) + cb_ref[...])
        dt = jnp.where(valid, _softplus(sm_ref[s] + bias_ref[...]), 0.0)
        acum = _cumsum_rows(dt * neg_a)
        last = acum[c - 1:c, :]
        e_last = jnp.exp(last)
        for g in range(SSM_GROUPS):
            b_g = xbc[:, D_INNER + g * SSM_STATE:D_INNER + (g + 1) * SSM_STATE]
            c_g = xbc[:, D_INNER + SSM_BC + g * SSM_STATE:D_INNER + SSM_BC + (g + 1) * SSM_STATE]
            gcols = slice(g * GROUP_COLS, (g + 1) * GROUP_COLS)
            cb = _dot_nt(c_g, b_g)
            cb2 = jnp.concatenate([cb, cb], axis=1)
            y_off = _dot_nt(c_g, sto_ref[s, gcols, :]) if has_state else _dot(c_g, ht[s, :, gcols])
            ys, xws, els = [], [], []
            for q in range(PAIRS_PER_GROUP):
                p = g * PAIRS_PER_GROUP + q
                dx = _pair_expand(dt, p, w2, half2)
                ax = dx * _pair_expand(neg_a, p, w2, half2[0:1])
                row_cum = jnp.sum(jnp.where(upto2, ax, 0.0), axis=0, keepdims=True)
                dt_row = jnp.sum(jnp.where(eye2, dx, 0.0), axis=0, keepdims=True)
                col_cum = _pair_expand(acum, p, w2, half2)
                decay = jnp.where(causal2, jnp.exp(col_cum - row_cum), 0.0)
                wgt = decay * cb2 * dt_row
                xp = xbc[:, p * LANES:(p + 1) * LANES]
                rhs = jnp.concatenate([jnp.where(half_p, xp, 0.0), jnp.where(half_p, 0.0, xp)], axis=0)
                y = _dot(wgt, rhs)
                if w2 == LANES:
                    acum_p, dt_p = col_cum, dx
                else:
                    acum_p, dt_p = _pair_expand(acum, p, LANES, half_p), _pair_expand(dt, p, LANES, half_p)
                last_p = acum_p[c - 1:c, :]
                y = y + y_off[:, q * LANES:(q + 1) * LANES] * jnp.exp(acum_p)
                y = y + dexp_ref[:, p * LANES:(p + 1) * LANES] * xp
                ys.append(y)
                xws.append(xp * (jnp.exp(last_p - acum_p) * dt_p))
                els.append(jnp.exp(last_p))
            xw_g = jnp.concatenate(xws, axis=1)
            if has_state:
                scale = jnp.concatenate(
                    [jnp.where(even_rows,
                               jnp.broadcast_to(e_last[:, p:p + 1], (LANES, SSM_STATE)),
                               jnp.broadcast_to(e_last[:, SSM_PAIRS + p:SSM_PAIRS + p + 1], (LANES, SSM_STATE)))
                     for p in range(g * PAIRS_PER_GROUP, (g + 1) * PAIRS_PER_GROUP)], axis=0)
                sto_ref[s, gcols, :] = sto_ref[s, gcols, :] * scale + _dot_tn(xw_g, b_g)
            else:
                ht[s, :, gcols] = ht[s, :, gcols] * jnp.concatenate(els, axis=1) + _dot_tn(b_g, xw_g)
            yz = jnp.concatenate(ys, axis=1) * _silu(z_ref[s, :, gcols])
            y_ref[s, :, gcols] = _rms(yz, nw_ref[:, gcols]).astype(y_ref.dtype)

    if not has_state:
        @pl.when(ci == pl.num_programs(1) - 1)
        def _():
            for s in range(nb):
                for p in range(SSM_PAIRS):
                    sto_ref[s, p * LANES:(p + 1) * LANES, :] = ht[s, :, p * LANES:(p + 1) * LANES].T


def _ssd(proj3, slab3, st, cst, prev, prm, layer, depth, *, c, v0, v1):
    params = [prm["ssm_conv_w"], prm["ssm_conv_b"], prm["slab_bias"], prm["slab_alog"], prm["ssm_d_cols"],
              prm["ssm_norm_w"]]
    return _scan_call(_ssd_kernel, "ssd_scan", proj3, slab3, [(SSM_XBC, 0), (D_INNER, Z_BLOCK)], st, cst, params,
                      prev, D_INNER, (D_INNER, SSM_STATE), SSM_XBC, SSM_CONV,
                      [] if st is not None else [lambda nb: pltpu.VMEM((nb, SSM_STATE, D_INNER), F32)],
                      layer, depth, c=c, v0=v0, v1=v1)


def _head_products(x, y, c):
    per_op = min(GDN_HEADS, MXU_DIM // c)
    width = per_op * c
    lane = lax.broadcasted_iota(jnp.int32, (c, width), 1)
    outs = []
    head = lane // c if per_op > 1 else None
    for s in range(GDN_HEADS // per_op):
        ys = y[:, s * width:(s + 1) * width]
        diag = ys if per_op == 1 else jnp.concatenate(
            [jnp.where(head == r, ys, 0.0) for r in range(per_op)], axis=0)
        outs.append(_dot(x[:, s * width:(s + 1) * width], diag))
    return outs[0] if len(outs) == 1 else jnp.concatenate(outs, axis=1)


def _unit_lower_inverse(a, eye, same_half, c):
    blk = min(c, INVERSE_BLOCK)
    assert c in (blk, 2 * blk)
    a_d = a if blk == c else jnp.where(same_half, a, 0.0)
    n = -a_d
    t = eye + n
    m = _head_products(n, n, c)
    covered = 2
    while 2 * covered < blk:
        both = _head_products(jnp.concatenate([t, m], axis=0), m, c)
        t = t + both[0:c]
        m = both[c:2 * c]
        covered *= 2
    d = t + _head_products(t, m, c)
    if blk == c:
        return d
    a_o = jnp.where(same_half, 0.0, a)
    return d - _head_products(_head_products(d, a_o, c), d, c)


def _gdn_kernel(*refs, nb, c, v0, v1, vend, has_state, has_prev):
    refs = list(refs)
    qkv_ref, gate_ref, sm_ref = refs[:3]
    k = 3
    st_ref = cst_ref = None
    if has_state:
        st_ref, cst_ref = refs[k:k + 2]
        k += 2
    cw_ref, bias_ref, alog_ref, nw_ref = refs[k:k + 4]
    k += 4 + (1 if has_prev else 0)
    y_ref, sto_ref, csto_ref, buf = refs[k:]
    ci = pl.program_id(1)

    @pl.when(ci == 0)
    def _():
        for s in range(nb):
            _conv_init(buf.at[s], cst_ref.at[s] if has_state else None, GDN_CONV, GDN_QKV)
            sto_ref[s] = st_ref[s] if has_state else jnp.zeros(sto_ref.shape[1:], F32)

    row = lax.broadcasted_iota(jnp.int32, (c, SLAB), 0) + ci * c
    valid = jnp.logical_and(row >= v0, row < v1)
    neg_a = -jnp.exp(alog_ref[...])
    w8 = GDN_HEADS * c
    lane8 = lax.broadcasted_iota(jnp.int32, (c, w8), 1)
    row8 = lax.broadcasted_iota(jnp.int32, (c, w8), 0)
    j8 = lane8 & (c - 1)
    causal8 = j8 <= row8
    strict8 = j8 < row8
    upto8 = row8 <= j8
    eye8 = jnp.where(j8 == row8, 1.0, 0.0)
    same_half8 = (j8 // INVERSE_BLOCK) == (row8 // INVERSE_BLOCK)
    lane_k =lax.broadcasted_iota(jnp.int32, (c, 2 * GDN_DK), 1) < GDN_DK

    def heads_on_lanes(arr):
        return jnp.concatenate(
            [jnp.broadcast_to(arr[:, SLAB_GA + h:SLAB_GA + h + 1], (c, c)) for h in range(GDN_HEADS)], axis=1)

    for s in range(nb):
        qkv = _silu(_conv_step(buf.at[s], qkv_ref.at[s], csto_ref.at[s], cw_ref, GDN_CONV, c, vend))
        sm = sm_ref[s]
        g = jnp.where(valid, neg_a * _softplus(sm + bias_ref[...]), 0.0)
        beta = jnp.where(valid, _sigmoid(sm), 0.0)
        gcum = _cumsum_rows(g)
        e_g = jnp.exp(gcum)
        g_last = gcum[c - 1:c, :]
        e_last = jnp.exp(g_last)
        k_dec = jnp.exp(g_last - gcum)
        row_cum = jnp.sum(jnp.where(upto8, heads_on_lanes(g), 0.0), axis=0, keepdims=True)
        decay = jnp.where(causal8, jnp.exp(heads_on_lanes(gcum) - row_cum), 0.0)

        qs, ks, kbs = [], [], []
        for h in range(GDN_HEADS):
            qh = qkv[:, h * GDN_DK:(h + 1) * GDN_DK]
            kh = qkv[:, GDN_QK + h * GDN_DK:GDN_QK + (h + 1) * GDN_DK]
            qs.append(qh * lax.rsqrt(jnp.sum(qh * qh, axis=-1, keepdims=True) + EPS) * (GDN_DK ** -0.5))
            kn = kh * lax.rsqrt(jnp.sum(kh * kh, axis=-1, keepdims=True) + EPS)
            ks.append(kn)
            kbs.append(kn * beta[:, SLAB_GB + h:SLAB_GB + h + 1])

        kk_parts, qk_parts = [], []
        for hp in range(GDN_HEADS // 2):
            k2 = jnp.concatenate([ks[2 * hp], ks[2 * hp + 1]], axis=1)
            rhs_t = jnp.concatenate([jnp.where(lane_k, k2, 0.0), jnp.where(lane_k, 0.0, k2)], axis=0)
            lhs = jnp.concatenate([jnp.concatenate([kbs[2 * hp], kbs[2 * hp + 1]], axis=1),
                                   jnp.concatenate([qs[2 * hp], qs[2 * hp + 1]], axis=1)], axis=0)
            res = _dot_nt(lhs, rhs_t)
            kk_parts.append(res[0:c, :])
            qk_parts.append(res[c:2 * c, :])
        a_mat = jnp.where(strict8, jnp.concatenate(kk_parts, axis=1) * decay, 0.0)
        attn = jnp.concatenate(qk_parts, axis=1) * decay
        t_mat = _unit_lower_inverse(a_mat, eye8, same_half8, c)

        for h in range(GDN_HEADS):
            beta_h = beta[:, SLAB_GB + h:SLAB_GB + h + 1]
            eg_h = e_g[:, SLAB_GA + h:SLAB_GA + h + 1]
            vh = qkv[:, 2 * GDN_QK + h * GDN_DV:2 * GDN_QK + (h + 1) * GDN_DV]
            rhs = jnp.concatenate([vh * beta_h, kbs[h] * eg_h], axis=1)
            sol = _dot(t_mat[:, h * c:(h + 1) * c], rhs)
            attn_h = attn[:, h * c:(h + 1) * c]
            k_dec_h = ks[h] * k_dec[:, SLAB_GA + h:SLAB_GA + h + 1]
            if c % LANES == 0:
                both = _dot(jnp.concatenate([attn_h, k_dec_h.T], axis=0), sol)
                au_aw, ku_kw = both[0:c], both[c:]
            else:
                au_aw, ku_kw = _dot(attn_h, sol), _dot_tn(k_dec_h, sol)
            q_eff = qs[h] * eg_h - au_aw[:, GDN_DV:]
            s_old = sto_ref[s, h]
            through = _dot(jnp.concatenate([q_eff, ku_kw[:, GDN_DV:]], axis=0), s_old)
            o = through[0:c] + au_aw[:, 0:GDN_DV]
            sto_ref[s, h] = (s_old * e_last[:, SLAB_GA + h:SLAB_GA + h + 1] - through[c:] + ku_kw[:, 0:GDN_DV])
            gt = gate_ref[s, :, h * GDN_DV:(h + 1) * GDN_DV]
            y_ref[s, :, h * GDN_DV:(h + 1) * GDN_DV] = (_rms(o, nw_ref[...]) * _silu(gt)).astype(y_ref.dtype)


def _gdn(proj3, slab3, st, cst, prev, prm, layer, depth, *, c, v0, v1):
    params = [prm["gdn_conv_w"], prm["slab_bias"], prm["slab_alog"], prm["gdn_norm_w"]]
    return _scan_call(_gdn_kernel, "gdn_scan", proj3, slab3, [(GDN_QKV, 1), (GDN_HEADS * GDN_DV, GATE_BLOCK)],
                      st, cst, params, prev, GDN_HEADS * GDN_DV, (GDN_HEADS, GDN_DK, GDN_DV), GDN_QKV, GDN_CONV,
                      [], layer, depth, c=c, v0=v0, v1=v1)


def _mix_kernel(x_ref, ys_ref, yg_ref, ms_ref, mg_ref, wso_ref, wgo_ref, wo_ref, o_ref):
    y_ssm = jnp.dot(ys_ref[...], wso_ref[...], preferred_element_type=F32)
    y_gdn = jnp.dot(yg_ref[...], wgo_ref[...], preferred_element_type=F32)
    merged = _sigmoid(ms_ref[...]) * y_ssm + _sigmoid(mg_ref[...]) * y_gdn
    o_ref[...] = x_ref[...] + _dot(merged, wo_ref[...])


def _mix(x, y_ssm, y_gdn, proj, prm, layer):
    t, d = x.shape
    tm = _pick_tile(t, 512, 16)
    lyr = lambda i: (layer, 0, 0)
    return pl.pallas_call(
        _mix_kernel,
        grid=(t // tm,),
        in_specs=[pl.BlockSpec((tm, d), lambda i: (i, 0)),
                  pl.BlockSpec((tm, D_INNER), lambda i: (i, 0)),
                  pl.BlockSpec((tm, GDN_HEADS * GDN_DV), lambda i: (i, 0)),
                  pl.BlockSpec((tm, d), lambda i: (i, GATE_BLOCK + 1)),
                  pl.BlockSpec((tm, d), lambda i: (i, GATE_BLOCK + 2)),
                  pl.BlockSpec((None, D_INNER, d), lyr),
                  pl.BlockSpec((None, GDN_HEADS * GDN_DV, d), lyr),
                  pl.BlockSpec((None, d, d), lyr)],
        out_specs=pl.BlockSpec((tm, d), lambda i: (i, 0)),
        out_shape=jax.ShapeDtypeStruct((t, d), F32),
        compiler_params=pltpu.CompilerParams(dimension_semantics=("parallel",), vmem_limit_bytes=VMEM_LIMIT),
        name="mix_out",
    )(x, y_ssm, y_gdn, proj, proj, prm["w_ssm_out"], prm["w_gdn_out"], prm["w_o"])


def _ffn_kernel(*refs, nb, rows, vend, has_state):
    if has_state:
        (x_ref, cst_ref, nw_ref, wup_ref, cw_ref, cb_ref, wdn_ref, nnw_ref, xo_ref, hn_ref, csto_ref, buf) = refs
    else:
        (x_ref, nw_ref, wup_ref, cw_ref, cb_ref, wdn_ref, nnw_ref, xo_ref, hn_ref, csto_ref, buf) = refs
    ti = pl.program_id(1)
    last_tile = pl.num_programs(1) - 1
    hist = FFN_CONV - 1

    @pl.when(ti == 0)
    def _():
        buf[:, 0:SUBLANES, :] = jnp.zeros((nb, SUBLANES, D_FF), F32)
        if has_state:
            buf[:, SUBLANES - hist:SUBLANES, :] = cst_ref[...]

    x = x_ref[...]
    gu = _dot(_rms(x, nw_ref[...]), wup_ref[...])
    buf[:, SUBLANES:SUBLANES + rows, :] = gu[:, 0:D_FF].reshape(nb, rows, D_FF)
    conv = _causal_conv(buf.at[0] if nb == 1 else buf, cw_ref, FFN_CONV, rows)

    @pl.when(ti == last_tile)
    def _():
        csto_ref[...] = buf[:, SUBLANES + vend - hist:SUBLANES + vend, :]

    buf[:, 0:SUBLANES, :] = buf[:, rows:rows + SUBLANES, :]
    act = _silu(conv + cb_ref[...]).reshape(nb * rows, D_FF) * gu[:, D_FF:]
    out = x + _dot(act, wdn_ref[...])
    xo_ref[...] = out
    hn_ref[...] = _rms(out, nnw_ref[...]).astype(hn_ref.dtype)


def _ffn(x, cst, prm, layer, next_norm_w, next_layer, hn_dtype, *, b, lp, v1):
    t, d = x.shape
    has_state = cst is not None
    if lp <= 64:
        rows, nb = lp, _pick_tile(b, max(1, 256 // lp), 1)
    else:
        rows, nb = _pick_tile(lp, 384, 16), 1
    nt = lp // rows
    vend = v1 - (nt - 1) * rows
    assert FFN_CONV - 1 <= vend <= rows
    lyr = lambda bi, i: (layer, 0, 0)
    flat = lambda bi, i: (bi * nt + i, 0)
    per_b = lambda bi, i: (bi, 0, 0)
    in_specs = [pl.BlockSpec((nb * rows, d), flat)]
    args = [x]
    if has_state:
        in_specs.append(pl.BlockSpec((None, nb, FFN_CONV - 1, D_FF), lambda bi, i: (layer, bi, 0, 0)))
        args.append(cst)
    in_specs += [pl.BlockSpec((None, 1, d), lyr),
                 pl.BlockSpec((None, d, 2 * D_FF), lyr),
                 pl.BlockSpec((None, FFN_CONV, D_FF), lyr),
                 pl.BlockSpec((None, 1, D_FF), lyr),
                 pl.BlockSpec((None, D_FF, d), lyr),
                 pl.BlockSpec((None, 1, d), lambda bi, i: (next_layer, 0, 0))]
    args += [prm["norm_ffn_w"], prm["w_up"], prm["ffn_conv_w"], prm["ffn_conv_b"], prm["w_down"], next_norm_w]
    return pl.pallas_call(
        functools.partial(_ffn_kernel, nb=nb, rows=rows, vend=vend, has_state=has_state),
        grid=(b // nb, nt),
        in_specs=in_specs,
        out_specs=[pl.BlockSpec((nb * rows, d), flat),
                   pl.BlockSpec((nb * rows, d), flat),
                   pl.BlockSpec((nb, FFN_CONV - 1, D_FF), per_b)],
        out_shape=[jax.ShapeDtypeStruct((t, d), F32),
                   jax.ShapeDtypeStruct((t, d), hn_dtype),
                   jax.ShapeDtypeStruct((b, FFN_CONV - 1, D_FF), F32)],
        scratch_shapes=[pltpu.VMEM((nb, SUBLANES + rows, D_FF), F32)],
        compiler_params=pltpu.CompilerParams(dimension_semantics=("parallel", "arbitrary"),
                                             vmem_limit_bytes=VMEM_LIMIT),
        name="conv_ffn",
    )(*args)


def _prepare_params(norm_mix_w, w_in, ssm_conv_w, ssm_conv_b, ssm_dt_bias, ssm_a_log, ssm_d, ssm_norm_w,
                    gdn_conv_w, gdn_dt_bias, gdn_a_log, gdn_norm_w, w_ssm_out, w_gdn_out, w_o,
                    norm_ffn_w, w_up, ffn_conv_w, ffn_conv_b, w_down, norm_f_w):
    depth = w_in.shape[0]
    offs = [0]
    for s in IN_SIZES:
        offs.append(offs[-1] + s)
    col = lambda k: w_in[:, :, offs[k]:offs[k + 1]]
    w_main = jnp.concatenate([col(1), col(3), col(0), col(6), col(7), col(8)], axis=-1).astype(BF16)
    dt_cols = col(2)
    pad = jnp.zeros((depth, w_in.shape[1], SLAB - SSM_HEADS - 2 * GDN_HEADS), w_in.dtype)
    w_slab = jnp.concatenate([dt_cols[..., 0::2], dt_cols[..., 1::2], col(4), col(5), pad], axis=-1).astype(BF16)

    def slab(ssm_vec, gdn_vec):
        z = jnp.zeros((depth, SLAB - SSM_HEADS - GDN_HEADS), F32)
        return jnp.concatenate([ssm_vec[:, 0::2], ssm_vec[:, 1::2], gdn_vec, z], axis=-1)[:, None, :]

    return {
        "norm_mix_w": norm_mix_w[:, None, :],
        "w_main": w_main,
        "w_slab": w_slab,
        "ssm_conv_w": ssm_conv_w,
        "ssm_conv_b": ssm_conv_b[:, None, :],
        "slab_bias": slab(ssm_dt_bias, gdn_dt_bias),
        "slab_alog": slab(ssm_a_log, gdn_a_log),
        "ssm_d_cols": jnp.repeat(ssm_d, SSM_HEAD_DIM, axis=-1)[:, None, :],
        "ssm_norm_w": ssm_norm_w[:, None, :],
        "gdn_conv_w": gdn_conv_w,
        "gdn_norm_w": gdn_norm_w[:, None, :],
        "w_ssm_out": w_ssm_out.astype(BF16),
        "w_gdn_out": w_gdn_out.astype(BF16),
        "w_o": w_o.astype(BF16),
        "norm_ffn_w": norm_ffn_w[:, None, :],
        "w_up": w_up.astype(BF16),
        "ffn_conv_w": ffn_conv_w,
        "ffn_conv_b": ffn_conv_b[:, None, :],
        "w_down": w_down.astype(BF16),
        "norm_f_w": norm_f_w[None, None, :],
    }


def _run_trunk(x3, states, prm, *, c_ssd, c_gdn, v0, v1):
    b, lp, d = x3.shape
    depth = prm["w_main"].shape[0]
    x = x3.reshape(b * lp, d)
    h = _norm(x, prm["norm_mix_w"], 0, BF16)
    if states is None:
        st_ssm = cst_ssm = st_gdn = cst_gdn = cst_ffn = None
    else:
        st_ssm = states[0].reshape(depth, b, D_INNER, SSM_STATE)
        cst_ssm, st_gdn, cst_gdn, cst_ffn = states[1:]
    o_ssm = o_gdn = None
    conv_outs = [[] for _ in range(3)]
    for l in range(depth):
        proj, slab = _inproj(h, prm["w_main"], prm["w_slab"], l)
        proj3 = proj.reshape(b, lp, PROJ_COLS)
        slab3 = slab.reshape(b, lp, SLAB)
        y_ssm, o_ssm, o_ssm_conv = _ssd(proj3, slab3, st_ssm, cst_ssm, o_ssm, prm, l, depth, c=c_ssd, v0=v0, v1=v1)
        y_gdn, o_gdn, o_gdn_conv = _gdn(proj3, slab3, st_gdn, cst_gdn, o_gdn, prm, l, depth, c=c_gdn, v0=v0, v1=v1)
        x = _mix(x, y_ssm.reshape(b * lp, D_INNER), y_gdn.reshape(b * lp, GDN_HEADS * GDN_DV), proj, prm, l)
        if l + 1 < depth:
            x, h, o_ffn_conv = _ffn(x, cst_ffn, prm, l, prm["norm_mix_w"], l + 1, BF16, b=b, lp=lp, v1=v1)
        else:
            x, h, o_ffn_conv = _ffn(x, cst_ffn, prm, l, prm["norm_f_w"], 0, F32, b=b, lp=lp, v1=v1)
        conv_outs[0].append(o_ssm_conv)
        conv_outs[1].append(o_gdn_conv)
        conv_outs[2].append(o_ffn_conv)
    return (h.reshape(b, lp, d), o_ssm.reshape(depth, b, SSM_HEADS, SSM_HEAD_DIM, SSM_STATE),
            jnp.stack(conv_outs[0]), o_gdn, jnp.stack(conv_outs[1]), jnp.stack(conv_outs[2]))


def kernel(x_prompt, x_sample, state_ssm, state_ssm_conv, state_gdn, state_gdn_conv, state_ffn_conv, meta_tokens, norm_mix_w, w_in, ssm_conv_w, ssm_conv_b, ssm_dt_bias, ssm_a_log, ssm_d, ssm_norm_w, gdn_conv_w, gdn_dt_bias, gdn_a_log, gdn_norm_w, w_ssm_out, w_gdn_out, w_o, norm_ffn_w, w_up, ffn_conv_w, ffn_conv_b, w_down, norm_f_w):
    prm = _prepare_params(norm_mix_w, w_in, ssm_conv_w, ssm_conv_b, ssm_dt_bias, ssm_a_log, ssm_d, ssm_norm_w,
                          gdn_conv_w, gdn_dt_bias, gdn_a_log, gdn_norm_w, w_ssm_out, w_gdn_out, w_o,
                          norm_ffn_w, w_up, ffn_conv_w, ffn_conv_b, w_down, norm_f_w)
    bp, sp, d = x_prompt.shape
    bs, ls, _ = x_sample.shape
    c_p = max(SSD_CHUNK, GDN_CHUNK)
    assert c_p % SSD_CHUNK == 0 and c_p % GDN_CHUNK == 0
    lp = -(-(N_META + sp) // c_p) * c_p
    pad = lp - N_META - sp
    xp = jnp.concatenate([jnp.zeros((bp, pad, d), x_prompt.dtype),
                          jnp.broadcast_to(meta_tokens.astype(x_prompt.dtype), (bp, N_META, d)), x_prompt], axis=1)
    p_out = _run_trunk(xp, None, prm, c_ssd=SSD_CHUNK, c_gdn=GDN_CHUNK, v0=pad, v1=lp)
    y_prompt = p_out[0][:, pad + N_META:]
    c_s = -(-ls // SUBLANES) * SUBLANES
    xs = jnp.concatenate([x_sample, jnp.zeros((bs, c_s - ls, d), x_sample.dtype)], axis=1)
    s_out = _run_trunk(xs, (state_ssm, state_ssm_conv, state_gdn, state_gdn_conv, state_ffn_conv), prm,
                       c_ssd=c_s, c_gdn=c_s, v0=0, v1=ls)
    y_sample = s_out[0][:, :ls]
    return (y_prompt,) + (y_sample,) + p_out[1:] + s_out[1:]
```

```python
import functools

import jax
import jax.numpy as jnp
from jax import lax
from jax.experimental import pallas as pl
from jax.experimental.pallas import tpu as pltpu

F32 = jnp.float32
BF16 = jnp.bfloat16
HIGHEST = lax.Precision.HIGHEST

EPS = 1e-6
N_META = 16
D_MODEL = 1024
SSM_HEADS = 32
SSM_HEAD_DIM = 64
SSM_GROUPS = 4
SSM_STATE = 128
SSM_CONV = 4
D_INNER = SSM_HEADS * SSM_HEAD_DIM
SSM_BC = SSM_GROUPS * SSM_STATE
SSM_XBC = D_INNER + 2 * SSM_BC
SSM_PAIRS = SSM_HEADS // 2
PAIRS_PER_GROUP = SSM_PAIRS // SSM_GROUPS
GROUP_COLS = D_INNER // SSM_GROUPS
GDN_HEADS = 8
GDN_DK = 128
GDN_DV = 128
GDN_CONV = 4
GDN_QK = GDN_HEADS * GDN_DK
GDN_QKV = 2 * GDN_QK + GDN_HEADS * GDN_DV
D_FF = 2816
FFN_CONV = 3
IN_SIZES = (D_INNER, SSM_XBC, SSM_HEADS, GDN_QKV, GDN_HEADS, GDN_HEADS, GDN_HEADS * GDN_DV, D_MODEL, D_MODEL)
PROJ_COLS = SSM_XBC + GDN_QKV + D_INNER + 3 * D_MODEL
Z_BLOCK = (SSM_XBC + GDN_QKV) // D_INNER
GATE_BLOCK = (SSM_XBC + GDN_QKV + D_INNER) // D_MODEL
SLAB = 128
SLAB_GA = SSM_HEADS
SLAB_GB = SSM_HEADS + GDN_HEADS

LANES = 128
SUBLANES = 8
MXU_DIM = 256
SSD_CHUNK = 64
GDN_CHUNK = 128
INVERSE_BLOCK = 64
VMEM_LIMIT = 56 * 1024 * 1024
SCAN_SEQS_LONG = 2
SCAN_SEQS_SHORT = 8
INPROJ_COLS = 2816
FFN_ROWS = 576


def _pick_tile(total, cap, mult):
    best = None
    for t in range(mult, min(total, cap) + 1, mult):
        if total % t == 0:
            best = t
    assert best is not None, (total, cap, mult)
    return best


def _sigmoid(x):
    return 0.5 * jnp.tanh(0.5 * x) + 0.5


def _silu(x):
    half = 0.5 * x
    return half * jnp.tanh(half) + half


def _softplus(x):
    return jnp.maximum(x, 0.0) + jnp.log1p(jnp.exp(-jnp.abs(x)))


def _rms(x, w):
    return x * lax.rsqrt(jnp.mean(x * x, axis=-1, keepdims=True) + EPS) * w


def _dot(a, b):
    return jnp.dot(a.astype(BF16), b.astype(BF16), preferred_element_type=F32)


def _dot_nt(a, b):
    return lax.dot_general(a.astype(BF16), b.astype(BF16), (((1,), (1,)), ((), ())), preferred_element_type=F32)


def _dot_tn(a, b):
    return lax.dot_general(a.astype(BF16), b.astype(BF16), (((0,), (0,)), ((), ())), preferred_element_type=F32)


def _cumsum_rows(x):
    c = x.shape[0]
    tril = (lax.broadcasted_iota(jnp.int32, (c, c), 0) >= lax.broadcasted_iota(jnp.int32, (c, c), 1)).astype(F32)
    return jnp.dot(tril, x, precision=HIGHEST, preferred_element_type=F32)


def _causal_conv(buf, w_ref, width, rows):
    acc = None
    if len(buf.shape) == 2:
        xin = buf[0:SUBLANES + rows, :]
        for j in range(width):
            shift = width - 1 - j
            xs = pltpu.roll(xin, shift, axis=0) if shift else xin
            term = xs[SUBLANES:SUBLANES + rows, :] * w_ref[j:j + 1, :]
            acc = term if acc is None else acc + term
        return acc
    for j in range(width):
        off = SUBLANES - (width - 1) + j
        term = buf[:, off:off + rows, :] * w_ref[j:j + 1, :]
        acc = term if acc is None else acc + term
    return acc


def _norm_kernel(x_ref, w_ref, o_ref):
    o_ref[...] = _rms(x_ref[...], w_ref[...]).astype(o_ref.dtype)


def _norm(x, w, layer, out_dtype):
    t, d = x.shape
    tm = _pick_tile(t, 1024, 16)
    return pl.pallas_call(
        _norm_kernel,
        grid=(t // tm,),
        in_specs=[pl.BlockSpec((tm, d), lambda i: (i, 0)),
                  pl.BlockSpec((None, 1, d), lambda i: (layer, 0, 0))],
        out_specs=pl.BlockSpec((tm, d), lambda i: (i, 0)),
        out_shape=jax.ShapeDtypeStruct((t, d), out_dtype),
        compiler_params=pltpu.CompilerParams(dimension_semantics=("parallel",)),
        name="rmsnorm",
    )(x, w)


def _inproj_kernel(h_ref, w_ref, ws_ref, o_ref, os_ref):
    h = h_ref[...]
    o_ref[...] = jnp.dot(h, w_ref[...], preferred_element_type=F32)

    @pl.when(pl.program_id(1) == 0)
    def _():
        os_ref[...] = jnp.dot(h, ws_ref[...], preferred_element_type=F32)


def _inproj(h, w_main, w_slab, layer):
    t, d = h.shape
    ncols = w_main.shape[-1]
    tm = _pick_tile(t, 1536, 16)
    tn = _pick_tile(ncols, INPROJ_COLS, LANES)
    return pl.pallas_call(
        _inproj_kernel,
        grid=(t // tm, ncols // tn),
        in_specs=[pl.BlockSpec((tm, d), lambda i, j: (i, 0)),
                  pl.BlockSpec((None, d, tn), lambda i, j: (layer, 0, j)),
                  pl.BlockSpec((None, d, SLAB), lambda i, j: (layer, 0, 0))],
        out_specs=[pl.BlockSpec((tm, tn), lambda i, j: (i, j)),
                   pl.BlockSpec((tm, SLAB), lambda i, j: (i, 0))],
        out_shape=[jax.ShapeDtypeStruct((t, ncols), F32), jax.ShapeDtypeStruct((t, SLAB), F32)],
        compiler_params=pltpu.CompilerParams(dimension_semantics=("parallel", "arbitrary"),
                                             vmem_limit_bytes=VMEM_LIMIT),
        name="inproj",
    )(h, w_main, w_slab)


def _scan_call(body, name, proj3, slab3, col_blocks, state_in, conv_in, params, prev_state, out_cols,
               state_block, conv_cols, conv_width, scratch, layer, depth, *, c, v0, v1):
    b, lp, _ = proj3.shape
    nc = lp // c
    nb = _pick_tile(b, SCAN_SEQS_LONG if nc > 1 else SCAN_SEQS_SHORT, 1)
    vend = v1 - (nc - 1) * c
    hist = conv_width - 1
    assert hist <= vend <= c
    has_state = state_in is not None
    has_prev = prev_state is not None
    zeros = (0,) * len(state_block)
    lyr = lambda bi, i: (layer, 0, 0)
    in_specs = [pl.BlockSpec((nb, c, width), functools.partial(lambda blk, bi, i: (bi, i, blk), blk))
                for width, blk in col_blocks]
    in_specs.append(pl.BlockSpec((nb, c, SLAB), lambda bi, i: (bi, i, 0)))
    args = [proj3] * len(col_blocks) + [slab3]
    if has_state:
        in_specs += [pl.BlockSpec((None, nb) + state_block, lambda bi, i: (layer, bi) + zeros),
                     pl.BlockSpec((None, nb, hist, conv_cols), lambda bi, i: (layer, bi, 0, 0))]
        args += [state_in, conv_in]
    for p in params:
        in_specs.append(pl.BlockSpec((None,) + p.shape[1:], lyr))
        args.append(p)
    aliases = {}
    if has_prev:
        aliases[len(args)] = 1
        in_specs.append(pl.BlockSpec(memory_space=pl.ANY))
        args.append(prev_state)
    return pl.pallas_call(
        functools.partial(body, nb=nb, c=c, v0=v0, v1=v1, vend=vend, has_state=has_state, has_prev=has_prev),
        grid=(b // nb, nc),
        in_specs=in_specs,
        out_specs=[pl.BlockSpec((nb, c, out_cols), lambda bi, i: (bi, i, 0)),
                   pl.BlockSpec((None, nb) + state_block, lambda bi, i: (layer, bi) + zeros),
                   pl.BlockSpec((nb, hist, conv_cols), lambda bi, i: (bi, 0, 0))],
        out_shape=[jax.ShapeDtypeStruct((b, lp, out_cols), BF16),
                   jax.ShapeDtypeStruct((depth, b) + state_block, F32),
                   jax.ShapeDtypeStruct((b, hist, conv_cols), F32)],
        scratch_shapes=[pltpu.VMEM((nb, SUBLANES + c, conv_cols), F32)] + [s(nb) for s in scratch],
        input_output_aliases=aliases,
        compiler_params=pltpu.CompilerParams(dimension_semantics=("parallel", "arbitrary"),
                                             vmem_limit_bytes=VMEM_LIMIT),
        name=name,
    )(*args)


def _conv_step(buf, x_ref, cst_out, w_ref, width, c, vend):
    buf[SUBLANES:SUBLANES + c, :] = x_ref[...]
    out = _causal_conv(buf, w_ref, width, c)
    cst_out[...] = buf[SUBLANES + vend - (width - 1):SUBLANES + vend, :]
    buf[0:SUBLANES, :] = buf[c:c + SUBLANES, :]
    return out


def _conv_init(buf, cst_ref, width, cols):
    buf[0:SUBLANES, :] = jnp.zeros((SUBLANES, cols), F32)
    if cst_ref is not None:
        buf[SUBLANES - (width - 1):SUBLANES, :] = cst_ref[...]


def _pair_expand(arr, p, width, half):
    r = arr.shape[0]
    even = jnp.broadcast_to(arr[:, p:p + 1], (r, width))
    odd = jnp.broadcast_to(arr[:, SSM_PAIRS + p:SSM_PAIRS + p + 1], (r, width))
    return jnp.where(half, even, odd)


def _ssd_kernel(*refs, nb, c, v0, v1, vend, has_state, has_prev):
    refs = list(refs)
    xbc_ref, z_ref, sm_ref = refs[:3]
    k = 3
    st_ref = cst_ref = None
    if has_state:
        st_ref, cst_ref = refs[k:k + 2]
        k += 2
    cw_ref, cb_ref, bias_ref, alog_ref, dexp_ref, nw_ref = refs[k:k + 6]
    k += 6 + (1 if has_prev else 0)
    y_ref, sto_ref, csto_ref, buf = refs[k:k + 4]
    ht = None if has_state else refs[k + 4]
    ci = pl.program_id(1)

    @pl.when(ci == 0)
    def _():
        for s in range(nb):
            _conv_init(buf.at[s], cst_ref.at[s] if has_state else None, SSM_CONV, SSM_XBC)
            if has_state:
                sto_ref[s] = st_ref[s]
            else:
                ht[s] = jnp.zeros(ht.shape[1:], F32)

    row = lax.broadcasted_iota(jnp.int32, (c, SLAB), 0) + ci * c
    valid = jnp.logical_and(row >= v0, row < v1)
    neg_a = -jnp.exp(alog_ref[...])
    w2 = 2 * c
    lane2 = lax.broadcasted_iota(jnp.int32, (c, w2), 1)
    row2 = lax.broadcasted_iota(jnp.int32, (c, w2), 0)
    half2 = lane2 < c
    j2 = jnp.where(half2, lane2, lane2 - c)
    causal2 = j2 <= row2
    upto2 = row2 <= j2
    eye2 = j2 == row2
    half_p = lax.broadcasted_iota(jnp.int32, (c, LANES), 1) < SSM_HEAD_DIM
    even_rows = lax.broadcasted_iota(jnp.int32, (LANES, SSM_STATE), 0) < SSM_HEAD_DIM

    seqs, groups, pairs = range(nb), range(SSM_GROUPS), range(SSM_PAIRS)
    gcols = [slice(g * GROUP_COLS, (g + 1) * GROUP_COLS) for g in groups]
    pcols = [slice(p * LANES, (p + 1) * LANES) for p in pairs]
    xbc = [_silu(_conv_step(buf.at[s], xbc_ref.at[s], csto_ref.at[s], cw_ref, SSM_CONV, c, vend) + cb_ref[...])
           for s in seqs]
    dt = [jnp.where(valid, _softplus(sm_ref[s] + bias_ref[...]), 0.0) for s in seqs]
    acum = [_cumsum_rows(dt[s] * neg_a) for s in seqs]
    b_g = [[xbc[s][:, D_INNER + g * SSM_STATE:D_INNER + (g + 1) * SSM_STATE] for g in groups] for s in seqs]
    c_g = [[xbc[s][:, D_INNER + SSM_BC + g * SSM_STATE:D_INNER + SSM_BC + (g + 1) * SSM_STATE] for g in groups]
           for s in seqs]
    cb = [[_dot_nt(c_g[s][g], b_g[s][g]) for g in groups] for s in seqs]
    y_off = [[_dot_nt(c_g[s][g], sto_ref[s, gcols[g], :]) if has_state else _dot(c_g[s][g], ht[s, :, gcols[g]])
              for g in groups] for s in seqs]

    def intra(s, p):
        g = p // PAIRS_PER_GROUP
        dx = _pair_expand(dt[s], p, w2, half2)
        ax = dx * _pair_expand(neg_a, p, w2, half2[0:1])
        row_cum = jnp.sum(jnp.where(upto2, ax, 0.0), axis=0, keepdims=True)
        dt_row = jnp.sum(jnp.where(eye2, dx, 0.0), axis=0, keepdims=True)
        col_cum = _pair_expand(acum[s], p, w2, half2)
        decay = jnp.where(causal2, jnp.exp(col_cum - row_cum), 0.0)
        wgt = decay * jnp.concatenate([cb[s][g], cb[s][g]], axis=1) * dt_row
        xp = xbc[s][:, pcols[p]]
        rhs = jnp.concatenate([jnp.where(half_p, xp, 0.0), jnp.where(half_p, 0.0, xp)], axis=0)
        if w2 == LANES:
            acum_p, dt_p = col_cum, dx
        else:
            acum_p, dt_p = _pair_expand(acum[s], p, LANES, half_p), _pair_expand(dt[s], p, LANES, half_p)
        return _dot(wgt, rhs), acum_p, dt_p

    parts = [[intra(s, p) for p in pairs] for s in seqs]

    def pair_outputs(s, p):
        y_in, acum_p, dt_p = parts[s][p]
        g, q = divmod(p, PAIRS_PER_GROUP)
        xp = xbc[s][:, pcols[p]]
        last_p = acum_p[c - 1:c, :]
        y = y_in + y_off[s][g][:, q * LANES:(q + 1) * LANES] * jnp.exp(acum_p) + dexp_ref[:, pcols[p]] * xp
        return y, xp * (jnp.exp(last_p - acum_p) * dt_p), jnp.exp(last_p)

    outs = [[pair_outputs(s, p) for p in pairs] for s in seqs]
    in_group = lambda s, g, k: [outs[s][p][k] for p in range(g * PAIRS_PER_GROUP, (g + 1) * PAIRS_PER_GROUP)]
    xw = [[jnp.concatenate(in_group(s, g, 1), axis=1) for g in groups] for s in seqs]
    upd = [[_dot_tn(xw[s][g], b_g[s][g]) if has_state else _dot_tn(b_g[s][g], xw[s][g]) for g in groups]
           for s in seqs]
    for s in seqs:
        e_last = jnp.exp(acum[s][c - 1:c, :])
        for g in groups:
            if has_state:
                scale = jnp.concatenate(
                    [jnp.where(even_rows,
                               jnp.broadcast_to(e_last[:, p:p + 1], (LANES, SSM_STATE)),
                               jnp.broadcast_to(e_last[:, SSM_PAIRS + p:SSM_PAIRS + p + 1], (LANES, SSM_STATE)))
                     for p in range(g * PAIRS_PER_GROUP, (g + 1) * PAIRS_PER_GROUP)], axis=0)
                sto_ref[s, gcols[g], :] = sto_ref[s, gcols[g], :] * scale + upd[s][g]
            else:
                ht[s, :, gcols[g]] = ht[s, :, gcols[g]] * jnp.concatenate(in_group(s, g, 2), axis=1) + upd[s][g]
            yz = jnp.concatenate(in_group(s, g, 0), axis=1) * _silu(z_ref[s, :, gcols[g]])
            y_ref[s, :, gcols[g]] = _rms(yz, nw_ref[:, gcols[g]]).astype(y_ref.dtype)

    if not has_state:
        @pl.when(ci == pl.num_programs(1) - 1)
        def _():
            for s in range(nb):
                for p in range(SSM_PAIRS):
                    sto_ref[s, p * LANES:(p + 1) * LANES, :] = ht[s, :, p * LANES:(p + 1) * LANES].T


def _ssd(proj3, slab3, st, cst, prev, prm, layer, depth, *, c, v0, v1):
    params = [prm["ssm_conv_w"], prm["ssm_conv_b"], prm["slab_bias"], prm["slab_alog"], prm["ssm_d_cols"],
              prm["ssm_norm_w"]]
    return _scan_call(_ssd_kernel, "ssd_scan", proj3, slab3, [(SSM_XBC, 0), (D_INNER, Z_BLOCK)], st, cst, params,
                      prev, D_INNER, (D_INNER, SSM_STATE), SSM_XBC, SSM_CONV,
                      [] if st is not None else [lambda nb: pltpu.VMEM((nb, SSM_STATE, D_INNER), F32)],
                      layer, depth, c=c, v0=v0, v1=v1)


def _head_products(x, y, c):
    per_op = min(GDN_HEADS, MXU_DIM // c)
    width = per_op * c
    lane = lax.broadcasted_iota(jnp.int32, (c, width), 1)
    outs = []
    head = lane // c if per_op > 1 else None
    for s in range(GDN_HEADS // per_op):
        ys = y[:, s * width:(s + 1) * width]
        diag = ys if per_op == 1 else jnp.concatenate(
            [jnp.where(head == r, ys, 0.0) for r in range(per_op)], axis=0)
        outs.append(_dot(x[:, s * width:(s + 1) * width], diag))
    return outs[0] if len(outs) == 1 else jnp.concatenate(outs, axis=1)


def _unit_lower_inverse(mats, eye, same_half, c):
    blk = min(c, INVERSE_BLOCK)
    assert c in (blk, 2 * blk)
    ns = [-(a if blk == c else jnp.where(same_half, a, 0.0)) for a in mats]
    ts = [eye + n for n in ns]
    ms = [_head_products(n, n, c) for n in ns]
    covered = 2
    while 2 * covered < blk:
        boths = [_head_products(jnp.concatenate([t, m], axis=0), m, c) for t, m in zip(ts, ms)]
        ts = [t + both[0:c] for t, both in zip(ts, boths)]
        ms = [both[c:2 * c] for both in boths]
        covered *= 2
    ds = [t + _head_products(t, m, c) for t, m in zip(ts, ms)]
    if blk == c:
        return ds
    firsts = [_head_products(d, jnp.where(same_half, 0.0, a), c) for d, a in zip(ds, mats)]
    return [d - _head_products(f, d, c) for d, f in zip(ds, firsts)]


def _gdn_kernel(*refs, nb, c, v0, v1, vend, has_state, has_prev):
    refs = list(refs)
    qkv_ref, gate_ref, sm_ref = refs[:3]
    k = 3
    st_ref = cst_ref = None
    if has_state:
        st_ref, cst_ref = refs[k:k + 2]
        k += 2
    cw_ref, bias_ref, alog_ref, nw_ref = refs[k:k + 4]
    k += 4 + (1 if has_prev else 0)
    y_ref, sto_ref, csto_ref, buf = refs[k:]
    ci = pl.program_id(1)

    @pl.when(ci == 0)
    def _():
        for s in range(nb):
            _conv_init(buf.at[s], cst_ref.at[s] if has_state else None, GDN_CONV, GDN_QKV)
            sto_ref[s] = st_ref[s] if has_state else jnp.zeros(sto_ref.shape[1:], F32)

    row = lax.broadcasted_iota(jnp.int32, (c, SLAB), 0) + ci * c
    valid = jnp.logical_and(row >= v0, row < v1)
    neg_a = -jnp.exp(alog_ref[...])
    w8 = GDN_HEADS * c
    lane8 = lax.broadcasted_iota(jnp.int32, (c, w8), 1)
    row8 = lax.broadcasted_iota(jnp.int32, (c, w8), 0)
    j8 = lane8 & (c - 1)
    causal8 = j8 <= row8
    strict8 = j8 < row8
    upto8 = row8 <= j8
    eye8 = jnp.where(j8 == row8, 1.0, 0.0)
    same_half8 = (j8 // INVERSE_BLOCK) == (row8 // INVERSE_BLOCK)
    lane_k =lax.broadcasted_iota(jnp.int32, (c, 2 * GDN_DK), 1) < GDN_DK

    def heads_on_lanes(arr):
        return jnp.concatenate(
            [jnp.broadcast_to(arr[:, SLAB_GA + h:SLAB_GA + h + 1], (c, c)) for h in range(GDN_HEADS)], axis=1)

    seqs, heads = range(nb), range(GDN_HEADS)
    col_ga = lambda arr, h: arr[:, SLAB_GA + h:SLAB_GA + h + 1]
    col_gb = lambda arr, h: arr[:, SLAB_GB + h:SLAB_GB + h + 1]
    qkv = [_silu(_conv_step(buf.at[s], qkv_ref.at[s], csto_ref.at[s], cw_ref, GDN_CONV, c, vend)) for s in seqs]
    g = [jnp.where(valid, neg_a * _softplus(sm_ref[s] + bias_ref[...]), 0.0) for s in seqs]
    beta = [jnp.where(valid, _sigmoid(sm_ref[s]), 0.0) for s in seqs]
    gcum = [_cumsum_rows(g[s]) for s in seqs]
    e_g = [jnp.exp(gcum[s]) for s in seqs]
    e_last = [jnp.exp(gcum[s][c - 1:c, :]) for s in seqs]
    k_dec = [jnp.exp(gcum[s][c - 1:c, :] - gcum[s]) for s in seqs]
    row_cum = [jnp.sum(jnp.where(upto8, heads_on_lanes(g[s]), 0.0), axis=0, keepdims=True) for s in seqs]
    decay = [jnp.where(causal8, jnp.exp(heads_on_lanes(gcum[s]) - row_cum[s]), 0.0) for s in seqs]

    def unit(x):
        return x * lax.rsqrt(jnp.sum(x * x, axis=-1, keepdims=True) + EPS)

    qs = [[unit(qkv[s][:, h * GDN_DK:(h + 1) * GDN_DK]) * (GDN_DK ** -0.5) for h in heads] for s in seqs]
    ks = [[unit(qkv[s][:, GDN_QK + h * GDN_DK:GDN_QK + (h + 1) * GDN_DK]) for h in heads] for s in seqs]
    kbs = [[ks[s][h] * col_gb(beta[s], h) for h in heads] for s in seqs]

    def pair_scores(s, hp):
        k2 = jnp.concatenate([ks[s][2 * hp], ks[s][2 * hp + 1]], axis=1)
        rhs_t = jnp.concatenate([jnp.where(lane_k, k2, 0.0), jnp.where(lane_k, 0.0, k2)], axis=0)
        lhs = jnp.concatenate([jnp.concatenate([kbs[s][2 * hp], kbs[s][2 * hp + 1]], axis=1),
                               jnp.concatenate([qs[s][2 * hp], qs[s][2 * hp + 1]], axis=1)], axis=0)
        return _dot_nt(lhs, rhs_t)

    scores = [[pair_scores(s, hp) for hp in range(GDN_HEADS // 2)] for s in seqs]
    a_mat = [jnp.where(strict8, jnp.concatenate([r[0:c, :] for r in scores[s]], axis=1) * decay[s], 0.0) for s in seqs]
    attn = [jnp.concatenate([r[c:2 * c, :] for r in scores[s]], axis=1) * decay[s] for s in seqs]
    t_mat = _unit_lower_inverse(a_mat, eye8, same_half8, c)

    def solve(s, h):
        vh = qkv[s][:, 2 * GDN_QK + h * GDN_DV:2 * GDN_QK + (h + 1) * GDN_DV]
        rhs = jnp.concatenate([vh * col_gb(beta[s], h), kbs[s][h] * col_ga(e_g[s], h)], axis=1)
        return _dot(t_mat[s][:, h * c:(h + 1) * c], rhs)

    sol = [[solve(s, h) for h in heads] for s in seqs]

    def chunk_products(s, h):
        attn_h = attn[s][:, h * c:(h + 1) * c]
        k_dec_h = ks[s][h] * col_ga(k_dec[s], h)
        if c % LANES == 0:
            both = _dot(jnp.concatenate([attn_h, k_dec_h.T], axis=0), sol[s][h])
            return both[0:c], both[c:]
        return _dot(attn_h, sol[s][h]), _dot_tn(k_dec_h, sol[s][h])

    prods = [[chunk_products(s, h) for h in heads] for s in seqs]

    def through_state(s, h):
        au_aw, ku_kw = prods[s][h]
        q_eff = qs[s][h] * col_ga(e_g[s], h) - au_aw[:, GDN_DV:]
        return _dot(jnp.concatenate([q_eff, ku_kw[:, GDN_DV:]], axis=0), sto_ref[s, h])

    through = [[through_state(s, h) for h in heads] for s in seqs]
    for s in seqs:
        for h in heads:
            au_aw, ku_kw = prods[s][h]
            o = through[s][h][0:c] + au_aw[:, 0:GDN_DV]
            sto_ref[s, h] = sto_ref[s, h] * col_ga(e_last[s], h) - through[s][h][c:] + ku_kw[:, 0:GDN_DV]
            gt = gate_ref[s, :, h * GDN_DV:(h + 1) * GDN_DV]
            y_ref[s, :, h * GDN_DV:(h + 1) * GDN_DV] = (_rms(o, nw_ref[...]) * _silu(gt)).astype(y_ref.dtype)


def _gdn(proj3, slab3, st, cst, prev, prm, layer, depth, *, c, v0, v1):
    params = [prm["gdn_conv_w"], prm["slab_bias"], prm["slab_alog"], prm["gdn_norm_w"]]
    return _scan_call(_gdn_kernel, "gdn_scan", proj3, slab3, [(GDN_QKV, 1), (GDN_HEADS * GDN_DV, GATE_BLOCK)],
                      st, cst, params, prev, GDN_HEADS * GDN_DV, (GDN_HEADS, GDN_DK, GDN_DV), GDN_QKV, GDN_CONV,
                      [], layer, depth, c=c, v0=v0, v1=v1)


def _mix_kernel(x_ref, ys_ref, yg_ref, ms_ref, mg_ref, wso_ref, wgo_ref, wo_ref, o_ref):
    y_ssm = jnp.dot(ys_ref[...], wso_ref[...], preferred_element_type=F32)
    y_gdn = jnp.dot(yg_ref[...], wgo_ref[...], preferred_element_type=F32)
    merged = _sigmoid(ms_ref[...]) * y_ssm + _sigmoid(mg_ref[...]) * y_gdn
    o_ref[...] = x_ref[...] + _dot(merged, wo_ref[...])


def _mix(x, y_ssm, y_gdn, proj, prm, layer):
    t, d = x.shape
    tm = _pick_tile(t, 512, 16)
    lyr = lambda i: (layer, 0, 0)
    return pl.pallas_call(
        _mix_kernel,
        grid=(t // tm,),
        in_specs=[pl.BlockSpec((tm, d), lambda i: (i, 0)),
                  pl.BlockSpec((tm, D_INNER), lambda i: (i, 0)),
                  pl.BlockSpec((tm, GDN_HEADS * GDN_DV), lambda i: (i, 0)),
                  pl.BlockSpec((tm, d), lambda i: (i, GATE_BLOCK + 1)),
                  pl.BlockSpec((tm, d), lambda i: (i, GATE_BLOCK + 2)),
                  pl.BlockSpec((None, D_INNER, d), lyr),
                  pl.BlockSpec((None, GDN_HEADS * GDN_DV, d), lyr),
                  pl.BlockSpec((None, d, d), lyr)],
        out_specs=pl.BlockSpec((tm, d), lambda i: (i, 0)),
        out_shape=jax.ShapeDtypeStruct((t, d), F32),
        compiler_params=pltpu.CompilerParams(dimension_semantics=("parallel",), vmem_limit_bytes=VMEM_LIMIT),
        name="mix_out",
    )(x, y_ssm, y_gdn, proj, proj, prm["w_ssm_out"], prm["w_gdn_out"], prm["w_o"])


def _ffn_kernel(*refs, nb, rows, vend, has_state):
    if has_state:
        (x_ref, cst_ref, nw_ref, wup_ref, cw_ref, cb_ref, wdn_ref, nnw_ref, xo_ref, hn_ref, csto_ref, buf) = refs
    else:
        (x_ref, nw_ref, wup_ref, cw_ref, cb_ref, wdn_ref, nnw_ref, xo_ref, hn_ref, csto_ref, buf) = refs
    ti = pl.program_id(1)
    last_tile = pl.num_programs(1) - 1
    hist = FFN_CONV - 1

    @pl.when(ti == 0)
    def _():
        buf[:, 0:SUBLANES, :] = jnp.zeros((nb, SUBLANES, D_FF), F32)
        if has_state:
            buf[:, SUBLANES - hist:SUBLANES, :] = cst_ref[...]

    x = x_ref[...]
    gu = _dot(_rms(x, nw_ref[...]), wup_ref[...])
    buf[:, SUBLANES:SUBLANES + rows, :] = gu[:, 0:D_FF].reshape(nb, rows, D_FF)
    conv = _causal_conv(buf.at[0] if nb == 1 else buf, cw_ref, FFN_CONV, rows)

    @pl.when(ti == last_tile)
    def _():
        csto_ref[...] = buf[:, SUBLANES + vend - hist:SUBLANES + vend, :]

    buf[:, 0:SUBLANES, :] = buf[:, rows:rows + SUBLANES, :]
    act = _silu(conv + cb_ref[...]).reshape(nb * rows, D_FF) * gu[:, D_FF:]
    out = x + _dot(act, wdn_ref[...])
    xo_ref[...] = out
    hn_ref[...] = _rms(out, nnw_ref[...]).astype(hn_ref.dtype)


def _ffn(x, cst, prm, layer, next_norm_w, next_layer, hn_dtype, *, b, lp, v1):
    t, d = x.shape
    has_state = cst is not None
    if lp <= 64:
        rows, nb = lp, _pick_tile(b, max(1, 256 // lp), 1)
    else:
        rows, nb = _pick_tile(lp, FFN_ROWS, 16), 1
    nt = lp // rows
    vend = v1 - (nt - 1) * rows
    assert FFN_CONV - 1 <= vend <= rows
    lyr = lambda bi, i: (layer, 0, 0)
    flat = lambda bi, i: (bi * nt + i, 0)
    per_b = lambda bi, i: (bi, 0, 0)
    in_specs = [pl.BlockSpec((nb * rows, d), flat)]
    args = [x]
    if has_state:
        in_specs.append(pl.BlockSpec((None, nb, FFN_CONV - 1, D_FF), lambda bi, i: (layer, bi, 0, 0)))
        args.append(cst)
    in_specs += [pl.BlockSpec((None, 1, d), lyr),
                 pl.BlockSpec((None, d, 2 * D_FF), lyr, pipeline_mode=pl.Buffered(1)),
                 pl.BlockSpec((None, FFN_CONV, D_FF), lyr),
                 pl.BlockSpec((None, 1, D_FF), lyr),
                 pl.BlockSpec((None, D_FF, d), lyr, pipeline_mode=pl.Buffered(1)),
                 pl.BlockSpec((None, 1, d), lambda bi, i: (next_layer, 0, 0))]
    args += [prm["norm_ffn_w"], prm["w_up"], prm["ffn_conv_w"], prm["ffn_conv_b"], prm["w_down"], next_norm_w]
    return pl.pallas_call(
        functools.partial(_ffn_kernel, nb=nb, rows=rows, vend=vend, has_state=has_state),
        grid=(b // nb, nt),
        in_specs=in_specs,
        out_specs=[pl.BlockSpec((nb * rows, d), flat),
                   pl.BlockSpec((nb * rows, d), flat),
                   pl.BlockSpec((nb, FFN_CONV - 1, D_FF), per_b)],
        out_shape=[jax.ShapeDtypeStruct((t, d), F32),
                   jax.ShapeDtypeStruct((t, d), hn_dtype),
                   jax.ShapeDtypeStruct((b, FFN_CONV - 1, D_FF), F32)],
        scratch_shapes=[pltpu.VMEM((nb, SUBLANES + rows, D_FF), F32)],
        compiler_params=pltpu.CompilerParams(dimension_semantics=("parallel", "arbitrary"),
                                             vmem_limit_bytes=VMEM_LIMIT),
        name="conv_ffn",
    )(*args)


def _prepare_params(norm_mix_w, w_in, ssm_conv_w, ssm_conv_b, ssm_dt_bias, ssm_a_log, ssm_d, ssm_norm_w,
                    gdn_conv_w, gdn_dt_bias, gdn_a_log, gdn_norm_w, w_ssm_out, w_gdn_out, w_o,
                    norm_ffn_w, w_up, ffn_conv_w, ffn_conv_b, w_down, norm_f_w):
    depth = w_in.shape[0]
    offs = [0]
    for s in IN_SIZES:
        offs.append(offs[-1] + s)
    col = lambda k: w_in[:, :, offs[k]:offs[k + 1]]
    w_main = jnp.concatenate([col(1), col(3), col(0), col(6), col(7), col(8)], axis=-1).astype(BF16)
    dt_cols = col(2)
    pad = jnp.zeros((depth, w_in.shape[1], SLAB - SSM_HEADS - 2 * GDN_HEADS), w_in.dtype)
    w_slab = jnp.concatenate([dt_cols[..., 0::2], dt_cols[..., 1::2], col(4), col(5), pad], axis=-1).astype(BF16)

    def slab(ssm_vec, gdn_vec):
        z = jnp.zeros((depth, SLAB - SSM_HEADS - GDN_HEADS), F32)
        return jnp.concatenate([ssm_vec[:, 0::2], ssm_vec[:, 1::2], gdn_vec, z], axis=-1)[:, None, :]

    return {
        "norm_mix_w": norm_mix_w[:, None, :],
        "w_main": w_main,
        "w_slab": w_slab,
        "ssm_conv_w": ssm_conv_w,
        "ssm_conv_b": ssm_conv_b[:, None, :],
        "slab_bias": slab(ssm_dt_bias, gdn_dt_bias),
        "slab_alog": slab(ssm_a_log, gdn_a_log),
        "ssm_d_cols": jnp.repeat(ssm_d, SSM_HEAD_DIM, axis=-1)[:, None, :],
        "ssm_norm_w": ssm_norm_w[:, None, :],
        "gdn_conv_w": gdn_conv_w,
        "gdn_norm_w": gdn_norm_w[:, None, :],
        "w_ssm_out": w_ssm_out.astype(BF16),
        "w_gdn_out": w_gdn_out.astype(BF16),
        "w_o": w_o.astype(BF16),
        "norm_ffn_w": norm_ffn_w[:, None, :],
        "w_up": w_up.astype(BF16),
        "ffn_conv_w": ffn_conv_w,
        "ffn_conv_b": ffn_conv_b[:, None, :],
        "w_down": w_down.astype(BF16),
        "norm_f_w": norm_f_w[None, None, :],
    }


def _run_trunk(x3, states, prm, *, c_ssd, c_gdn, v0, v1):
    b, lp, d = x3.shape
    depth = prm["w_main"].shape[0]
    x = x3.reshape(b * lp, d)
    h = _norm(x, prm["norm_mix_w"], 0, BF16)
    if states is None:
        st_ssm = cst_ssm = st_gdn = cst_gdn = cst_ffn = None
    else:
        st_ssm = states[0].reshape(depth, b, D_INNER, SSM_STATE)
        cst_ssm, st_gdn, cst_gdn, cst_ffn = states[1:]
    o_ssm = o_gdn = None
    conv_outs = [[] for _ in range(3)]
    for l in range(depth):
        proj, slab = _inproj(h, prm["w_main"], prm["w_slab"], l)
        proj3 = proj.reshape(b, lp, PROJ_COLS)
        slab3 = slab.reshape(b, lp, SLAB)
        y_ssm, o_ssm, o_ssm_conv = _ssd(proj3, slab3, st_ssm, cst_ssm, o_ssm, prm, l, depth, c=c_ssd, v0=v0, v1=v1)
        y_gdn, o_gdn, o_gdn_conv = _gdn(proj3, slab3, st_gdn, cst_gdn, o_gdn, prm, l, depth, c=c_gdn, v0=v0, v1=v1)
        x = _mix(x, y_ssm.reshape(b * lp, D_INNER), y_gdn.reshape(b * lp, GDN_HEADS * GDN_DV), proj, prm, l)
        if l + 1 < depth:
            x, h, o_ffn_conv = _ffn(x, cst_ffn, prm, l, prm["norm_mix_w"], l + 1, BF16, b=b, lp=lp, v1=v1)
        else:
            x, h, o_ffn_conv = _ffn(x, cst_ffn, prm, l, prm["norm_f_w"], 0, F32, b=b, lp=lp, v1=v1)
        conv_outs[0].append(o_ssm_conv)
        conv_outs[1].append(o_gdn_conv)
        conv_outs[2].append(o_ffn_conv)
    return (h.reshape(b, lp, d), o_ssm.reshape(depth, b, SSM_HEADS, SSM_HEAD_DIM, SSM_STATE),
            jnp.stack(conv_outs[0]), o_gdn, jnp.stack(conv_outs[1]), jnp.stack(conv_outs[2]))


def kernel(x_prompt, x_sample, state_ssm, state_ssm_conv, state_gdn, state_gdn_conv, state_ffn_conv, meta_tokens, norm_mix_w, w_in, ssm_conv_w, ssm_conv_b, ssm_dt_bias, ssm_a_log, ssm_d, ssm_norm_w, gdn_conv_w, gdn_dt_bias, gdn_a_log, gdn_norm_w, w_ssm_out, w_gdn_out, w_o, norm_ffn_w, w_up, ffn_conv_w, ffn_conv_b, w_down, norm_f_w):
    prm = _prepare_params(norm_mix_w, w_in, ssm_conv_w, ssm_conv_b, ssm_dt_bias, ssm_a_log, ssm_d, ssm_norm_w,
                          gdn_conv_w, gdn_dt_bias, gdn_a_log, gdn_norm_w, w_ssm_out, w_gdn_out, w_o,
                          norm_ffn_w, w_up, ffn_conv_w, ffn_conv_b, w_down, norm_f_w)
    bp, sp, d = x_prompt.shape
    bs, ls, _ = x_sample.shape
    c_p = max(SSD_CHUNK, GDN_CHUNK)
    assert c_p % SSD_CHUNK == 0 and c_p % GDN_CHUNK == 0
    lp = -(-(N_META + sp) // c_p) * c_p
    pad = lp - N_META - sp
    xp = jnp.concatenate([jnp.zeros((bp, pad, d), x_prompt.dtype),
                          jnp.broadcast_to(meta_tokens.astype(x_prompt.dtype), (bp, N_META, d)), x_prompt], axis=1)
    p_out = _run_trunk(xp, None, prm, c_ssd=SSD_CHUNK, c_gdn=GDN_CHUNK, v0=pad, v1=lp)
    y_prompt = p_out[0][:, pad + N_META:]
    c_s = -(-ls // SUBLANES) * SUBLANES
    xs = jnp.concatenate([x_sample, jnp.zeros((bs, c_s - ls, d), x_sample.dtype)], axis=1)
    s_out = _run_trunk(xs, (state_ssm, state_ssm_conv, state_gdn, state_gdn_conv, state_ffn_conv), prm,
                       c_ssd=c_s, c_gdn=c_s, v0=0, v1=ls)
    y_sample = s_out[0][:, :ls]
    return (y_prompt,) + (y_sample,) + p_out[1:] + s_out[1:]
```

```python
import functools

import jax
import jax.numpy as jnp
from jax import lax
from jax.experimental import pallas as pl
from jax.experimental.pallas import tpu as pltpu

F32 = jnp.float32
BF16 = jnp.bfloat16
HIGHEST = lax.Precision.HIGHEST

EPS = 1e-6
N_META = 16
D_MODEL = 1024
SSM_HEADS = 32
SSM_HEAD_DIM = 64
SSM_GROUPS = 4
SSM_STATE = 128
SSM_CONV = 4
D_INNER = SSM_HEADS * SSM_HEAD_DIM
SSM_BC = SSM_GROUPS * SSM_STATE
SSM_XBC = D_INNER + 2 * SSM_BC
SSM_PAIRS = SSM_HEADS // 2
PAIRS_PER_GROUP = SSM_PAIRS // SSM_GROUPS
GROUP_COLS = D_INNER // SSM_GROUPS
GDN_HEADS = 8
GDN_DK = 128
GDN_DV = 128
GDN_CONV = 4
GDN_QK = GDN_HEADS * GDN_DK
GDN_QKV = 2 * GDN_QK + GDN_HEADS * GDN_DV
D_FF = 2816
FFN_CONV = 3
IN_SIZES = (D_INNER, SSM_XBC, SSM_HEADS, GDN_QKV, GDN_HEADS, GDN_HEADS, GDN_HEADS * GDN_DV, D_MODEL, D_MODEL)
CONV_COLS = SSM_XBC + GDN_QKV
PLAIN_COLS = D_INNER + 3 * D_MODEL
GATE_BLOCK = D_INNER // D_MODEL
SLAB = 128
SLAB_GA = SSM_HEADS
SLAB_GB = SSM_HEADS + GDN_HEADS

LANES = 128
SUBLANES = 8
MXU_DIM = 256
SSD_CHUNK = 64
GDN_CHUNK = 128
INVERSE_BLOCK = 64
VMEM_LIMIT = 56 * 1024 * 1024
SCAN_SEQS_LONG = 2
SCAN_SEQS_SHORT = 8
INPROJ_COLS = 2816
CONV_SUB_ROWS = 160
FFN_ROWS = 576


def _pick_tile(total, cap, mult):
    best = None
    for t in range(mult, min(total, cap) + 1, mult):
        if total % t == 0:
            best = t
    assert best is not None, (total, cap, mult)
    return best


def _sigmoid(x):
    return 0.5 * jnp.tanh(0.5 * x) + 0.5


def _silu_of_half(half):
    return half * jnp.tanh(half) + half


def _silu(x):
    return _silu_of_half(0.5 * x)


def _softplus(x):
    return jnp.maximum(x, 0.0) + jnp.log1p(jnp.exp(-jnp.abs(x)))


def _rms(x, w):
    return x * lax.rsqrt(jnp.mean(x * x, axis=-1, keepdims=True) + EPS) * w


def _dot(a, b):
    return jnp.dot(a.astype(BF16), b.astype(BF16), preferred_element_type=F32)


def _dot_nt(a, b):
    return lax.dot_general(a.astype(BF16), b.astype(BF16), (((1,), (1,)), ((), ())), preferred_element_type=F32)


def _dot_tn(a, b):
    return lax.dot_general(a.astype(BF16), b.astype(BF16), (((0,), (0,)), ((), ())), preferred_element_type=F32)


def _cumsum_rows(x):
    c = x.shape[0]
    tril = (lax.broadcasted_iota(jnp.int32, (c, c), 0) >= lax.broadcasted_iota(jnp.int32, (c, c), 1)).astype(F32)
    return jnp.dot(tril, x, precision=HIGHEST, preferred_element_type=F32)


def _causal_conv(buf, w_ref, width, rows):
    acc = None
    if len(buf.shape) == 2:
        cols = buf.shape[1]
        tiles = buf[0:SUBLANES + rows, :].reshape(1 + rows // SUBLANES, SUBLANES, cols)
        sublane = lax.broadcasted_iota(jnp.int32, (1, SUBLANES, cols), 1)
        shifted = [tiles[1:]]
        rot = tiles
        for shift in range(1, width):
            rot = pltpu.roll(rot, 1, axis=1)
            shifted.append(jnp.where(sublane < shift, rot[:-1], rot[1:]))
        for j in range(width):
            term = shifted[width - 1 - j] * w_ref[j:j + 1, :]
            acc = term if acc is None else acc + term
        return acc.reshape(rows, cols)
    for j in range(width):
        off = SUBLANES - (width - 1) + j
        term = buf[:, off:off + rows, :] * w_ref[j:j + 1, :]
        acc = term if acc is None else acc + term
    return acc


def _norm_kernel(x_ref, w_ref, o_ref):
    o_ref[...] = _rms(x_ref[...], w_ref[...]).astype(o_ref.dtype)


def _norm(x, w, layer, out_dtype):
    t, d = x.shape
    tm = _pick_tile(t, 1024, 16)
    return pl.pallas_call(
        _norm_kernel,
        grid=(t // tm,),
        in_specs=[pl.BlockSpec((tm, d), lambda i: (i, 0)),
                  pl.BlockSpec((None, 1, d), lambda i: (layer, 0, 0))],
        out_specs=pl.BlockSpec((tm, d), lambda i: (i, 0)),
        out_shape=jax.ShapeDtypeStruct((t, d), out_dtype),
        compiler_params=pltpu.CompilerParams(dimension_semantics=("parallel",)),
        name="rmsnorm",
    )(x, w)


def _inproj_kernel(*refs, with_slab):
    if with_slab:
        h_ref, w_ref, ws_ref, o_ref, os_ref = refs
    else:
        h_ref, w_ref, o_ref = refs
    h = h_ref[...]
    o_ref[...] = jnp.dot(h, w_ref[...], preferred_element_type=F32)
    if with_slab:
        @pl.when(pl.program_id(1) == 0)
        def _():
            os_ref[...] = jnp.dot(h, ws_ref[...], preferred_element_type=F32)


def _inproj(h, w_main, w_slab, layer):
    t, d = h.shape
    ncols = w_main.shape[-1]
    tm = _pick_tile(t, 1536, 16)
    tn = _pick_tile(ncols, INPROJ_COLS, LANES)
    with_slab = w_slab is not None
    in_specs = [pl.BlockSpec((tm, d), lambda i, j: (i, 0)),
                pl.BlockSpec((None, d, tn), lambda i, j: (layer, 0, j))]
    out_specs = [pl.BlockSpec((tm, tn), lambda i, j: (i, j))]
    out_shape = [jax.ShapeDtypeStruct((t, ncols), F32)]
    args = [h, w_main]
    if with_slab:
        in_specs.append(pl.BlockSpec((None, d, SLAB), lambda i, j: (layer, 0, 0)))
        out_specs.append(pl.BlockSpec((tm, SLAB), lambda i, j: (i, 0)))
        out_shape.append(jax.ShapeDtypeStruct((t, SLAB), F32))
        args.append(w_slab)
    outs = pl.pallas_call(
        functools.partial(_inproj_kernel, with_slab=with_slab),
        grid=(t // tm, ncols // tn),
        in_specs=in_specs,
        out_specs=out_specs,
        out_shape=out_shape,
        compiler_params=pltpu.CompilerParams(dimension_semantics=("parallel", "arbitrary"),
                                             vmem_limit_bytes=VMEM_LIMIT),
        name="inproj",
    )(*args)
    return outs if with_slab else outs[0]


def _inproj_conv_kernel(h_ref, w_ref, cw_ref, cb_ref, o_ref, cst_ref, pre, *, tm, sub, tiles_per_seq, width):
    @pl.when(pl.program_id(1) % tiles_per_seq == 0)
    def _():
        pre[0:SUBLANES, :] = jnp.zeros((SUBLANES, pre.shape[1]), F32)

    def project(k):
        pre[SUBLANES + k * sub:SUBLANES + (k + 1) * sub, :] = jnp.dot(
            h_ref[k * sub:(k + 1) * sub, :], w_ref[...], preferred_element_type=F32)

    def activate(k):
        o_ref[k * sub:(k + 1) * sub, :] = _silu_of_half(
            _causal_conv(pre.at[k * sub:SUBLANES + (k + 1) * sub, :], cw_ref, width, sub) + cb_ref[...])

    nsub = tm // sub
    project(0)
    for k in range(1, nsub):
        project(k)
        activate(k - 1)
    activate(nsub - 1)
    cst_ref[...] = pre[tm:tm + SUBLANES, :]
    pre[0:SUBLANES, :] = pre[tm:tm + SUBLANES, :]


def _inproj_conv(h, w_conv, conv_w, conv_b, layer, *, b, lp, width):
    t, d = h.shape
    ncols = w_conv.shape[-1]
    tm = _pick_tile(lp, 1536, 16)
    sub = _pick_tile(tm, CONV_SUB_ROWS, SUBLANES)
    tn = _pick_tile(ncols, 1536, LANES)
    tiles_per_seq = lp // tm
    lyr = lambda j, i: (layer, 0, j)
    return pl.pallas_call(
        functools.partial(_inproj_conv_kernel, tm=tm, sub=sub, tiles_per_seq=tiles_per_seq, width=width),
        grid=(ncols // tn, t // tm),
        in_specs=[pl.BlockSpec((tm, d), lambda j, i: (i, 0)),
                  pl.BlockSpec((None, d, tn), lyr),
                  pl.BlockSpec((None, width, tn), lyr),
                  pl.BlockSpec((None, 1, tn), lyr)],
        out_specs=[pl.BlockSpec((tm, tn), lambda j, i: (i, j)),
                   pl.BlockSpec((None, SUBLANES, tn), lambda j, i: (i // tiles_per_seq, 0, j))],
        out_shape=[jax.ShapeDtypeStruct((t, ncols), F32), jax.ShapeDtypeStruct((b, SUBLANES, ncols), F32)],
        scratch_shapes=[pltpu.VMEM((SUBLANES + tm, tn), F32)],
        compiler_params=pltpu.CompilerParams(dimension_semantics=("parallel", "arbitrary"),
                                             vmem_limit_bytes=VMEM_LIMIT),
        name="inproj_conv",
    )(h, w_conv, conv_w, conv_b)


def _scan_call(body, name, col_blocks, slab3, state_in, conv_in, params, prev_state, out_cols,
               state_block, conv_cols, conv_width, scratch, layer, depth, *, c, v0, v1, fused_conv):
    b, lp, _ = slab3.shape
    nc = lp // c
    nb = _pick_tile(b, SCAN_SEQS_LONG if nc > 1 else SCAN_SEQS_SHORT, 1)
    vend = v1 - (nc - 1) * c
    hist = conv_width - 1
    assert hist <= vend <= c
    has_state = state_in is not None
    has_prev = prev_state is not None
    assert not (fused_conv and has_state)
    zeros = (0,) * len(state_block)
    lyr = lambda bi, i: (layer, 0, 0)
    in_specs = [pl.BlockSpec((nb, c, width), functools.partial(lambda blk, bi, i: (bi, i, blk), blk))
                for _, width, blk in col_blocks]
    in_specs.append(pl.BlockSpec((nb, c, SLAB), lambda bi, i: (bi, i, 0)))
    args = [arr for arr, _, _ in col_blocks] + [slab3]
    if has_state:
        in_specs += [pl.BlockSpec((None, nb) + state_block, lambda bi, i: (layer, bi) + zeros),
                     pl.BlockSpec((None, nb, hist, conv_cols), lambda bi, i: (layer, bi, 0, 0))]
        args += [state_in, conv_in]
    for p in params:
        in_specs.append(pl.BlockSpec((None,) + p.shape[1:], lyr))
        args.append(p)
    aliases = {}
    if has_prev:
        aliases[len(args)] = 1
        in_specs.append(pl.BlockSpec(memory_space=pl.ANY))
        args.append(prev_state)
    out_specs = [pl.BlockSpec((nb, c, out_cols), lambda bi, i: (bi, i, 0)),
                 pl.BlockSpec((None, nb) + state_block, lambda bi, i: (layer, bi) + zeros)]
    out_shape = [jax.ShapeDtypeStruct((b, lp, out_cols), BF16),
                 jax.ShapeDtypeStruct((depth, b) + state_block, F32)]
    scratch_shapes = [s(nb) for s in scratch]
    if not fused_conv:
        out_specs.append(pl.BlockSpec((nb, hist, conv_cols), lambda bi, i: (bi, 0, 0)))
        out_shape.append(jax.ShapeDtypeStruct((b, hist, conv_cols), F32))
        scratch_shapes = [pltpu.VMEM((nb, SUBLANES + c, conv_cols), F32)] + scratch_shapes
    return pl.pallas_call(
        functools.partial(body, nb=nb, c=c, v0=v0, v1=v1, vend=vend, has_state=has_state, has_prev=has_prev,
                          fused_conv=fused_conv),
        grid=(b // nb, nc),
        in_specs=in_specs,
        out_specs=out_specs,
        out_shape=out_shape,
        scratch_shapes=scratch_shapes,
        input_output_aliases=aliases,
        compiler_params=pltpu.CompilerParams(dimension_semantics=("parallel", "arbitrary"),
                                             vmem_limit_bytes=VMEM_LIMIT),
        name=name,
    )(*args)


def _conv_step(buf, x_ref, cst_out, w_ref, width, c, vend):
    buf[SUBLANES:SUBLANES + c, :] = x_ref[...]
    out = _causal_conv(buf, w_ref, width, c)
    cst_out[...] = buf[SUBLANES + vend - (width - 1):SUBLANES + vend, :]
    buf[0:SUBLANES, :] = buf[c:c + SUBLANES, :]
    return out


def _conv_init(buf, cst_ref, width, cols):
    buf[0:SUBLANES, :] = jnp.zeros((SUBLANES, cols), F32)
    if cst_ref is not None:
        buf[SUBLANES - (width - 1):SUBLANES, :] = cst_ref[...]


def _pair_expand(arr, p, width, half):
    r = arr.shape[0]
    even = jnp.broadcast_to(arr[:, p:p + 1], (r, width))
    odd = jnp.broadcast_to(arr[:, SSM_PAIRS + p:SSM_PAIRS + p + 1], (r, width))
    return jnp.where(half, even, odd)


def _ssd_kernel(*refs, nb, c, v0, v1, vend, has_state, has_prev, fused_conv):
    refs = list(refs)
    xbc_ref, z_ref, sm_ref = refs[:3]
    k = 3
    st_ref = cst_ref = None
    if has_state:
        st_ref, cst_ref = refs[k:k + 2]
        k += 2
    cw_ref, cb_ref, bias_ref, alog_ref, dexp_ref, nw_ref = refs[k:k + 6]
    k += 6 + (1 if has_prev else 0)
    y_ref, sto_ref = refs[k:k + 2]
    k += 2
    csto_ref = buf = None
    if not fused_conv:
        csto_ref, buf = refs[k], refs[k + 1]
        k += 1
    ht = None if has_state else refs[-1]
    ci = pl.program_id(1)

    @pl.when(ci == 0)
    def _():
        for s in range(nb):
            if not fused_conv:
                _conv_init(buf.at[s], cst_ref.at[s] if has_state else None, SSM_CONV, SSM_XBC)
            if has_state:
                sto_ref[s] = st_ref[s]
            else:
                ht[s] = jnp.zeros(ht.shape[1:], F32)

    row = lax.broadcasted_iota(jnp.int32, (c, SLAB), 0) + ci * c
    valid = jnp.logical_and(row >= v0, row < v1)
    neg_a = -jnp.exp(alog_ref[...])
    w2 = 2 * c
    lane2 = lax.broadcasted_iota(jnp.int32, (c, w2), 1)
    row2 = lax.broadcasted_iota(jnp.int32, (c, w2), 0)
    half2 = lane2 < c
    j2 = jnp.where(half2, lane2, lane2 - c)
    causal2 = j2 <= row2
    upto2 = row2 <= j2
    eye2 = j2 == row2
    half_p = lax.broadcasted_iota(jnp.int32, (c, LANES), 1) < SSM_HEAD_DIM
    even_rows = lax.broadcasted_iota(jnp.int32, (LANES, SSM_STATE), 0) < SSM_HEAD_DIM

    seqs, groups, pairs = range(nb), range(SSM_GROUPS), range(SSM_PAIRS)
    gcols = [slice(g * GROUP_COLS, (g + 1) * GROUP_COLS) for g in groups]
    pcols = [slice(p * LANES, (p + 1) * LANES) for p in pairs]
    if fused_conv:
        xbc = [xbc_ref[s] for s in seqs]
    else:
        xbc = [_silu_of_half(_conv_step(buf.at[s], xbc_ref.at[s], csto_ref.at[s], cw_ref, SSM_CONV, c, vend)
                             + cb_ref[...])
               for s in seqs]
    dt = [jnp.where(valid, _softplus(sm_ref[s] + bias_ref[...]), 0.0) for s in seqs]
    acum = [_cumsum_rows(dt[s] * neg_a) for s in seqs]
    b_g = [[xbc[s][:, D_INNER + g * SSM_STATE:D_INNER + (g + 1) * SSM_STATE] for g in groups] for s in seqs]
    c_g = [[xbc[s][:, D_INNER + SSM_BC + g * SSM_STATE:D_INNER + SSM_BC + (g + 1) * SSM_STATE] for g in groups]
           for s in seqs]
    cb = [[_dot_nt(c_g[s][g], b_g[s][g]) for g in groups] for s in seqs]
    y_off = [[_dot_nt(c_g[s][g], sto_ref[s, gcols[g], :]) if has_state else _dot(c_g[s][g], ht[s, :, gcols[g]])
              for g in groups] for s in seqs]

    def intra(s, p):
        g = p // PAIRS_PER_GROUP
        dx = _pair_expand(dt[s], p, w2, half2)
        ax = dx * _pair_expand(neg_a, p, w2, half2[0:1])
        row_cum = jnp.sum(jnp.where(upto2, ax, 0.0), axis=0, keepdims=True)
        dt_row = jnp.sum(jnp.where(eye2, dx, 0.0), axis=0, keepdims=True)
        col_cum = _pair_expand(acum[s], p, w2, half2)
        decay = jnp.where(causal2, jnp.exp(col_cum - row_cum), 0.0)
        wgt = decay * jnp.concatenate([cb[s][g], cb[s][g]], axis=1) * dt_row
        xp = xbc[s][:, pcols[p]]
        rhs = jnp.concatenate([jnp.where(half_p, xp, 0.0), jnp.where(half_p, 0.0, xp)], axis=0)
        if w2 == LANES:
            acum_p, dt_p = col_cum, dx
        else:
            acum_p, dt_p = _pair_expand(acum[s], p, LANES, half_p), _pair_expand(dt[s], p, LANES, half_p)
        return _dot(wgt, rhs), acum_p, dt_p

    parts = [[intra(s, p) for p in pairs] for s in seqs]

    def pair_outputs(s, p):
        y_in, acum_p, dt_p = parts[s][p]
        g, q = divmod(p, PAIRS_PER_GROUP)
        xp = xbc[s][:, pcols[p]]
        last_p = acum_p[c - 1:c, :]
        y = y_in + y_off[s][g][:, q * LANES:(q + 1) * LANES] * jnp.exp(acum_p) + dexp_ref[:, pcols[p]] * xp
        return y, xp * (jnp.exp(last_p - acum_p) * dt_p), jnp.exp(last_p)

    outs = [[pair_outputs(s, p) for p in pairs] for s in seqs]
    in_group = lambda s, g, k: [outs[s][p][k] for p in range(g * PAIRS_PER_GROUP, (g + 1) * PAIRS_PER_GROUP)]
    xw = [[jnp.concatenate(in_group(s, g, 1), axis=1) for g in groups] for s in seqs]
    upd = [[_dot_tn(xw[s][g], b_g[s][g]) if has_state else _dot_tn(b_g[s][g], xw[s][g]) for g in groups]
           for s in seqs]
    for s in seqs:
        e_last = jnp.exp(acum[s][c - 1:c, :])
        for g in groups:
            if has_state:
                scale = jnp.concatenate(
                    [jnp.where(even_rows,
                               jnp.broadcast_to(e_last[:, p:p + 1], (LANES, SSM_STATE)),
                               jnp.broadcast_to(e_last[:, SSM_PAIRS + p:SSM_PAIRS + p + 1], (LANES, SSM_STATE)))
                     for p in range(g * PAIRS_PER_GROUP, (g + 1) * PAIRS_PER_GROUP)], axis=0)
                sto_ref[s, gcols[g], :] = sto_ref[s, gcols[g], :] * scale + upd[s][g]
            else:
                ht[s, :, gcols[g]] = ht[s, :, gcols[g]] * jnp.concatenate(in_group(s, g, 2), axis=1) + upd[s][g]
            yz = jnp.concatenate(in_group(s, g, 0), axis=1) * _silu(z_ref[s, :, gcols[g]])
            y_ref[s, :, gcols[g]] = _rms(yz, nw_ref[:, gcols[g]]).astype(y_ref.dtype)

    if not has_state:
        @pl.when(ci == pl.num_programs(1) - 1)
        def _():
            for s in range(nb):
                for p in range(SSM_PAIRS):
                    sto_ref[s, p * LANES:(p + 1) * LANES, :] = ht[s, :, p * LANES:(p + 1) * LANES].T


def _ssd(conv3, plain3, slab3, st, cst, prev, prm, layer, depth, *, c, v0, v1, fused_conv):
    params = [prm["ssm_conv_w"], prm["ssm_conv_b"], prm["slab_bias"], prm["slab_alog"], prm["ssm_d_cols"],
              prm["ssm_norm_w"]]
    return _scan_call(_ssd_kernel, "ssd_scan", [(conv3, SSM_XBC, 0), (plain3, D_INNER, 0)], slab3, st, cst, params,
                      prev, D_INNER, (D_INNER, SSM_STATE), SSM_XBC, SSM_CONV,
                      [] if st is not None else [lambda nb: pltpu.VMEM((nb, SSM_STATE, D_INNER), F32)],
                      layer, depth, c=c, v0=v0, v1=v1, fused_conv=fused_conv)


def _head_products(x, y, c):
    per_op = min(GDN_HEADS, MXU_DIM // c)
    width = per_op * c
    lane = lax.broadcasted_iota(jnp.int32, (c, width), 1)
    outs = []
    head = lane // c if per_op > 1 else None
    for s in range(GDN_HEADS // per_op):
        ys = y[:, s * width:(s + 1) * width]
        diag = ys if per_op == 1 else jnp.concatenate(
            [jnp.where(head == r, ys, 0.0) for r in range(per_op)], axis=0)
        outs.append(_dot(x[:, s * width:(s + 1) * width], diag))
    return outs[0] if len(outs) == 1 else jnp.concatenate(outs, axis=1)


def _unit_lower_inverse(mats, eye, same_half, c):
    blk = min(c, INVERSE_BLOCK)
    assert c in (blk, 2 * blk)
    ns = [-(a if blk == c else jnp.where(same_half, a, 0.0)) for a in mats]
    ts = [eye + n for n in ns]
    ms = [_head_products(n, n, c) for n in ns]
    covered = 2
    while 2 * covered < blk:
        boths = [_head_products(jnp.concatenate([t, m], axis=0), m, c) for t, m in zip(ts, ms)]
        ts = [t + both[0:c] for t, both in zip(ts, boths)]
        ms = [both[c:2 * c] for both in boths]
        covered *= 2
    ds = [t + _head_products(t, m, c) for t, m in zip(ts, ms)]
    if blk == c:
        return ds
    firsts = [_head_products(d, jnp.where(same_half, 0.0, a), c) for d, a in zip(ds, mats)]
    return [d - _head_products(f, d, c) for d, f in zip(ds, firsts)]


def _gdn_kernel(*refs, nb, c, v0, v1, vend, has_state, has_prev, fused_conv):
    refs = list(refs)
    qkv_ref, gate_ref, sm_ref = refs[:3]
    k = 3
    st_ref = cst_ref = None
    if has_state:
        st_ref, cst_ref = refs[k:k + 2]
        k += 2
    cw_ref, bias_ref, alog_ref, nw_ref = refs[k:k + 4]
    k += 4 + (1 if has_prev else 0)
    y_ref, sto_ref = refs[k:k + 2]
    csto_ref, buf = (None, None) if fused_conv else refs[k + 2:k + 4]
    ci = pl.program_id(1)

    @pl.when(ci == 0)
    def _():
        for s in range(nb):
            if not fused_conv:
                _conv_init(buf.at[s], cst_ref.at[s] if has_state else None, GDN_CONV, GDN_QKV)
            sto_ref[s] = st_ref[s] if has_state else jnp.zeros(sto_ref.shape[1:], F32)

    row = lax.broadcasted_iota(jnp.int32, (c, SLAB), 0) + ci * c
    valid = jnp.logical_and(row >= v0, row < v1)
    neg_a = -jnp.exp(alog_ref[...])
    w8 = GDN_HEADS * c
    lane8 = lax.broadcasted_iota(jnp.int32, (c, w8), 1)
    row8 = lax.broadcasted_iota(jnp.int32, (c, w8), 0)
    j8 = lane8 & (c - 1)
    causal8 = j8 <= row8
    strict8 = j8 < row8
    upto8 = row8 <= j8
    eye8 = jnp.where(j8 == row8, 1.0, 0.0)
    same_half8 = (j8 // INVERSE_BLOCK) == (row8 // INVERSE_BLOCK)
    lane_k =lax.broadcasted_iota(jnp.int32, (c, 2 * GDN_DK), 1) < GDN_DK

    def heads_on_lanes(arr):
        return jnp.concatenate(
            [jnp.broadcast_to(arr[:, SLAB_GA + h:SLAB_GA + h + 1], (c, c)) for h in range(GDN_HEADS)], axis=1)

    seqs, heads = range(nb), range(GDN_HEADS)
    col_ga = lambda arr, h: arr[:, SLAB_GA + h:SLAB_GA + h + 1]
    col_gb = lambda arr, h: arr[:, SLAB_GB + h:SLAB_GB + h + 1]
    if fused_conv:
        qkv = [qkv_ref[s] for s in seqs]
    else:
        qkv = [_silu_of_half(_conv_step(buf.at[s], qkv_ref.at[s], csto_ref.at[s], cw_ref, GDN_CONV, c, vend))
               for s in seqs]
    g = [jnp.where(valid, neg_a * _softplus(sm_ref[s] + bias_ref[...]), 0.0) for s in seqs]
    beta = [jnp.where(valid, _sigmoid(sm_ref[s]), 0.0) for s in seqs]
    gcum = [_cumsum_rows(g[s]) for s in seqs]
    e_g = [jnp.exp(gcum[s]) for s in seqs]
    e_last = [jnp.exp(gcum[s][c - 1:c, :]) for s in seqs]
    k_dec = [jnp.exp(gcum[s][c - 1:c, :] - gcum[s]) for s in seqs]
    row_cum = [jnp.sum(jnp.where(upto8, heads_on_lanes(g[s]), 0.0), axis=0, keepdims=True) for s in seqs]
    decay = [jnp.where(causal8, jnp.exp(heads_on_lanes(gcum[s]) - row_cum[s]), 0.0) for s in seqs]

    def unit(x):
        return x * lax.rsqrt(jnp.sum(x * x, axis=-1, keepdims=True) + EPS)

    qs = [[unit(qkv[s][:, h * GDN_DK:(h + 1) * GDN_DK]) * (GDN_DK ** -0.5) for h in heads] for s in seqs]
    ks = [[unit(qkv[s][:, GDN_QK + h * GDN_DK:GDN_QK + (h + 1) * GDN_DK]) for h in heads] for s in seqs]
    kbs = [[ks[s][h] * col_gb(beta[s], h) for h in heads] for s in seqs]

    def pair_scores(s, hp):
        k2 = jnp.concatenate([ks[s][2 * hp], ks[s][2 * hp + 1]], axis=1)
        rhs_t = jnp.concatenate([jnp.where(lane_k, k2, 0.0), jnp.where(lane_k, 0.0, k2)], axis=0)
        lhs = jnp.concatenate([jnp.concatenate([kbs[s][2 * hp], kbs[s][2 * hp + 1]], axis=1),
                               jnp.concatenate([qs[s][2 * hp], qs[s][2 * hp + 1]], axis=1)], axis=0)
        return _dot_nt(lhs, rhs_t)

    scores = [[pair_scores(s, hp) for hp in range(GDN_HEADS // 2)] for s in seqs]
    a_mat = [jnp.where(strict8, jnp.concatenate([r[0:c, :] for r in scores[s]], axis=1) * decay[s], 0.0) for s in seqs]
    attn = [jnp.concatenate([r[c:2 * c, :] for r in scores[s]], axis=1) * decay[s] for s in seqs]
    t_mat = _unit_lower_inverse(a_mat, eye8, same_half8, c)

    def solve(s, h):
        vh = qkv[s][:, 2 * GDN_QK + h * GDN_DV:2 * GDN_QK + (h + 1) * GDN_DV]
        rhs = jnp.concatenate([vh * col_gb(beta[s], h), kbs[s][h] * col_ga(e_g[s], h)], axis=1)
        return _dot(t_mat[s][:, h * c:(h + 1) * c], rhs)

    sol = [[solve(s, h) for h in heads] for s in seqs]

    def chunk_products(s, h):
        attn_h = attn[s][:, h * c:(h + 1) * c]
        k_dec_h = ks[s][h] * col_ga(k_dec[s], h)
        if c % LANES == 0:
            both = _dot(jnp.concatenate([attn_h, k_dec_h.T], axis=0), sol[s][h])
            return both[0:c], both[c:]
        return _dot(attn_h, sol[s][h]), _dot_tn(k_dec_h, sol[s][h])

    prods = [[chunk_products(s, h) for h in heads] for s in seqs]

    def through_state(s, h):
        au_aw, ku_kw = prods[s][h]
        q_eff = qs[s][h] * col_ga(e_g[s], h) - au_aw[:, GDN_DV:]
        return _dot(jnp.concatenate([q_eff, ku_kw[:, GDN_DV:]], axis=0), sto_ref[s, h])

    through = [[through_state(s, h) for h in heads] for s in seqs]
    for s in seqs:
        for h in heads:
            au_aw, ku_kw = prods[s][h]
            o = through[s][h][0:c] + au_aw[:, 0:GDN_DV]
            sto_ref[s, h] = sto_ref[s, h] * col_ga(e_last[s], h) - through[s][h][c:] + ku_kw[:, 0:GDN_DV]
            gt = gate_ref[s, :, h * GDN_DV:(h + 1) * GDN_DV]
            y_ref[s, :, h * GDN_DV:(h + 1) * GDN_DV] = (_rms(o, nw_ref[...]) * _silu(gt)).astype(y_ref.dtype)


def _gdn(conv3, plain3, slab3, st, cst, prev, prm, layer, depth, *, c, v0, v1, fused_conv):
    params = [prm["gdn_conv_w"], prm["slab_bias"], prm["slab_alog"], prm["gdn_norm_w"]]
    gd = GDN_HEADS * GDN_DV
    return _scan_call(_gdn_kernel, "gdn_scan", [(conv3, GDN_QKV, 1), (plain3, gd, GATE_BLOCK)], slab3,
                      st, cst, params, prev, gd, (GDN_HEADS, GDN_DK, GDN_DV), GDN_QKV, GDN_CONV,
                      [], layer, depth, c=c, v0=v0, v1=v1, fused_conv=fused_conv)


def _mix_kernel(x_ref, ys_ref, yg_ref, ms_ref, mg_ref, wso_ref, wgo_ref, wo_ref, o_ref):
    y_ssm = jnp.dot(ys_ref[...], wso_ref[...], preferred_element_type=F32)
    y_gdn = jnp.dot(yg_ref[...], wgo_ref[...], preferred_element_type=F32)
    merged = _sigmoid(ms_ref[...]) * y_ssm + _sigmoid(mg_ref[...]) * y_gdn
    o_ref[...] = x_ref[...] + _dot(merged, wo_ref[...])


def _mix(x, y_ssm, y_gdn, proj, prm, layer):
    t, d = x.shape
    tm = _pick_tile(t, 512, 16)
    lyr = lambda i: (layer, 0, 0)
    return pl.pallas_call(
        _mix_kernel,
        grid=(t // tm,),
        in_specs=[pl.BlockSpec((tm, d), lambda i: (i, 0)),
                  pl.BlockSpec((tm, D_INNER), lambda i: (i, 0)),
                  pl.BlockSpec((tm, GDN_HEADS * GDN_DV), lambda i: (i, 0)),
                  pl.BlockSpec((tm, d), lambda i: (i, GATE_BLOCK + 1)),
                  pl.BlockSpec((tm, d), lambda i: (i, GATE_BLOCK + 2)),
                  pl.BlockSpec((None, D_INNER, d), lyr),
                  pl.BlockSpec((None, GDN_HEADS * GDN_DV, d), lyr),
                  pl.BlockSpec((None, d, d), lyr)],
        out_specs=pl.BlockSpec((tm, d), lambda i: (i, 0)),
        out_shape=jax.ShapeDtypeStruct((t, d), F32),
        compiler_params=pltpu.CompilerParams(dimension_semantics=("parallel",), vmem_limit_bytes=VMEM_LIMIT),
        name="mix_out",
    )(x, y_ssm, y_gdn, proj, proj, prm["w_ssm_out"], prm["w_gdn_out"], prm["w_o"])


def _ffn_kernel(*refs, nb, rows, vend, has_state):
    if has_state:
        (x_ref, cst_ref, nw_ref, wup_ref, cw_ref, cb_ref, wdn_ref, nnw_ref, xo_ref, hn_ref, csto_ref, buf) = refs
    else:
        (x_ref, nw_ref, wup_ref, cw_ref, cb_ref, wdn_ref, nnw_ref, xo_ref, hn_ref, csto_ref, buf) = refs
    ti = pl.program_id(1)
    last_tile = pl.num_programs(1) - 1
    hist = FFN_CONV - 1

    @pl.when(ti == 0)
    def _():
        buf[:, 0:SUBLANES, :] = jnp.zeros((nb, SUBLANES, D_FF), F32)
        if has_state:
            buf[:, SUBLANES - hist:SUBLANES, :] = cst_ref[...]

    x = x_ref[...]
    gu = _dot(_rms(x, nw_ref[...]), wup_ref[...])
    buf[:, SUBLANES:SUBLANES + rows, :] = gu[:, 0:D_FF].reshape(nb, rows, D_FF)
    conv = _causal_conv(buf.at[0] if nb == 1 else buf, cw_ref, FFN_CONV, rows)

    @pl.when(ti == last_tile)
    def _():
        csto_ref[...] = buf[:, SUBLANES + vend - hist:SUBLANES + vend, :]

    buf[:, 0:SUBLANES, :] = buf[:, rows:rows + SUBLANES, :]
    act = _silu_of_half(conv + cb_ref[...]).reshape(nb * rows, D_FF) * gu[:, D_FF:]
    out = x + _dot(act, wdn_ref[...])
    xo_ref[...] = out
    hn_ref[...] = _rms(out, nnw_ref[...]).astype(hn_ref.dtype)


def _ffn(x, cst, prm, layer, next_norm_w, next_layer, hn_dtype, *, b, lp, v1):
    t, d = x.shape
    has_state = cst is not None
    if lp <= 64:
        rows, nb = lp, _pick_tile(b, max(1, 256 // lp), 1)
    else:
        rows, nb = _pick_tile(lp, FFN_ROWS, 16), 1
    nt = lp // rows
    vend = v1 - (nt - 1) * rows
    assert FFN_CONV - 1 <= vend <= rows
    lyr = lambda bi, i: (layer, 0, 0)
    flat = lambda bi, i: (bi * nt + i, 0)
    per_b = lambda bi, i: (bi, 0, 0)
    in_specs = [pl.BlockSpec((nb * rows, d), flat)]
    args = [x]
    if has_state:
        in_specs.append(pl.BlockSpec((None, nb, FFN_CONV - 1, D_FF), lambda bi, i: (layer, bi, 0, 0)))
        args.append(cst)
    in_specs += [pl.BlockSpec((None, 1, d), lyr),
                 pl.BlockSpec((None, d, 2 * D_FF), lyr, pipeline_mode=pl.Buffered(1)),
                 pl.BlockSpec((None, FFN_CONV, D_FF), lyr),
                 pl.BlockSpec((None, 1, D_FF), lyr),
                 pl.BlockSpec((None, D_FF, d), lyr, pipeline_mode=pl.Buffered(1)),
                 pl.BlockSpec((None, 1, d), lambda bi, i: (next_layer, 0, 0))]
    args += [prm["norm_ffn_w"], prm["w_up"], prm["ffn_conv_w"], prm["ffn_conv_b"], prm["w_down"], next_norm_w]
    return pl.pallas_call(
        functools.partial(_ffn_kernel, nb=nb, rows=rows, vend=vend, has_state=has_state),
        grid=(b // nb, nt),
        in_specs=in_specs,
        out_specs=[pl.BlockSpec((nb * rows, d), flat),
                   pl.BlockSpec((nb * rows, d), flat),
                   pl.BlockSpec((nb, FFN_CONV - 1, D_FF), per_b)],
        out_shape=[jax.ShapeDtypeStruct((t, d), F32),
                   jax.ShapeDtypeStruct((t, d), hn_dtype),
                   jax.ShapeDtypeStruct((b, FFN_CONV - 1, D_FF), F32)],
        scratch_shapes=[pltpu.VMEM((nb, SUBLANES + rows, D_FF), F32)],
        compiler_params=pltpu.CompilerParams(dimension_semantics=("parallel", "arbitrary"),
                                             vmem_limit_bytes=VMEM_LIMIT),
        name="conv_ffn",
    )(*args)


def _prepare_params(norm_mix_w, w_in, ssm_conv_w, ssm_conv_b, ssm_dt_bias, ssm_a_log, ssm_d, ssm_norm_w,
                    gdn_conv_w, gdn_dt_bias, gdn_a_log, gdn_norm_w, w_ssm_out, w_gdn_out, w_o,
                    norm_ffn_w, w_up, ffn_conv_w, ffn_conv_b, w_down, norm_f_w):
    depth = w_in.shape[0]
    offs = [0]
    for s in IN_SIZES:
        offs.append(offs[-1] + s)
    col = lambda k: w_in[:, :, offs[k]:offs[k + 1]]
    w_conv = jnp.concatenate([col(1), col(3)], axis=-1).astype(BF16)
    w_plain = jnp.concatenate([col(0), col(6), col(7), col(8)], axis=-1).astype(BF16)
    dt_cols = col(2)
    pad = jnp.zeros((depth, w_in.shape[1], SLAB - SSM_HEADS - 2 * GDN_HEADS), w_in.dtype)
    w_slab = jnp.concatenate([dt_cols[..., 0::2], dt_cols[..., 1::2], col(4), col(5), pad], axis=-1).astype(BF16)

    def slab(ssm_vec, gdn_vec):
        z = jnp.zeros((depth, SLAB - SSM_HEADS - GDN_HEADS), F32)
        return jnp.concatenate([ssm_vec[:, 0::2], ssm_vec[:, 1::2], gdn_vec, z], axis=-1)[:, None, :]

    return {
        "norm_mix_w": norm_mix_w[:, None, :],
        "w_conv": w_conv,
        "w_plain": w_plain,
        "w_slab": w_slab,
        "conv_w": 0.5 * jnp.concatenate([ssm_conv_w, gdn_conv_w], axis=-1),
        "conv_b": 0.5 * jnp.concatenate([ssm_conv_b, jnp.zeros((depth, GDN_QKV), F32)], axis=-1)[:, None, :],
        "ssm_conv_w": 0.5 * ssm_conv_w,
        "ssm_conv_b": 0.5 * ssm_conv_b[:, None, :],
        "slab_bias": slab(ssm_dt_bias, gdn_dt_bias),
        "slab_alog": slab(ssm_a_log, gdn_a_log),
        "ssm_d_cols": jnp.repeat(ssm_d, SSM_HEAD_DIM, axis=-1)[:, None, :],
        "ssm_norm_w": ssm_norm_w[:, None, :],
        "gdn_conv_w": 0.5 * gdn_conv_w,
        "gdn_norm_w": gdn_norm_w[:, None, :],
        "w_ssm_out": w_ssm_out.astype(BF16),
        "w_gdn_out": w_gdn_out.astype(BF16),
        "w_o": w_o.astype(BF16),
        "norm_ffn_w": norm_ffn_w[:, None, :],
        "w_up": w_up.astype(BF16),
        "ffn_conv_w": 0.5 * ffn_conv_w,
        "ffn_conv_b": 0.5 * ffn_conv_b[:, None, :],
        "w_down": w_down.astype(BF16),
        "norm_f_w": norm_f_w[None, None, :],
    }


def _run_trunk(x3, states, prm, *, c_ssd, c_gdn, v0, v1):
    b, lp, d = x3.shape
    depth = prm["w_conv"].shape[0]
    x = x3.reshape(b * lp, d)
    h = _norm(x, prm["norm_mix_w"], 0, BF16)
    fused_conv = states is None
    if fused_conv:
        assert v1 == lp and SSM_CONV == GDN_CONV
        st_ssm = cst_ssm = st_gdn = cst_gdn = cst_ffn = None
    else:
        st_ssm = states[0].reshape(depth, b, D_INNER, SSM_STATE)
        cst_ssm, st_gdn, cst_gdn, cst_ffn = states[1:]
    o_ssm = o_gdn = None
    conv_outs = [[] for _ in range(3)]
    for l in range(depth):
        if fused_conv:
            conv, tail = _inproj_conv(h, prm["w_conv"], prm["conv_w"], prm["conv_b"], l, b=b, lp=lp, width=SSM_CONV)
            tail = tail[:, SUBLANES - (SSM_CONV - 1):, :]
        else:
            conv = _inproj(h, prm["w_conv"], None, l)
        plain, slab = _inproj(h, prm["w_plain"], prm["w_slab"], l)
        conv3 = conv.reshape(b, lp, CONV_COLS)
        plain3 = plain.reshape(b, lp, PLAIN_COLS)
        slab3 = slab.reshape(b, lp, SLAB)
        res = _ssd(conv3, plain3, slab3, st_ssm, cst_ssm, o_ssm, prm, l, depth, c=c_ssd, v0=v0, v1=v1,
                   fused_conv=fused_conv)
        y_ssm, o_ssm = res[0], res[1]
        o_ssm_conv = tail[:, :, :SSM_XBC] if fused_conv else res[2]
        res = _gdn(conv3, plain3, slab3, st_gdn, cst_gdn, o_gdn, prm, l, depth, c=c_gdn, v0=v0, v1=v1,
                   fused_conv=fused_conv)
        y_gdn, o_gdn = res[0], res[1]
        o_gdn_conv = tail[:, :, SSM_XBC:] if fused_conv else res[2]
        x = _mix(x, y_ssm.reshape(b * lp, D_INNER), y_gdn.reshape(b * lp, GDN_HEADS * GDN_DV), plain, prm, l)
        if l + 1 < depth:
            x, h, o_ffn_conv = _ffn(x, cst_ffn, prm, l, prm["norm_mix_w"], l + 1, BF16, b=b, lp=lp, v1=v1)
        else:
            x, h, o_ffn_conv = _ffn(x, cst_ffn, prm, l, prm["norm_f_w"], 0, F32, b=b, lp=lp, v1=v1)
        conv_outs[0].append(o_ssm_conv)
        conv_outs[1].append(o_gdn_conv)
        conv_outs[2].append(o_ffn_conv)
    return (h.reshape(b, lp, d), o_ssm.reshape(depth, b, SSM_HEADS, SSM_HEAD_DIM, SSM_STATE),
            jnp.stack(conv_outs[0]), o_gdn, jnp.stack(conv_outs[1]), jnp.stack(conv_outs[2]))


def kernel(x_prompt, x_sample, state_ssm, state_ssm_conv, state_gdn, state_gdn_conv, state_ffn_conv, meta_tokens, norm_mix_w, w_in, ssm_conv_w, ssm_conv_b, ssm_dt_bias, ssm_a_log, ssm_d, ssm_norm_w, gdn_conv_w, gdn_dt_bias, gdn_a_log, gdn_norm_w, w_ssm_out, w_gdn_out, w_o, norm_ffn_w, w_up, ffn_conv_w, ffn_conv_b, w_down, norm_f_w):
    prm = _prepare_params(norm_mix_w, w_in, ssm_conv_w, ssm_conv_b, ssm_dt_bias, ssm_a_log, ssm_d, ssm_norm_w,
                          gdn_conv_w, gdn_dt_bias, gdn_a_log, gdn_norm_w, w_ssm_out, w_gdn_out, w_o,
                          norm_ffn_w, w_up, ffn_conv_w, ffn_conv_b, w_down, norm_f_w)
    bp, sp, d = x_prompt.shape
    bs, ls, _ = x_sample.shape
    c_p = max(SSD_CHUNK, GDN_CHUNK)
    assert c_p % SSD_CHUNK == 0 and c_p % GDN_CHUNK == 0
    lp = -(-(N_META + sp) // c_p) * c_p
    pad = lp - N_META - sp
    xp = jnp.concatenate([jnp.zeros((bp, pad, d), x_prompt.dtype),
                          jnp.broadcast_to(meta_tokens.astype(x_prompt.dtype), (bp, N_META, d)), x_prompt], axis=1)
    p_out = _run_trunk(xp, None, prm, c_ssd=SSD_CHUNK, c_gdn=GDN_CHUNK, v0=pad, v1=lp)
    y_prompt = p_out[0][:, pad + N_META:]
    c_s = -(-ls // SUBLANES) * SUBLANES
    xs = jnp.concatenate([x_sample, jnp.zeros((bs, c_s - ls, d), x_sample.dtype)], axis=1)
    s_out = _run_trunk(xs, (state_ssm, state_ssm_conv, state_gdn, state_gdn_conv, state_ffn_conv), prm,
                       c_ssd=c_s, c_gdn=c_s, v0=0, v1=ls)
    y_sample = s_out[0][:, :ls]
    return (y_prompt,) + (y_sample,) + p_out[1:] + s_out[1:]
```

```python
import functools

import jax
import jax.numpy as jnp
from jax import lax
from jax.experimental import pallas as pl
from jax.experimental.pallas import tpu as pltpu

F32 = jnp.float32
BF16 = jnp.bfloat16
HIGHEST = lax.Precision.HIGHEST

EPS = 1e-6
N_META = 16
D_MODEL = 1024
SSM_HEADS = 32
SSM_HEAD_DIM = 64
SSM_GROUPS = 4
SSM_STATE = 128
SSM_CONV = 4
D_INNER = SSM_HEADS * SSM_HEAD_DIM
SSM_BC = SSM_GROUPS * SSM_STATE
SSM_XBC = D_INNER + 2 * SSM_BC
SSM_PAIRS = SSM_HEADS // 2
PAIRS_PER_GROUP = SSM_PAIRS // SSM_GROUPS
GROUP_COLS = D_INNER // SSM_GROUPS
GDN_HEADS = 8
GDN_DK = 128
GDN_DV = 128
GDN_CONV = 4
GDN_QK = GDN_HEADS * GDN_DK
GDN_QKV = 2 * GDN_QK + GDN_HEADS * GDN_DV
D_FF = 2816
FFN_CONV = 3
IN_SIZES = (D_INNER, SSM_XBC, SSM_HEADS, GDN_QKV, GDN_HEADS, GDN_HEADS, GDN_HEADS * GDN_DV, D_MODEL, D_MODEL)
PROJ_COLS = SSM_XBC + GDN_QKV + D_INNER + 3 * D_MODEL
Z_BLOCK = (SSM_XBC + GDN_QKV) // D_INNER
GATE_BLOCK = (SSM_XBC + GDN_QKV + D_INNER) // D_MODEL
SLAB = 128
SLAB_GA = SSM_HEADS
SLAB_GB = SSM_HEADS + GDN_HEADS

LANES = 128
SUBLANES = 8
MXU_DIM = 256
SSD_CHUNK = 64
GDN_CHUNK = 128
INVERSE_BLOCK = 64
VMEM_LIMIT = 56 * 1024 * 1024
SCAN_SEQS_LONG = 4
SCAN_SEQS_SHORT = 8
INPROJ_COLS = 2816
FFN_ROWS = 576


def _pick_tile(total, cap, mult):
    best = None
    for t in range(mult, min(total, cap) + 1, mult):
        if total % t == 0:
            best = t
    assert best is not None, (total, cap, mult)
    return best


def _sigmoid(x):
    return 0.5 * jnp.tanh(0.5 * x) + 0.5


def _silu_of_half(half):
    return half * jnp.tanh(half) + half


def _silu(x):
    return _silu_of_half(0.5 * x)


def _softplus(x):
    return jnp.maximum(x, 0.0) + jnp.log1p(jnp.exp(-jnp.abs(x)))


def _rms(x, w):
    return x * lax.rsqrt(jnp.mean(x * x, axis=-1, keepdims=True) + EPS) * w


def _dot(a, b):
    return jnp.dot(a.astype(BF16), b.astype(BF16), preferred_element_type=F32)


def _dot_nt(a, b):
    return lax.dot_general(a.astype(BF16), b.astype(BF16), (((1,), (1,)), ((), ())), preferred_element_type=F32)


def _dot_tn(a, b):
    return lax.dot_general(a.astype(BF16), b.astype(BF16), (((0,), (0,)), ((), ())), preferred_element_type=F32)


def _cumsum_rows(x):
    c = x.shape[0]
    tril = (lax.broadcasted_iota(jnp.int32, (c, c), 0) >= lax.broadcasted_iota(jnp.int32, (c, c), 1)).astype(F32)
    return jnp.dot(tril, x, precision=HIGHEST, preferred_element_type=F32)


def _causal_conv(buf, w_ref, width, rows):
    acc = None
    if len(buf.shape) == 2:
        cols = buf.shape[1]
        tiles = buf[0:SUBLANES + rows, :].reshape(1 + rows // SUBLANES, SUBLANES, cols)
        sublane = lax.broadcasted_iota(jnp.int32, (1, SUBLANES, cols), 1)
        shifted = [tiles[1:]]
        rot = tiles
        for shift in range(1, width):
            rot = pltpu.roll(rot, 1, axis=1)
            shifted.append(jnp.where(sublane < shift, rot[:-1], rot[1:]))
        for j in range(width):
            term = shifted[width - 1 - j] * w_ref[j:j + 1, :]
            acc = term if acc is None else acc + term
        return acc.reshape(rows, cols)
    for j in range(width):
        off = SUBLANES - (width - 1) + j
        term = buf[:, off:off + rows, :] * w_ref[j:j + 1, :]
        acc = term if acc is None else acc + term
    return acc


def _norm_kernel(x_ref, w_ref, o_ref):
    o_ref[...] = _rms(x_ref[...], w_ref[...]).astype(o_ref.dtype)


def _norm(x, w, layer, out_dtype):
    t, d = x.shape
    tm = _pick_tile(t, 1024, 16)
    return pl.pallas_call(
        _norm_kernel,
        grid=(t // tm,),
        in_specs=[pl.BlockSpec((tm, d), lambda i: (i, 0)),
                  pl.BlockSpec((None, 1, d), lambda i: (layer, 0, 0))],
        out_specs=pl.BlockSpec((tm, d), lambda i: (i, 0)),
        out_shape=jax.ShapeDtypeStruct((t, d), out_dtype),
        compiler_params=pltpu.CompilerParams(dimension_semantics=("parallel",)),
        name="rmsnorm",
    )(x, w)


def _inproj_kernel(h_ref, w_ref, ws_ref, o_ref, os_ref):
    h = h_ref[...]
    o_ref[...] = jnp.dot(h, w_ref[...], preferred_element_type=F32)

    @pl.when(pl.program_id(1) == 0)
    def _():
        os_ref[...] = jnp.dot(h, ws_ref[...], preferred_element_type=F32)


def _inproj(h, w_main, w_slab, layer):
    t, d = h.shape
    ncols = w_main.shape[-1]
    tm = _pick_tile(t, 1536, 16)
    tn = _pick_tile(ncols, INPROJ_COLS, LANES)
    return pl.pallas_call(
        _inproj_kernel,
        grid=(t // tm, ncols // tn),
        in_specs=[pl.BlockSpec((tm, d), lambda i, j: (i, 0)),
                  pl.BlockSpec((None, d, tn), lambda i, j: (layer, 0, j)),
                  pl.BlockSpec((None, d, SLAB), lambda i, j: (layer, 0, 0))],
        out_specs=[pl.BlockSpec((tm, tn), lambda i, j: (i, j)),
                   pl.BlockSpec((tm, SLAB), lambda i, j: (i, 0))],
        out_shape=[jax.ShapeDtypeStruct((t, ncols), F32), jax.ShapeDtypeStruct((t, SLAB), F32)],
        compiler_params=pltpu.CompilerParams(dimension_semantics=("parallel", "arbitrary"),
                                             vmem_limit_bytes=VMEM_LIMIT),
        name="inproj",
    )(h, w_main, w_slab)


def _scan_call(body, name, proj3, slab3, col_blocks, state_in, conv_in, params, prev_state, out_cols,
               state_block, conv_cols, conv_width, scratch, layer, depth, *, c, v0, v1):
    b, lp, _ = proj3.shape
    nc = lp // c
    nb = _pick_tile(b, SCAN_SEQS_LONG if nc > 1 else SCAN_SEQS_SHORT, 1)
    vend = v1 - (nc - 1) * c
    hist = conv_width - 1
    assert hist <= vend <= c
    has_state = state_in is not None
    has_prev = prev_state is not None
    zeros = (0,) * len(state_block)
    lyr = lambda bi, i: (layer, 0, 0)
    in_specs = [pl.BlockSpec((nb, c, width), functools.partial(lambda blk, bi, i: (bi, i, blk), blk))
                for width, blk in col_blocks]
    in_specs.append(pl.BlockSpec((nb, c, SLAB), lambda bi, i: (bi, i, 0)))
    args = [proj3] * len(col_blocks) + [slab3]
    if has_state:
        in_specs += [pl.BlockSpec((None, nb) + state_block, lambda bi, i: (layer, bi) + zeros),
                     pl.BlockSpec((None, nb, hist, conv_cols), lambda bi, i: (layer, bi, 0, 0))]
        args += [state_in, conv_in]
    for p in params:
        in_specs.append(pl.BlockSpec((None,) + p.shape[1:], lyr))
        args.append(p)
    aliases = {}
    if has_prev:
        aliases[len(args)] = 1
        in_specs.append(pl.BlockSpec(memory_space=pl.ANY))
        args.append(prev_state)
    return pl.pallas_call(
        functools.partial(body, nb=nb, c=c, v0=v0, v1=v1, vend=vend, has_state=has_state, has_prev=has_prev),
        grid=(b // nb, nc),
        in_specs=in_specs,
        out_specs=[pl.BlockSpec((nb, c, out_cols), lambda bi, i: (bi, i, 0)),
                   pl.BlockSpec((None, nb) + state_block, lambda bi, i: (layer, bi) + zeros),
                   pl.BlockSpec((nb, hist, conv_cols), lambda bi, i: (bi, 0, 0))],
        out_shape=[jax.ShapeDtypeStruct((b, lp, out_cols), BF16),
                   jax.ShapeDtypeStruct((depth, b) + state_block, F32),
                   jax.ShapeDtypeStruct((b, hist, conv_cols), F32)],
        scratch_shapes=[pltpu.VMEM((nb, SUBLANES + c, conv_cols), F32)] + [s(nb) for s in scratch],
        input_output_aliases=aliases,
        compiler_params=pltpu.CompilerParams(dimension_semantics=("parallel", "arbitrary"),
                                             vmem_limit_bytes=VMEM_LIMIT),
        name=name,
    )(*args)


def _conv_step(buf, x_ref, cst_out, w_ref, width, c, vend):
    buf[SUBLANES:SUBLANES + c, :] = x_ref[...]
    out = _causal_conv(buf, w_ref, width, c)
    cst_out[...] = buf[SUBLANES + vend - (width - 1):SUBLANES + vend, :]
    buf[0:SUBLANES, :] = buf[c:c + SUBLANES, :]
    return out


def _conv_init(buf, cst_ref, width, cols):
    buf[0:SUBLANES, :] = jnp.zeros((SUBLANES, cols), F32)
    if cst_ref is not None:
        buf[SUBLANES - (width - 1):SUBLANES, :] = cst_ref[...]


def _pair_expand(arr, p, width, half):
    r = arr.shape[0]
    even = jnp.broadcast_to(arr[:, p:p + 1], (r, width))
    odd = jnp.broadcast_to(arr[:, SSM_PAIRS + p:SSM_PAIRS + p + 1], (r, width))
    return jnp.where(half, even, odd)


def _ssd_kernel(*refs, nb, c, v0, v1, vend, has_state, has_prev):
    refs = list(refs)
    xbc_ref, z_ref, sm_ref = refs[:3]
    k = 3
    st_ref = cst_ref = None
    if has_state:
        st_ref, cst_ref = refs[k:k + 2]
        k += 2
    cw_ref, cb_ref, bias_ref, alog_ref, dexp_ref, nw_ref = refs[k:k + 6]
    k += 6 + (1 if has_prev else 0)
    y_ref, sto_ref, csto_ref, buf = refs[k:k + 4]
    ht = None if has_state else refs[-1]
    ci = pl.program_id(1)

    @pl.when(ci == 0)
    def _():
        for s in range(nb):
            _conv_init(buf.at[s], cst_ref.at[s] if has_state else None, SSM_CONV, SSM_XBC)
            if has_state:
                sto_ref[s] = st_ref[s]
            else:
                ht[s] = jnp.zeros(ht.shape[1:], F32)

    row = lax.broadcasted_iota(jnp.int32, (c, SLAB), 0) + ci * c
    valid = jnp.logical_and(row >= v0, row < v1)
    neg_a = -jnp.exp(alog_ref[...])
    w2 = 2 * c
    lane2 = lax.broadcasted_iota(jnp.int32, (c, w2), 1)
    row2 = lax.broadcasted_iota(jnp.int32, (c, w2), 0)
    half2 = lane2 < c
    j2 = jnp.where(half2, lane2, lane2 - c)
    causal2 = j2 <= row2
    upto2 = row2 <= j2
    eye2 = j2 == row2
    half_p = lax.broadcasted_iota(jnp.int32, (c, LANES), 1) < SSM_HEAD_DIM
    even_rows = lax.broadcasted_iota(jnp.int32, (LANES, SSM_STATE), 0) < SSM_HEAD_DIM

    seqs, groups, pairs = range(nb), range(SSM_GROUPS), range(SSM_PAIRS)
    gcols = [slice(g * GROUP_COLS, (g + 1) * GROUP_COLS) for g in groups]
    pcols = [slice(p * LANES, (p + 1) * LANES) for p in pairs]
    xbc = [_silu_of_half(_conv_step(buf.at[s], xbc_ref.at[s], csto_ref.at[s], cw_ref, SSM_CONV, c, vend)
                         + cb_ref[...]) for s in seqs]
    dt = [jnp.where(valid, _softplus(sm_ref[s] + bias_ref[...]), 0.0) for s in seqs]
    acum = [_cumsum_rows(dt[s] * neg_a) for s in seqs]
    b_g = [[xbc[s][:, D_INNER + g * SSM_STATE:D_INNER + (g + 1) * SSM_STATE] for g in groups] for s in seqs]
    c_g = [[xbc[s][:, D_INNER + SSM_BC + g * SSM_STATE:D_INNER + SSM_BC + (g + 1) * SSM_STATE] for g in groups]
           for s in seqs]
    cb = [[_dot_nt(c_g[s][g], b_g[s][g]) for g in groups] for s in seqs]
    y_off = [[_dot_nt(c_g[s][g], sto_ref[s, gcols[g], :]) if has_state else _dot(c_g[s][g], ht[s, :, gcols[g]])
              for g in groups] for s in seqs]

    def intra(s, p):
        g = p // PAIRS_PER_GROUP
        dx = _pair_expand(dt[s], p, w2, half2)
        ax = dx * _pair_expand(neg_a, p, w2, half2[0:1])
        row_cum = jnp.sum(jnp.where(upto2, ax, 0.0), axis=0, keepdims=True)
        dt_row = jnp.sum(jnp.where(eye2, dx, 0.0), axis=0, keepdims=True)
        col_cum = _pair_expand(acum[s], p, w2, half2)
        decay = jnp.where(causal2, jnp.exp(col_cum - row_cum), 0.0)
        wgt = decay * jnp.concatenate([cb[s][g], cb[s][g]], axis=1) * dt_row
        xp = xbc[s][:, pcols[p]]
        rhs = jnp.concatenate([jnp.where(half_p, xp, 0.0), jnp.where(half_p, 0.0, xp)], axis=0)
        if w2 == LANES:
            acum_p, dt_p = col_cum, dx
        else:
            acum_p, dt_p = _pair_expand(acum[s], p, LANES, half_p), _pair_expand(dt[s], p, LANES, half_p)
        return _dot(wgt, rhs), acum_p, dt_p

    parts = [[intra(s, p) for p in pairs] for s in seqs]

    def pair_outputs(s, p):
        y_in, acum_p, dt_p = parts[s][p]
        g, q = divmod(p, PAIRS_PER_GROUP)
        xp = xbc[s][:, pcols[p]]
        last_p = acum_p[c - 1:c, :]
        y = y_in + y_off[s][g][:, q * LANES:(q + 1) * LANES] * jnp.exp(acum_p) + dexp_ref[:, pcols[p]] * xp
        return y, xp * (jnp.exp(last_p - acum_p) * dt_p), jnp.exp(last_p)

    outs = [[pair_outputs(s, p) for p in pairs] for s in seqs]
    in_group = lambda s, g, k: [outs[s][p][k] for p in range(g * PAIRS_PER_GROUP, (g + 1) * PAIRS_PER_GROUP)]
    xw = [[jnp.concatenate(in_group(s, g, 1), axis=1) for g in groups] for s in seqs]
    upd = [[_dot_tn(xw[s][g], b_g[s][g]) if has_state else _dot_tn(b_g[s][g], xw[s][g]) for g in groups]
           for s in seqs]
    for s in seqs:
        e_last = jnp.exp(acum[s][c - 1:c, :])
        for g in groups:
            if has_state:
                scale = jnp.concatenate(
                    [jnp.where(even_rows,
                               jnp.broadcast_to(e_last[:, p:p + 1], (LANES, SSM_STATE)),
                               jnp.broadcast_to(e_last[:, SSM_PAIRS + p:SSM_PAIRS + p + 1], (LANES, SSM_STATE)))
                     for p in range(g * PAIRS_PER_GROUP, (g + 1) * PAIRS_PER_GROUP)], axis=0)
                sto_ref[s, gcols[g], :] = sto_ref[s, gcols[g], :] * scale + upd[s][g]
            else:
                ht[s, :, gcols[g]] = ht[s, :, gcols[g]] * jnp.concatenate(in_group(s, g, 2), axis=1) + upd[s][g]
            yz = jnp.concatenate(in_group(s, g, 0), axis=1) * _silu(z_ref[s, :, gcols[g]])
            y_ref[s, :, gcols[g]] = _rms(yz, nw_ref[:, gcols[g]]).astype(y_ref.dtype)

    if not has_state:
        @pl.when(ci == pl.num_programs(1) - 1)
        def _():
            for s in range(nb):
                for p in range(SSM_PAIRS):
                    sto_ref[s, p * LANES:(p + 1) * LANES, :] = ht[s, :, p * LANES:(p + 1) * LANES].T


def _ssd(proj3, slab3, st, cst, prev, prm, layer, depth, *, c, v0, v1):
    params = [prm["ssm_conv_w"], prm["ssm_conv_b"], prm["slab_bias"], prm["slab_alog"], prm["ssm_d_cols"],
              prm["ssm_norm_w"]]
    return _scan_call(_ssd_kernel, "ssd_scan", proj3, slab3, [(SSM_XBC, 0), (D_INNER, Z_BLOCK)], st, cst, params,
                      prev, D_INNER, (D_INNER, SSM_STATE), SSM_XBC, SSM_CONV,
                      [] if st is not None else [lambda nb: pltpu.VMEM((nb, SSM_STATE, D_INNER), F32)],
                      layer, depth, c=c, v0=v0, v1=v1)


def _head_products(x, y, c):
    per_op = min(GDN_HEADS, MXU_DIM // c)
    width = per_op * c
    lane = lax.broadcasted_iota(jnp.int32, (c, width), 1)
    outs = []
    head = lane // c if per_op > 1 else None
    for s in range(GDN_HEADS // per_op):
        ys = y[:, s * width:(s + 1) * width]
        diag = ys if per_op == 1 else jnp.concatenate(
            [jnp.where(head == r, ys, 0.0) for r in range(per_op)], axis=0)
        outs.append(_dot(x[:, s * width:(s + 1) * width], diag))
    return outs[0] if len(outs) == 1 else jnp.concatenate(outs, axis=1)


def _unit_lower_inverse(mats, eye, same_half, c):
    blk = min(c, INVERSE_BLOCK)
    assert c in (blk, 2 * blk)
    ns = [-(a if blk == c else jnp.where(same_half, a, 0.0)) for a in mats]
    ts = [eye + n for n in ns]
    ms = [_head_products(n, n, c) for n in ns]
    covered = 2
    while 2 * covered < blk:
        boths = [_head_products(jnp.concatenate([t, m], axis=0), m, c) for t, m in zip(ts, ms)]
        ts = [t + both[0:c] for t, both in zip(ts, boths)]
        ms = [both[c:2 * c] for both in boths]
        covered *= 2
    ds = [t + _head_products(t, m, c) for t, m in zip(ts, ms)]
    if blk == c:
        return ds
    firsts = [_head_products(d, jnp.where(same_half, 0.0, a), c) for d, a in zip(ds, mats)]
    return [d - _head_products(f, d, c) for d, f in zip(ds, firsts)]


def _gdn_kernel(*refs, nb, c, v0, v1, vend, has_state, has_prev):
    refs = list(refs)
    qkv_ref, gate_ref, sm_ref = refs[:3]
    k = 3
    st_ref = cst_ref = None
    if has_state:
        st_ref, cst_ref = refs[k:k + 2]
        k += 2
    cw_ref, bias_ref, alog_ref, nw_ref = refs[k:k + 4]
    k += 4 + (1 if has_prev else 0)
    y_ref, sto_ref, csto_ref, buf = refs[k:]
    ci = pl.program_id(1)

    @pl.when(ci == 0)
    def _():
        for s in range(nb):
            _conv_init(buf.at[s], cst_ref.at[s] if has_state else None, GDN_CONV, GDN_QKV)
            sto_ref[s] = st_ref[s] if has_state else jnp.zeros(sto_ref.shape[1:], F32)

    row = lax.broadcasted_iota(jnp.int32, (c, SLAB), 0) + ci * c
    valid = jnp.logical_and(row >= v0, row < v1)
    neg_a = -jnp.exp(alog_ref[...])
    w8 = GDN_HEADS * c
    lane8 = lax.broadcasted_iota(jnp.int32, (c, w8), 1)
    row8 = lax.broadcasted_iota(jnp.int32, (c, w8), 0)
    j8 = lane8 & (c - 1)
    causal8 = j8 <= row8
    strict8 = j8 < row8
    upto8 = row8 <= j8
    eye8 = jnp.where(j8 == row8, 1.0, 0.0)
    same_half8 = (j8 // INVERSE_BLOCK) == (row8 // INVERSE_BLOCK)
    lane_k =lax.broadcasted_iota(jnp.int32, (c, 2 * GDN_DK), 1) < GDN_DK

    def heads_on_lanes(arr):
        return jnp.concatenate(
            [jnp.broadcast_to(arr[:, SLAB_GA + h:SLAB_GA + h + 1], (c, c)) for h in range(GDN_HEADS)], axis=1)

    seqs, heads = range(nb), range(GDN_HEADS)
    col_ga = lambda arr, h: arr[:, SLAB_GA + h:SLAB_GA + h + 1]
    col_gb = lambda arr, h: arr[:, SLAB_GB + h:SLAB_GB + h + 1]
    qkv = [_silu_of_half(_conv_step(buf.at[s], qkv_ref.at[s], csto_ref.at[s], cw_ref, GDN_CONV, c, vend))
           for s in seqs]
    g = [jnp.where(valid, neg_a * _softplus(sm_ref[s] + bias_ref[...]), 0.0) for s in seqs]
    beta = [jnp.where(valid, _sigmoid(sm_ref[s]), 0.0) for s in seqs]
    gcum = [_cumsum_rows(g[s]) for s in seqs]
    e_g = [jnp.exp(gcum[s]) for s in seqs]
    e_last = [jnp.exp(gcum[s][c - 1:c, :]) for s in seqs]
    k_dec = [jnp.exp(gcum[s][c - 1:c, :] - gcum[s]) for s in seqs]
    row_cum = [jnp.sum(jnp.where(upto8, heads_on_lanes(g[s]), 0.0), axis=0, keepdims=True) for s in seqs]
    decay = [jnp.where(causal8, jnp.exp(heads_on_lanes(gcum[s]) - row_cum[s]), 0.0) for s in seqs]

    def unit(x):
        return x * lax.rsqrt(jnp.sum(x * x, axis=-1, keepdims=True) + EPS)

    qs = [[unit(qkv[s][:, h * GDN_DK:(h + 1) * GDN_DK]) * (GDN_DK ** -0.5) for h in heads] for s in seqs]
    ks = [[unit(qkv[s][:, GDN_QK + h * GDN_DK:GDN_QK + (h + 1) * GDN_DK]) for h in heads] for s in seqs]
    kbs = [[ks[s][h] * col_gb(beta[s], h) for h in heads] for s in seqs]

    def pair_scores(s, hp):
        k2 = jnp.concatenate([ks[s][2 * hp], ks[s][2 * hp + 1]], axis=1)
        rhs_t = jnp.concatenate([jnp.where(lane_k, k2, 0.0), jnp.where(lane_k, 0.0, k2)], axis=0)
        lhs = jnp.concatenate([jnp.concatenate([kbs[s][2 * hp], kbs[s][2 * hp + 1]], axis=1),
                               jnp.concatenate([qs[s][2 * hp], qs[s][2 * hp + 1]], axis=1)], axis=0)
        return _dot_nt(lhs, rhs_t)

    scores = [[pair_scores(s, hp) for hp in range(GDN_HEADS // 2)] for s in seqs]
    a_mat = [jnp.where(strict8, jnp.concatenate([r[0:c, :] for r in scores[s]], axis=1) * decay[s], 0.0) for s in seqs]
    attn = [jnp.concatenate([r[c:2 * c, :] for r in scores[s]], axis=1) * decay[s] for s in seqs]
    t_mat = _unit_lower_inverse(a_mat, eye8, same_half8, c)

    def solve(s, h):
        vh = qkv[s][:, 2 * GDN_QK + h * GDN_DV:2 * GDN_QK + (h + 1) * GDN_DV]
        rhs = jnp.concatenate([vh * col_gb(beta[s], h), kbs[s][h] * col_ga(e_g[s], h)], axis=1)
        return _dot(t_mat[s][:, h * c:(h + 1) * c], rhs)

    sol = [[solve(s, h) for h in heads] for s in seqs]

    def chunk_products(s, h):
        attn_h = attn[s][:, h * c:(h + 1) * c]
        k_dec_h = ks[s][h] * col_ga(k_dec[s], h)
        if c % LANES == 0:
            both = _dot(jnp.concatenate([attn_h, k_dec_h.T], axis=0), sol[s][h])
            return both[0:c], both[c:]
        return _dot(attn_h, sol[s][h]), _dot_tn(k_dec_h, sol[s][h])

    prods = [[chunk_products(s, h) for h in heads] for s in seqs]

    def through_state(s, h):
        au_aw, ku_kw = prods[s][h]
        q_eff = qs[s][h] * col_ga(e_g[s], h) - au_aw[:, GDN_DV:]
        return _dot(jnp.concatenate([q_eff, ku_kw[:, GDN_DV:]], axis=0), sto_ref[s, h])

    through = [[through_state(s, h) for h in heads] for s in seqs]
    for s in seqs:
        for h in heads:
            au_aw, ku_kw = prods[s][h]
            o = through[s][h][0:c] + au_aw[:, 0:GDN_DV]
            sto_ref[s, h] = sto_ref[s, h] * col_ga(e_last[s], h) - through[s][h][c:] + ku_kw[:, 0:GDN_DV]
            gt = gate_ref[s, :, h * GDN_DV:(h + 1) * GDN_DV]
            y_ref[s, :, h * GDN_DV:(h + 1) * GDN_DV] = (_rms(o, nw_ref[...]) * _silu(gt)).astype(y_ref.dtype)


def _gdn(proj3, slab3, st, cst, prev, prm, layer, depth, *, c, v0, v1):
    params = [prm["gdn_conv_w"], prm["slab_bias"], prm["slab_alog"], prm["gdn_norm_w"]]
    gd = GDN_HEADS * GDN_DV
    return _scan_call(_gdn_kernel, "gdn_scan", proj3, slab3, [(GDN_QKV, 1), (gd, GATE_BLOCK)],
                      st, cst, params, prev, gd, (GDN_HEADS, GDN_DK, GDN_DV), GDN_QKV, GDN_CONV,
                      [], layer, depth, c=c, v0=v0, v1=v1)


def _mix_kernel(x_ref, ys_ref, yg_ref, ms_ref, mg_ref, wso_ref, wgo_ref, wo_ref, o_ref):
    y_ssm = jnp.dot(ys_ref[...], wso_ref[...], preferred_element_type=F32)
    y_gdn = jnp.dot(yg_ref[...], wgo_ref[...], preferred_element_type=F32)
    merged = _sigmoid(ms_ref[...]) * y_ssm + _sigmoid(mg_ref[...]) * y_gdn
    o_ref[...] = x_ref[...] + _dot(merged, wo_ref[...])


def _mix(x, y_ssm, y_gdn, proj, prm, layer):
    t, d = x.shape
    tm = _pick_tile(t, 512, 16)
    lyr = lambda i: (layer, 0, 0)
    return pl.pallas_call(
        _mix_kernel,
        grid=(t // tm,),
        in_specs=[pl.BlockSpec((tm, d), lambda i: (i, 0)),
                  pl.BlockSpec((tm, D_INNER), lambda i: (i, 0)),
                  pl.BlockSpec((tm, GDN_HEADS * GDN_DV), lambda i: (i, 0)),
                  pl.BlockSpec((tm, d), lambda i: (i, GATE_BLOCK + 1)),
                  pl.BlockSpec((tm, d), lambda i: (i, GATE_BLOCK + 2)),
                  pl.BlockSpec((None, D_INNER, d), lyr),
                  pl.BlockSpec((None, GDN_HEADS * GDN_DV, d), lyr),
                  pl.BlockSpec((None, d, d), lyr)],
        out_specs=pl.BlockSpec((tm, d), lambda i: (i, 0)),
        out_shape=jax.ShapeDtypeStruct((t, d), F32),
        compiler_params=pltpu.CompilerParams(dimension_semantics=("parallel",), vmem_limit_bytes=VMEM_LIMIT),
        name="mix_out",
    )(x, y_ssm, y_gdn, proj, proj, prm["w_ssm_out"], prm["w_gdn_out"], prm["w_o"])


def _ffn_kernel(*refs, nb, rows, vend, has_state):
    if has_state:
        (x_ref, cst_ref, nw_ref, wup_ref, cw_ref, cb_ref, wdn_ref, nnw_ref, xo_ref, hn_ref, csto_ref, buf) = refs
    else:
        (x_ref, nw_ref, wup_ref, cw_ref, cb_ref, wdn_ref, nnw_ref, xo_ref, hn_ref, csto_ref, buf) = refs
    ti = pl.program_id(1)
    last_tile = pl.num_programs(1) - 1
    hist = FFN_CONV - 1

    @pl.when(ti == 0)
    def _():
        buf[:, 0:SUBLANES, :] = jnp.zeros((nb, SUBLANES, D_FF), F32)
        if has_state:
            buf[:, SUBLANES - hist:SUBLANES, :] = cst_ref[...]

    x = x_ref[...]
    gu = _dot(_rms(x, nw_ref[...]), wup_ref[...])
    buf[:, SUBLANES:SUBLANES + rows, :] = gu[:, 0:D_FF].reshape(nb, rows, D_FF)
    conv = _causal_conv(buf.at[0] if nb == 1 else buf, cw_ref, FFN_CONV, rows)

    @pl.when(ti == last_tile)
    def _():
        csto_ref[...] = buf[:, SUBLANES + vend - hist:SUBLANES + vend, :]

    buf[:, 0:SUBLANES, :] = buf[:, rows:rows + SUBLANES, :]
    act = _silu_of_half(conv + cb_ref[...]).reshape(nb * rows, D_FF) * gu[:, D_FF:]
    out = x + _dot(act, wdn_ref[...])
    xo_ref[...] = out
    hn_ref[...] = _rms(out, nnw_ref[...]).astype(hn_ref.dtype)


def _ffn(x, cst, prm, layer, next_norm_w, next_layer, hn_dtype, *, b, lp, v1):
    t, d = x.shape
    has_state = cst is not None
    if lp <= 64:
        rows, nb = lp, _pick_tile(b, max(1, 256 // lp), 1)
    else:
        rows, nb = _pick_tile(lp, FFN_ROWS, 16), 1
    nt = lp // rows
    vend = v1 - (nt - 1) * rows
    assert FFN_CONV - 1 <= vend <= rows
    lyr = lambda bi, i: (layer, 0, 0)
    flat = lambda bi, i: (bi * nt + i, 0)
    per_b = lambda bi, i: (bi, 0, 0)
    in_specs = [pl.BlockSpec((nb * rows, d), flat)]
    args = [x]
    if has_state:
        in_specs.append(pl.BlockSpec((None, nb, FFN_CONV - 1, D_FF), lambda bi, i: (layer, bi, 0, 0)))
        args.append(cst)
    in_specs += [pl.BlockSpec((None, 1, d), lyr),
                 pl.BlockSpec((None, d, 2 * D_FF), lyr, pipeline_mode=pl.Buffered(1)),
                 pl.BlockSpec((None, FFN_CONV, D_FF), lyr),
                 pl.BlockSpec((None, 1, D_FF), lyr),
                 pl.BlockSpec((None, D_FF, d), lyr, pipeline_mode=pl.Buffered(1)),
                 pl.BlockSpec((None, 1, d), lambda bi, i: (next_layer, 0, 0))]
    args += [prm["norm_ffn_w"], prm["w_up"], prm["ffn_conv_w"], prm["ffn_conv_b"], prm["w_down"], next_norm_w]
    return pl.pallas_call(
        functools.partial(_ffn_kernel, nb=nb, rows=rows, vend=vend, has_state=has_state),
        grid=(b // nb, nt),
        in_specs=in_specs,
        out_specs=[pl.BlockSpec((nb * rows, d), flat),
                   pl.BlockSpec((nb * rows, d), flat),
                   pl.BlockSpec((nb, FFN_CONV - 1, D_FF), per_b)],
        out_shape=[jax.ShapeDtypeStruct((t, d), F32),
                   jax.ShapeDtypeStruct((t, d), hn_dtype),
                   jax.ShapeDtypeStruct((b, FFN_CONV - 1, D_FF), F32)],
        scratch_shapes=[pltpu.VMEM((nb, SUBLANES + rows, D_FF), F32)],
        compiler_params=pltpu.CompilerParams(dimension_semantics=("parallel", "arbitrary"),
                                             vmem_limit_bytes=VMEM_LIMIT),
        name="conv_ffn",
    )(*args)


def _prepare_params(norm_mix_w, w_in, ssm_conv_w, ssm_conv_b, ssm_dt_bias, ssm_a_log, ssm_d, ssm_norm_w,
                    gdn_conv_w, gdn_dt_bias, gdn_a_log, gdn_norm_w, w_ssm_out, w_gdn_out, w_o,
                    norm_ffn_w, w_up, ffn_conv_w, ffn_conv_b, w_down, norm_f_w):
    depth = w_in.shape[0]
    offs = [0]
    for s in IN_SIZES:
        offs.append(offs[-1] + s)
    col = lambda k: w_in[:, :, offs[k]:offs[k + 1]]
    w_main = jnp.concatenate([col(1), col(3), col(0), col(6), col(7), col(8)], axis=-1).astype(BF16)
    dt_cols = col(2)
    pad = jnp.zeros((depth, w_in.shape[1], SLAB - SSM_HEADS - 2 * GDN_HEADS), w_in.dtype)
    w_slab = jnp.concatenate([dt_cols[..., 0::2], dt_cols[..., 1::2], col(4), col(5), pad], axis=-1).astype(BF16)

    def slab(ssm_vec, gdn_vec):
        z = jnp.zeros((depth, SLAB - SSM_HEADS - GDN_HEADS), F32)
        return jnp.concatenate([ssm_vec[:, 0::2], ssm_vec[:, 1::2], gdn_vec, z], axis=-1)[:, None, :]

    return {
        "norm_mix_w": norm_mix_w[:, None, :],
        "w_main": w_main,
        "w_slab": w_slab,
        "ssm_conv_w": 0.5 * ssm_conv_w,
        "ssm_conv_b": 0.5 * ssm_conv_b[:, None, :],
        "slab_bias": slab(ssm_dt_bias, gdn_dt_bias),
        "slab_alog": slab(ssm_a_log, gdn_a_log),
        "ssm_d_cols": jnp.repeat(ssm_d, SSM_HEAD_DIM, axis=-1)[:, None, :],
        "ssm_norm_w": ssm_norm_w[:, None, :],
        "gdn_conv_w": 0.5 * gdn_conv_w,
        "gdn_norm_w": gdn_norm_w[:, None, :],
        "w_ssm_out": w_ssm_out.astype(BF16),
        "w_gdn_out": w_gdn_out.astype(BF16),
        "w_o": w_o.astype(BF16),
        "norm_ffn_w": norm_ffn_w[:, None, :],
        "w_up": w_up.astype(BF16),
        "ffn_conv_w": 0.5 * ffn_conv_w,
        "ffn_conv_b": 0.5 * ffn_conv_b[:, None, :],
        "w_down": w_down.astype(BF16),
        "norm_f_w": norm_f_w[None, None, :],
    }


def _run_trunk(x3, states, prm, *, c_ssd, c_gdn, v0, v1):
    b, lp, d = x3.shape
    depth = prm["w_main"].shape[0]
    x = x3.reshape(b * lp, d)
    h = _norm(x, prm["norm_mix_w"], 0, BF16)
    if states is None:
        st_ssm = cst_ssm = st_gdn = cst_gdn = cst_ffn = None
    else:
        st_ssm = states[0].reshape(depth, b, D_INNER, SSM_STATE)
        cst_ssm, st_gdn, cst_gdn, cst_ffn = states[1:]
    o_ssm = o_gdn = None
    conv_outs = [[] for _ in range(3)]
    for l in range(depth):
        proj, slab = _inproj(h, prm["w_main"], prm["w_slab"], l)
        proj3 = proj.reshape(b, lp, PROJ_COLS)
        slab3 = slab.reshape(b, lp, SLAB)
        y_ssm, o_ssm, o_ssm_conv = _ssd(proj3, slab3, st_ssm, cst_ssm, o_ssm, prm, l, depth, c=c_ssd, v0=v0, v1=v1)
        y_gdn, o_gdn, o_gdn_conv = _gdn(proj3, slab3, st_gdn, cst_gdn, o_gdn, prm, l, depth, c=c_gdn, v0=v0, v1=v1)
        x = _mix(x, y_ssm.reshape(b * lp, D_INNER), y_gdn.reshape(b * lp, GDN_HEADS * GDN_DV), proj, prm, l)
        if l + 1 < depth:
            x, h, o_ffn_conv = _ffn(x, cst_ffn, prm, l, prm["norm_mix_w"], l + 1, BF16, b=b, lp=lp, v1=v1)
        else:
            x, h, o_ffn_conv = _ffn(x, cst_ffn, prm, l, prm["norm_f_w"], 0, F32, b=b, lp=lp, v1=v1)
        conv_outs[0].append(o_ssm_conv)
        conv_outs[1].append(o_gdn_conv)
        conv_outs[2].append(o_ffn_conv)
    return (h.reshape(b, lp, d), o_ssm.reshape(depth, b, SSM_HEADS, SSM_HEAD_DIM, SSM_STATE),
            jnp.stack(conv_outs[0]), o_gdn, jnp.stack(conv_outs[1]), jnp.stack(conv_outs[2]))


def kernel(x_prompt, x_sample, state_ssm, state_ssm_conv, state_gdn, state_gdn_conv, state_ffn_conv, meta_tokens, norm_mix_w, w_in, ssm_conv_w, ssm_conv_b, ssm_dt_bias, ssm_a_log, ssm_d, ssm_norm_w, gdn_conv_w, gdn_dt_bias, gdn_a_log, gdn_norm_w, w_ssm_out, w_gdn_out, w_o, norm_ffn_w, w_up, ffn_conv_w, ffn_conv_b, w_down, norm_f_w):
    prm = _prepare_params(norm_mix_w, w_in, ssm_conv_w, ssm_conv_b, ssm_dt_bias, ssm_a_log, ssm_d, ssm_norm_w,
                          gdn_conv_w, gdn_dt_bias, gdn_a_log, gdn_norm_w, w_ssm_out, w_gdn_out, w_o,
                          norm_ffn_w, w_up, ffn_conv_w, ffn_conv_b, w_down, norm_f_w)
    bp, sp, d = x_prompt.shape
    bs, ls, _ = x_sample.shape
    c_p = max(SSD_CHUNK, GDN_CHUNK)
    assert c_p % SSD_CHUNK == 0 and c_p % GDN_CHUNK == 0
    lp = -(-(N_META + sp) // c_p) * c_p
    pad = lp - N_META - sp
    xp = jnp.concatenate([jnp.zeros((bp, pad, d), x_prompt.dtype),
                          jnp.broadcast_to(meta_tokens.astype(x_prompt.dtype), (bp, N_META, d)), x_prompt], axis=1)
    p_out = _run_trunk(xp, None, prm, c_ssd=SSD_CHUNK, c_gdn=GDN_CHUNK, v0=pad, v1=lp)
    y_prompt = p_out[0][:, pad + N_META:]
    c_s = -(-ls // SUBLANES) * SUBLANES
    xs = jnp.concatenate([x_sample, jnp.zeros((bs, c_s - ls, d), x_sample.dtype)], axis=1)
    s_out = _run_trunk(xs, (state_ssm, state_ssm_conv, state_gdn, state_gdn_conv, state_ffn_conv), prm,
                       c_ssd=c_s, c_gdn=c_s, v0=0, v1=ls)
    y_sample = s_out[0][:, :ls]
    return (y_prompt,) + (y_sample,) + p_out[1:] + s_out[1:]
```

```python
import functools

import jax
import jax.numpy as jnp
from jax import lax
from jax.experimental import pallas as pl
from jax.experimental.pallas import tpu as pltpu

F32 = jnp.float32
BF16 = jnp.bfloat16
HIGHEST = lax.Precision.HIGHEST

EPS = 1e-6
N_META = 16
D_MODEL = 1024
SSM_HEADS = 32
SSM_HEAD_DIM = 64
SSM_GROUPS = 4
SSM_STATE = 128
SSM_CONV = 4
D_INNER = SSM_HEADS * SSM_HEAD_DIM
SSM_BC = SSM_GROUPS * SSM_STATE
SSM_XBC = D_INNER + 2 * SSM_BC
SSM_PAIRS = SSM_HEADS // 2
PAIRS_PER_GROUP = SSM_PAIRS // SSM_GROUPS
GROUP_COLS = D_INNER // SSM_GROUPS
GDN_HEADS = 8
GDN_DK = 128
GDN_DV = 128
GDN_CONV = 4
GDN_QK = GDN_HEADS * GDN_DK
GDN_QKV = 2 * GDN_QK + GDN_HEADS * GDN_DV
D_FF = 2816
FFN_CONV = 3
IN_SIZES = (D_INNER, SSM_XBC, SSM_HEADS, GDN_QKV, GDN_HEADS, GDN_HEADS, GDN_HEADS * GDN_DV, D_MODEL, D_MODEL)
PROJ_COLS = SSM_XBC + GDN_QKV + D_INNER + 3 * D_MODEL
Z_BLOCK = (SSM_XBC + GDN_QKV) // D_INNER
GATE_BLOCK = (SSM_XBC + GDN_QKV + D_INNER) // D_MODEL
SLAB = 128
SLAB_GA = SSM_HEADS
SLAB_GB = SSM_HEADS + GDN_HEADS

LANES = 128
SUBLANES = 8
MXU_DIM = 256
SSD_CHUNK = 64
GDN_CHUNK = 64
INVERSE_BLOCK = 64
VMEM_LIMIT = 56 * 1024 * 1024
SSD_SEQS_LONG = 4
GDN_SEQS_LONG = 8
SCAN_SEQS_SHORT = 8
INPROJ_ROWS = 1152
INPROJ_COLS = 2816
FFN_ROWS = 576


def _pick_tile(total, cap, mult):
    best = None
    for t in range(mult, min(total, cap) + 1, mult):
        if total % t == 0:
            best = t
    assert best is not None, (total, cap, mult)
    return best


def _sigmoid(x):
    return 0.5 * jnp.tanh(0.5 * x) + 0.5


def _silu_of_half(half):
    return half * jnp.tanh(half) + half


def _silu(x):
    return _silu_of_half(0.5 * x)


def _softplus(x):
    return jnp.maximum(x, 0.0) + jnp.log1p(jnp.exp(-jnp.abs(x)))


def _rms(x, w):
    return x * lax.rsqrt(jnp.mean(x * x, axis=-1, keepdims=True) + EPS) * w


def _dot(a, b):
    return jnp.dot(a.astype(BF16), b.astype(BF16), preferred_element_type=F32)


def _dot_nt(a, b):
    return lax.dot_general(a.astype(BF16), b.astype(BF16), (((1,), (1,)), ((), ())), preferred_element_type=F32)


def _dot_tn(a, b):
    return lax.dot_general(a.astype(BF16), b.astype(BF16), (((0,), (0,)), ((), ())), preferred_element_type=F32)


def _cumsum_rows(x):
    c = x.shape[0]
    tril = (lax.broadcasted_iota(jnp.int32, (c, c), 0) >= lax.broadcasted_iota(jnp.int32, (c, c), 1)).astype(F32)
    return jnp.dot(tril, x, precision=HIGHEST, preferred_element_type=F32)


def _causal_conv(buf, w_ref, width, rows):
    acc = None
    if len(buf.shape) == 2:
        cols = buf.shape[1]
        tiles = buf[0:SUBLANES + rows, :].reshape(1 + rows // SUBLANES, SUBLANES, cols)
        sublane = lax.broadcasted_iota(jnp.int32, (1, SUBLANES, cols), 1)
        shifted = [tiles[1:]]
        rot = tiles
        for shift in range(1, width):
            rot = pltpu.roll(rot, 1, axis=1)
            shifted.append(jnp.where(sublane < shift, rot[:-1], rot[1:]))
        for j in range(width):
            term = shifted[width - 1 - j] * w_ref[j:j + 1, :]
            acc = term if acc is None else acc + term
        return acc.reshape(rows, cols)
    for j in range(width):
        off = SUBLANES - (width - 1) + j
        term = buf[:, off:off + rows, :] * w_ref[j:j + 1, :]
        acc = term if acc is None else acc + term
    return acc


def _norm_kernel(x_ref, w_ref, o_ref):
    o_ref[...] = _rms(x_ref[...], w_ref[...]).astype(o_ref.dtype)


def _norm(x, w, layer, out_dtype):
    t, d = x.shape
    tm = _pick_tile(t, 1024, 16)
    return pl.pallas_call(
        _norm_kernel,
        grid=(t // tm,),
        in_specs=[pl.BlockSpec((tm, d), lambda i: (i, 0)),
                  pl.BlockSpec((None, 1, d), lambda i: (layer, 0, 0))],
        out_specs=pl.BlockSpec((tm, d), lambda i: (i, 0)),
        out_shape=jax.ShapeDtypeStruct((t, d), out_dtype),
        compiler_params=pltpu.CompilerParams(dimension_semantics=("parallel",)),
        name="rmsnorm",
    )(x, w)


def _inproj_kernel(h_ref, w_ref, ws_ref, o_ref, os_ref):
    h = h_ref[...]
    o_ref[...] = jnp.dot(h, w_ref[...], preferred_element_type=F32)

    @pl.when(pl.program_id(1) == 0)
    def _():
        os_ref[...] = jnp.dot(h, ws_ref[...], preferred_element_type=F32)


def _inproj(h, w_main, w_slab, layer):
    t, d = h.shape
    ncols = w_main.shape[-1]
    tm = _pick_tile(t, INPROJ_ROWS, 16)
    tn = _pick_tile(ncols, INPROJ_COLS, LANES)
    return pl.pallas_call(
        _inproj_kernel,
        grid=(t // tm, ncols // tn),
        in_specs=[pl.BlockSpec((tm, d), lambda i, j: (i, 0)),
                  pl.BlockSpec((None, d, tn), lambda i, j: (layer, 0, j)),
                  pl.BlockSpec((None, d, SLAB), lambda i, j: (layer, 0, 0))],
        out_specs=[pl.BlockSpec((tm, tn), lambda i, j: (i, j)),
                   pl.BlockSpec((tm, SLAB), lambda i, j: (i, 0))],
        out_shape=[jax.ShapeDtypeStruct((t, ncols), F32), jax.ShapeDtypeStruct((t, SLAB), F32)],
        compiler_params=pltpu.CompilerParams(dimension_semantics=("parallel", "arbitrary"),
                                             vmem_limit_bytes=VMEM_LIMIT),
        name="inproj",
    )(h, w_main, w_slab)


def _scan_call(body, name, proj3, slab3, col_blocks, state_in, conv_in, params, prev_state, out_cols,
               state_block, conv_cols, conv_width, scratch, layer, depth, *, c, v0, v1, seqs_long):
    b, lp, _ = proj3.shape
    nc = lp // c
    nb = _pick_tile(b, seqs_long if nc > 1 else SCAN_SEQS_SHORT, 1)
    vend = v1 - (nc - 1) * c
    hist = conv_width - 1
    assert hist <= vend <= c
    has_state = state_in is not None
    has_prev = prev_state is not None
    zeros = (0,) * len(state_block)
    lyr = lambda bi, i: (layer, 0, 0)
    in_specs = [pl.BlockSpec((nb, c, width), functools.partial(lambda blk, bi, i: (bi, i, blk), blk))
                for width, blk in col_blocks]
    in_specs.append(pl.BlockSpec((nb, c, SLAB), lambda bi, i: (bi, i, 0)))
    args = [proj3] * len(col_blocks) + [slab3]
    if has_state:
        in_specs += [pl.BlockSpec((None, nb) + state_block, lambda bi, i: (layer, bi) + zeros),
                     pl.BlockSpec((None, nb, hist, conv_cols), lambda bi, i: (layer, bi, 0, 0))]
        args += [state_in, conv_in]
    for p in params:
        in_specs.append(pl.BlockSpec((None,) + p.shape[1:], lyr))
        args.append(p)
    aliases = {}
    if has_prev:
        aliases[len(args)] = 1
        in_specs.append(pl.BlockSpec(memory_space=pl.ANY))
        args.append(prev_state)
    return pl.pallas_call(
        functools.partial(body, nb=nb, c=c, v0=v0, v1=v1, vend=vend, has_state=has_state, has_prev=has_prev),
        grid=(b // nb, nc),
        in_specs=in_specs,
        out_specs=[pl.BlockSpec((nb, c, out_cols), lambda bi, i: (bi, i, 0)),
                   pl.BlockSpec((None, nb) + state_block, lambda bi, i: (layer, bi) + zeros),
                   pl.BlockSpec((nb, hist, conv_cols), lambda bi, i: (bi, 0, 0))],
        out_shape=[jax.ShapeDtypeStruct((b, lp, out_cols), BF16),
                   jax.ShapeDtypeStruct((depth, b) + state_block, F32),
                   jax.ShapeDtypeStruct((b, hist, conv_cols), F32)],
        scratch_shapes=[pltpu.VMEM((nb, SUBLANES + c, conv_cols), F32)] + [s(nb) for s in scratch],
        input_output_aliases=aliases,
        compiler_params=pltpu.CompilerParams(dimension_semantics=("parallel", "arbitrary"),
                                             vmem_limit_bytes=VMEM_LIMIT),
        name=name,
    )(*args)


def _conv_step(buf, x_ref, cst_out, w_ref, width, c, vend):
    buf[SUBLANES:SUBLANES + c, :] = x_ref[...]
    out = _causal_conv(buf, w_ref, width, c)
    cst_out[...] = buf[SUBLANES + vend - (width - 1):SUBLANES + vend, :]
    buf[0:SUBLANES, :] = buf[c:c + SUBLANES, :]
    return out


def _conv_init(buf, cst_ref, width, cols):
    buf[0:SUBLANES, :] = jnp.zeros((SUBLANES, cols), F32)
    if cst_ref is not None:
        buf[SUBLANES - (width - 1):SUBLANES, :] = cst_ref[...]


def _pair_expand(arr, p, width, half):
    r = arr.shape[0]
    even = jnp.broadcast_to(arr[:, p:p + 1], (r, width))
    odd = jnp.broadcast_to(arr[:, SSM_PAIRS + p:SSM_PAIRS + p + 1], (r, width))
    return jnp.where(half, even, odd)


def _ssd_kernel(*refs, nb, c, v0, v1, vend, has_state, has_prev):
    refs = list(refs)
    xbc_ref, z_ref, sm_ref = refs[:3]
    k = 3
    st_ref = cst_ref = None
    if has_state:
        st_ref, cst_ref = refs[k:k + 2]
        k += 2
    cw_ref, cb_ref, bias_ref, alog_ref, dexp_ref, nw_ref = refs[k:k + 6]
    k += 6 + (1 if has_prev else 0)
    y_ref, sto_ref, csto_ref, buf = refs[k:k + 4]
    ht = None if has_state else refs[-1]
    ci = pl.program_id(1)

    @pl.when(ci == 0)
    def _():
        for s in range(nb):
            _conv_init(buf.at[s], cst_ref.at[s] if has_state else None, SSM_CONV, SSM_XBC)
            if has_state:
                sto_ref[s] = st_ref[s]
            else:
                ht[s] = jnp.zeros(ht.shape[1:], F32)

    row = lax.broadcasted_iota(jnp.int32, (c, SLAB), 0) + ci * c
    valid = jnp.logical_and(row >= v0, row < v1)
    neg_a = -jnp.exp(alog_ref[...])
    w2 = 2 * c
    lane2 = lax.broadcasted_iota(jnp.int32, (c, w2), 1)
    row2 = lax.broadcasted_iota(jnp.int32, (c, w2), 0)
    half2 = lane2 < c
    j2 = jnp.where(half2, lane2, lane2 - c)
    causal2 = j2 <= row2
    upto2 = row2 <= j2
    eye2 = j2 == row2
    half_p = lax.broadcasted_iota(jnp.int32, (c, LANES), 1) < SSM_HEAD_DIM
    even_rows = lax.broadcasted_iota(jnp.int32, (LANES, SSM_STATE), 0) < SSM_HEAD_DIM

    seqs, groups, pairs = range(nb), range(SSM_GROUPS), range(SSM_PAIRS)
    gcols = [slice(g * GROUP_COLS, (g + 1) * GROUP_COLS) for g in groups]
    pcols = [slice(p * LANES, (p + 1) * LANES) for p in pairs]
    xbc = [_silu_of_half(_conv_step(buf.at[s], xbc_ref.at[s], csto_ref.at[s], cw_ref, SSM_CONV, c, vend)
                         + cb_ref[...]) for s in seqs]
    dt = [jnp.where(valid, _softplus(sm_ref[s] + bias_ref[...]), 0.0) for s in seqs]
    acum = [_cumsum_rows(dt[s] * neg_a) for s in seqs]
    b_g = [[xbc[s][:, D_INNER + g * SSM_STATE:D_INNER + (g + 1) * SSM_STATE] for g in groups] for s in seqs]
    c_g = [[xbc[s][:, D_INNER + SSM_BC + g * SSM_STATE:D_INNER + SSM_BC + (g + 1) * SSM_STATE] for g in groups]
           for s in seqs]
    cb = [[_dot_nt(c_g[s][g], b_g[s][g]) for g in groups] for s in seqs]
    y_off = [[_dot_nt(c_g[s][g], sto_ref[s, gcols[g], :]) if has_state else _dot(c_g[s][g], ht[s, :, gcols[g]])
              for g in groups] for s in seqs]

    def intra(s, p):
        g = p // PAIRS_PER_GROUP
        dx = _pair_expand(dt[s], p, w2, half2)
        ax = dx * _pair_expand(neg_a, p, w2, half2[0:1])
        row_cum = jnp.sum(jnp.where(upto2, ax, 0.0), axis=0, keepdims=True)
        dt_row = jnp.sum(jnp.where(eye2, dx, 0.0), axis=0, keepdims=True)
        col_cum = _pair_expand(acum[s], p, w2, half2)
        decay = jnp.where(causal2, jnp.exp(col_cum - row_cum), 0.0)
        wgt = decay * jnp.concatenate([cb[s][g], cb[s][g]], axis=1) * dt_row
        xp = xbc[s][:, pcols[p]]
        rhs = jnp.concatenate([jnp.where(half_p, xp, 0.0), jnp.where(half_p, 0.0, xp)], axis=0)
        if w2 == LANES:
            acum_p, dt_p = col_cum, dx
        else:
            acum_p, dt_p = _pair_expand(acum[s], p, LANES, half_p), _pair_expand(dt[s], p, LANES, half_p)
        return _dot(wgt, rhs), acum_p, dt_p

    parts = [[intra(s, p) for p in pairs] for s in seqs]

    def pair_outputs(s, p):
        y_in, acum_p, dt_p = parts[s][p]
        g, q = divmod(p, PAIRS_PER_GROUP)
        xp = xbc[s][:, pcols[p]]
        last_p = acum_p[c - 1:c, :]
        y = y_in + y_off[s][g][:, q * LANES:(q + 1) * LANES] * jnp.exp(acum_p) + dexp_ref[:, pcols[p]] * xp
        return y, xp * (jnp.exp(last_p - acum_p) * dt_p), jnp.exp(last_p)

    outs = [[pair_outputs(s, p) for p in pairs] for s in seqs]
    in_group = lambda s, g, k: [outs[s][p][k] for p in range(g * PAIRS_PER_GROUP, (g + 1) * PAIRS_PER_GROUP)]
    xw = [[jnp.concatenate(in_group(s, g, 1), axis=1) for g in groups] for s in seqs]
    upd = [[_dot_tn(xw[s][g], b_g[s][g]) if has_state else _dot_tn(b_g[s][g], xw[s][g]) for g in groups]
           for s in seqs]
    for s in seqs:
        e_last = jnp.exp(acum[s][c - 1:c, :])
        for g in groups:
            if has_state:
                scale = jnp.concatenate(
                    [jnp.where(even_rows,
                               jnp.broadcast_to(e_last[:, p:p + 1], (LANES, SSM_STATE)),
                               jnp.broadcast_to(e_last[:, SSM_PAIRS + p:SSM_PAIRS + p + 1], (LANES, SSM_STATE)))
                     for p in range(g * PAIRS_PER_GROUP, (g + 1) * PAIRS_PER_GROUP)], axis=0)
                sto_ref[s, gcols[g], :] = sto_ref[s, gcols[g], :] * scale + upd[s][g]
            else:
                ht[s, :, gcols[g]] = ht[s, :, gcols[g]] * jnp.concatenate(in_group(s, g, 2), axis=1) + upd[s][g]
            yz = jnp.concatenate(in_group(s, g, 0), axis=1) * _silu(z_ref[s, :, gcols[g]])
            y_ref[s, :, gcols[g]] = _rms(yz, nw_ref[:, gcols[g]]).astype(y_ref.dtype)

    if not has_state:
        @pl.when(ci == pl.num_programs(1) - 1)
        def _():
            for s in range(nb):
                for p in range(SSM_PAIRS):
                    sto_ref[s, p * LANES:(p + 1) * LANES, :] = ht[s, :, p * LANES:(p + 1) * LANES].T


def _ssd(proj3, slab3, st, cst, prev, prm, layer, depth, *, c, v0, v1):
    params = [prm["ssm_conv_w"], prm["ssm_conv_b"], prm["slab_bias"], prm["slab_alog"], prm["ssm_d_cols"],
              prm["ssm_norm_w"]]
    return _scan_call(_ssd_kernel, "ssd_scan", proj3, slab3, [(SSM_XBC, 0), (D_INNER, Z_BLOCK)], st, cst, params,
                      prev, D_INNER, (D_INNER, SSM_STATE), SSM_XBC, SSM_CONV,
                      [] if st is not None else [lambda nb: pltpu.VMEM((nb, SSM_STATE, D_INNER), F32)],
                      layer, depth, c=c, v0=v0, v1=v1, seqs_long=SSD_SEQS_LONG)


def _head_products(x, y, c):
    per_op = min(GDN_HEADS, MXU_DIM // c)
    width = per_op * c
    lane = lax.broadcasted_iota(jnp.int32, (c, width), 1)
    outs = []
    head = lane // c if per_op > 1 else None
    for s in range(GDN_HEADS // per_op):
        ys = y[:, s * width:(s + 1) * width]
        diag = ys if per_op == 1 else jnp.concatenate(
            [jnp.where(head == r, ys, 0.0) for r in range(per_op)], axis=0)
        outs.append(_dot(x[:, s * width:(s + 1) * width], diag))
    return outs[0] if len(outs) == 1 else jnp.concatenate(outs, axis=1)


def _unit_lower_inverse(mats, eye, same_half, c):
    blk = min(c, INVERSE_BLOCK)
    assert c in (blk, 2 * blk)
    ns = [-(a if blk == c else jnp.where(same_half, a, 0.0)) for a in mats]
    ts = [eye + n for n in ns]
    ms = [_head_products(n, n, c) for n in ns]
    covered = 2
    while 2 * covered < blk:
        boths = [_head_products(jnp.concatenate([t, m], axis=0), m, c) for t, m in zip(ts, ms)]
        ts = [t + both[0:c] for t, both in zip(ts, boths)]
        ms = [both[c:2 * c] for both in boths]
        covered *= 2
    ds = [t + _head_products(t, m, c) for t, m in zip(ts, ms)]
    if blk == c:
        return ds
    firsts = [_head_products(d, jnp.where(same_half, 0.0, a), c) for d, a in zip(ds, mats)]
    return [d - _head_products(f, d, c) for d, f in zip(ds, firsts)]


def _gdn_kernel(*refs, nb, c, v0, v1, vend, has_state, has_prev):
    refs = list(refs)
    qkv_ref, gate_ref, sm_ref = refs[:3]
    k = 3
    st_ref = cst_ref = None
    if has_state:
        st_ref, cst_ref = refs[k:k + 2]
        k += 2
    cw_ref, bias_ref, alog_ref, nw_ref = refs[k:k + 4]
    k += 4 + (1 if has_prev else 0)
    y_ref, sto_ref, csto_ref, buf = refs[k:]
    ci = pl.program_id(1)

    @pl.when(ci == 0)
    def _():
        for s in range(nb):
            _conv_init(buf.at[s], cst_ref.at[s] if has_state else None, GDN_CONV, GDN_QKV)
            sto_ref[s] = st_ref[s] if has_state else jnp.zeros(sto_ref.shape[1:], F32)

    row = lax.broadcasted_iota(jnp.int32, (c, SLAB), 0) + ci * c
    valid = jnp.logical_and(row >= v0, row < v1)
    neg_a = -jnp.exp(alog_ref[...])
    w8 = GDN_HEADS * c
    lane8 = lax.broadcasted_iota(jnp.int32, (c, w8), 1)
    row8 = lax.broadcasted_iota(jnp.int32, (c, w8), 0)
    j8 = lane8 & (c - 1)
    causal8 = j8 <= row8
    strict8 = j8 < row8
    upto8 = row8 <= j8
    eye8 = jnp.where(j8 == row8, 1.0, 0.0)
    same_half8 = (j8 // INVERSE_BLOCK) == (row8 // INVERSE_BLOCK)
    lane_k =lax.broadcasted_iota(jnp.int32, (c, 2 * GDN_DK), 1) < GDN_DK

    def heads_on_lanes(arr):
        return jnp.concatenate(
            [jnp.broadcast_to(arr[:, SLAB_GA + h:SLAB_GA + h + 1], (c, c)) for h in range(GDN_HEADS)], axis=1)

    seqs, heads = range(nb), range(GDN_HEADS)
    col_ga = lambda arr, h: arr[:, SLAB_GA + h:SLAB_GA + h + 1]
    col_gb = lambda arr, h: arr[:, SLAB_GB + h:SLAB_GB + h + 1]
    qkv = [_silu_of_half(_conv_step(buf.at[s], qkv_ref.at[s], csto_ref.at[s], cw_ref, GDN_CONV, c, vend))
           for s in seqs]
    g = [jnp.where(valid, neg_a * _softplus(sm_ref[s] + bias_ref[...]), 0.0) for s in seqs]
    beta = [jnp.where(valid, _sigmoid(sm_ref[s]), 0.0) for s in seqs]
    gcum = [_cumsum_rows(g[s]) for s in seqs]
    e_g = [jnp.exp(gcum[s]) for s in seqs]
    e_last = [jnp.exp(gcum[s][c - 1:c, :]) for s in seqs]
    k_dec = [jnp.exp(gcum[s][c - 1:c, :] - gcum[s]) for s in seqs]
    row_cum = [jnp.sum(jnp.where(upto8, heads_on_lanes(g[s]), 0.0), axis=0, keepdims=True) for s in seqs]
    decay = [jnp.where(causal8, jnp.exp(heads_on_lanes(gcum[s]) - row_cum[s]), 0.0) for s in seqs]

    def unit(x):
        return x * lax.rsqrt(jnp.sum(x * x, axis=-1, keepdims=True) + EPS)

    qs = [[unit(qkv[s][:, h * GDN_DK:(h + 1) * GDN_DK]) * (GDN_DK ** -0.5) for h in heads] for s in seqs]
    ks = [[unit(qkv[s][:, GDN_QK + h * GDN_DK:GDN_QK + (h + 1) * GDN_DK]) for h in heads] for s in seqs]
    kbs = [[ks[s][h] * col_gb(beta[s], h) for h in heads] for s in seqs]

    def pair_scores(s, hp):
        k2 = jnp.concatenate([ks[s][2 * hp], ks[s][2 * hp + 1]], axis=1)
        rhs_t = jnp.concatenate([jnp.where(lane_k, k2, 0.0), jnp.where(lane_k, 0.0, k2)], axis=0)
        lhs = jnp.concatenate([jnp.concatenate([kbs[s][2 * hp], kbs[s][2 * hp + 1]], axis=1),
                               jnp.concatenate([qs[s][2 * hp], qs[s][2 * hp + 1]], axis=1)], axis=0)
        return _dot_nt(lhs, rhs_t)

    scores = [[pair_scores(s, hp) for hp in range(GDN_HEADS // 2)] for s in seqs]
    a_mat = [jnp.where(strict8, jnp.concatenate([r[0:c, :] for r in scores[s]], axis=1) * decay[s], 0.0) for s in seqs]
    attn = [jnp.concatenate([r[c:2 * c, :] for r in scores[s]], axis=1) * decay[s] for s in seqs]
    t_mat = _unit_lower_inverse(a_mat, eye8, same_half8, c)

    def solve(s, h):
        vh = qkv[s][:, 2 * GDN_QK + h * GDN_DV:2 * GDN_QK + (h + 1) * GDN_DV]
        rhs = jnp.concatenate([vh * col_gb(beta[s], h), kbs[s][h] * col_ga(e_g[s], h)], axis=1)
        return _dot(t_mat[s][:, h * c:(h + 1) * c], rhs)

    sol = [[solve(s, h) for h in heads] for s in seqs]

    def chunk_products(s, h):
        attn_h = attn[s][:, h * c:(h + 1) * c]
        k_dec_h = ks[s][h] * col_ga(k_dec[s], h)
        if c % LANES == 0:
            both = _dot(jnp.concatenate([attn_h, k_dec_h.T], axis=0), sol[s][h])
            return both[0:c], both[c:]
        return _dot(attn_h, sol[s][h]), _dot_tn(k_dec_h, sol[s][h])

    prods = [[chunk_products(s, h) for h in heads] for s in seqs]

    def through_state(s, h):
        au_aw, ku_kw = prods[s][h]
        q_eff = qs[s][h] * col_ga(e_g[s], h) - au_aw[:, GDN_DV:]
        return _dot(jnp.concatenate([q_eff, ku_kw[:, GDN_DV:]], axis=0), sto_ref[s, h])

    through = [[through_state(s, h) for h in heads] for s in seqs]
    for s in seqs:
        for h in heads:
            au_aw, ku_kw = prods[s][h]
            o = through[s][h][0:c] + au_aw[:, 0:GDN_DV]
            sto_ref[s, h] = sto_ref[s, h] * col_ga(e_last[s], h) - through[s][h][c:] + ku_kw[:, 0:GDN_DV]
            gt = gate_ref[s, :, h * GDN_DV:(h + 1) * GDN_DV]
            y_ref[s, :, h * GDN_DV:(h + 1) * GDN_DV] = (_rms(o, nw_ref[...]) * _silu(gt)).astype(y_ref.dtype)


def _gdn(proj3, slab3, st, cst, prev, prm, layer, depth, *, c, v0, v1):
    params = [prm["gdn_conv_w"], prm["slab_bias"], prm["slab_alog"], prm["gdn_norm_w"]]
    gd = GDN_HEADS * GDN_DV
    return _scan_call(_gdn_kernel, "gdn_scan", proj3, slab3, [(GDN_QKV, 1), (gd, GATE_BLOCK)],
                      st, cst, params, prev, gd, (GDN_HEADS, GDN_DK, GDN_DV), GDN_QKV, GDN_CONV,
                      [], layer, depth, c=c, v0=v0, v1=v1, seqs_long=GDN_SEQS_LONG)


def _mix_kernel(x_ref, ys_ref, yg_ref, ms_ref, mg_ref, wso_ref, wgo_ref, wo_ref, o_ref):
    y_ssm = jnp.dot(ys_ref[...], wso_ref[...], preferred_element_type=F32)
    y_gdn = jnp.dot(yg_ref[...], wgo_ref[...], preferred_element_type=F32)
    merged = _sigmoid(ms_ref[...]) * y_ssm + _sigmoid(mg_ref[...]) * y_gdn
    o_ref[...] = x_ref[...] + _dot(merged, wo_ref[...])


def _mix(x, y_ssm, y_gdn, proj, prm, layer):
    t, d = x.shape
    tm = _pick_tile(t, 512, 16)
    lyr = lambda i: (layer, 0, 0)
    return pl.pallas_call(
        _mix_kernel,
        grid=(t // tm,),
        in_specs=[pl.BlockSpec((tm, d), lambda i: (i, 0)),
                  pl.BlockSpec((tm, D_INNER), lambda i: (i, 0)),
                  pl.BlockSpec((tm, GDN_HEADS * GDN_DV), lambda i: (i, 0)),
                  pl.BlockSpec((tm, d), lambda i: (i, GATE_BLOCK + 1)),
                  pl.BlockSpec((tm, d), lambda i: (i, GATE_BLOCK + 2)),
                  pl.BlockSpec((None, D_INNER, d), lyr),
                  pl.BlockSpec((None, GDN_HEADS * GDN_DV, d), lyr),
                  pl.BlockSpec((None, d, d), lyr)],
        out_specs=pl.BlockSpec((tm, d), lambda i: (i, 0)),
        out_shape=jax.ShapeDtypeStruct((t, d), F32),
        compiler_params=pltpu.CompilerParams(dimension_semantics=("parallel",), vmem_limit_bytes=VMEM_LIMIT),
        name="mix_out",
    )(x, y_ssm, y_gdn, proj, proj, prm["w_ssm_out"], prm["w_gdn_out"], prm["w_o"])


def _ffn_kernel(*refs, nb, rows, vend, has_state):
    if has_state:
        (x_ref, cst_ref, nw_ref, wup_ref, cw_ref, cb_ref, wdn_ref, nnw_ref, xo_ref, hn_ref, csto_ref, buf) = refs
    else:
        (x_ref, nw_ref, wup_ref, cw_ref, cb_ref, wdn_ref, nnw_ref, xo_ref, hn_ref, csto_ref, buf) = refs
    ti = pl.program_id(1)
    last_tile = pl.num_programs(1) - 1
    hist = FFN_CONV - 1

    @pl.when(ti == 0)
    def _():
        buf[:, 0:SUBLANES, :] = jnp.zeros((nb, SUBLANES, D_FF), F32)
        if has_state:
            buf[:, SUBLANES - hist:SUBLANES, :] = cst_ref[...]

    x = x_ref[...]
    gu = _dot(_rms(x, nw_ref[...]), wup_ref[...])
    buf[:, SUBLANES:SUBLANES + rows, :] = gu[:, 0:D_FF].reshape(nb, rows, D_FF)
    conv = _causal_conv(buf.at[0] if nb == 1 else buf, cw_ref, FFN_CONV, rows)

    @pl.when(ti == last_tile)
    def _():
        csto_ref[...] = buf[:, SUBLANES + vend - hist:SUBLANES + vend, :]

    buf[:, 0:SUBLANES, :] = buf[:, rows:rows + SUBLANES, :]
    act = _silu_of_half(conv + cb_ref[...]).reshape(nb * rows, D_FF) * gu[:, D_FF:]
    out = x + _dot(act, wdn_ref[...])
    xo_ref[...] = out
    hn_ref[...] = _rms(out, nnw_ref[...]).astype(hn_ref.dtype)


def _ffn(x, cst, prm, layer, next_norm_w, next_layer, hn_dtype, *, b, lp, v1):
    t, d = x.shape
    has_state = cst is not None
    if lp <= 64:
        rows, nb = lp, _pick_tile(b, max(1, 256 // lp), 1)
    else:
        rows, nb = _pick_tile(lp, FFN_ROWS, 16), 1
    nt = lp // rows
    vend = v1 - (nt - 1) * rows
    assert FFN_CONV - 1 <= vend <= rows
    lyr = lambda bi, i: (layer, 0, 0)
    flat = lambda bi, i: (bi * nt + i, 0)
    per_b = lambda bi, i: (bi, 0, 0)
    in_specs = [pl.BlockSpec((nb * rows, d), flat)]
    args = [x]
    if has_state:
        in_specs.append(pl.BlockSpec((None, nb, FFN_CONV - 1, D_FF), lambda bi, i: (layer, bi, 0, 0)))
        args.append(cst)
    in_specs += [pl.BlockSpec((None, 1, d), lyr),
                 pl.BlockSpec((None, d, 2 * D_FF), lyr, pipeline_mode=pl.Buffered(1)),
                 pl.BlockSpec((None, FFN_CONV, D_FF), lyr),
                 pl.BlockSpec((None, 1, D_FF), lyr),
                 pl.BlockSpec((None, D_FF, d), lyr, pipeline_mode=pl.Buffered(1)),
                 pl.BlockSpec((None, 1, d), lambda bi, i: (next_layer, 0, 0))]
    args += [prm["norm_ffn_w"], prm["w_up"], prm["ffn_conv_w"], prm["ffn_conv_b"], prm["w_down"], next_norm_w]
    return pl.pallas_call(
        functools.partial(_ffn_kernel, nb=nb, rows=rows, vend=vend, has_state=has_state),
        grid=(b // nb, nt),
        in_specs=in_specs,
        out_specs=[pl.BlockSpec((nb * rows, d), flat),
                   pl.BlockSpec((nb * rows, d), flat),
                   pl.BlockSpec((nb, FFN_CONV - 1, D_FF), per_b)],
        out_shape=[jax.ShapeDtypeStruct((t, d), F32),
                   jax.ShapeDtypeStruct((t, d), hn_dtype),
                   jax.ShapeDtypeStruct((b, FFN_CONV - 1, D_FF), F32)],
        scratch_shapes=[pltpu.VMEM((nb, SUBLANES + rows, D_FF), F32)],
        compiler_params=pltpu.CompilerParams(dimension_semantics=("parallel", "arbitrary"),
                                             vmem_limit_bytes=VMEM_LIMIT),
        name="conv_ffn",
    )(*args)


def _prepare_params(norm_mix_w, w_in, ssm_conv_w, ssm_conv_b, ssm_dt_bias, ssm_a_log, ssm_d, ssm_norm_w,
                    gdn_conv_w, gdn_dt_bias, gdn_a_log, gdn_norm_w, w_ssm_out, w_gdn_out, w_o,
                    norm_ffn_w, w_up, ffn_conv_w, ffn_conv_b, w_down, norm_f_w):
    depth = w_in.shape[0]
    offs = [0]
    for s in IN_SIZES:
        offs.append(offs[-1] + s)
    col = lambda k: w_in[:, :, offs[k]:offs[k + 1]]
    w_main = jnp.concatenate([col(1), col(3), col(0), col(6), col(7), col(8)], axis=-1).astype(BF16)
    dt_cols = col(2)
    pad = jnp.zeros((depth, w_in.shape[1], SLAB - SSM_HEADS - 2 * GDN_HEADS), w_in.dtype)
    w_slab = jnp.concatenate([dt_cols[..., 0::2], dt_cols[..., 1::2], col(4), col(5), pad], axis=-1).astype(BF16)

    def slab(ssm_vec, gdn_vec):
        z = jnp.zeros((depth, SLAB - SSM_HEADS - GDN_HEADS), F32)
        return jnp.concatenate([ssm_vec[:, 0::2], ssm_vec[:, 1::2], gdn_vec, z], axis=-1)[:, None, :]

    return {
        "norm_mix_w": norm_mix_w[:, None, :],
        "w_main": w_main,
        "w_slab": w_slab,
        "ssm_conv_w": 0.5 * ssm_conv_w,
        "ssm_conv_b": 0.5 * ssm_conv_b[:, None, :],
        "slab_bias": slab(ssm_dt_bias, gdn_dt_bias),
        "slab_alog": slab(ssm_a_log, gdn_a_log),
        "ssm_d_cols": jnp.repeat(ssm_d, SSM_HEAD_DIM, axis=-1)[:, None, :],
        "ssm_norm_w": ssm_norm_w[:, None, :],
        "gdn_conv_w": 0.5 * gdn_conv_w,
        "gdn_norm_w": gdn_norm_w[:, None, :],
        "w_ssm_out": w_ssm_out.astype(BF16),
        "w_gdn_out": w_gdn_out.astype(BF16),
        "w_o": w_o.astype(BF16),
        "norm_ffn_w": norm_ffn_w[:, None, :],
        "w_up": w_up.astype(BF16),
        "ffn_conv_w": 0.5 * ffn_conv_w,
        "ffn_conv_b": 0.5 * ffn_conv_b[:, None, :],
        "w_down": w_down.astype(BF16),
        "norm_f_w": norm_f_w[None, None, :],
    }


def _run_trunk(x3, states, prm, *, c_ssd, c_gdn, v0, v1):
    b, lp, d = x3.shape
    depth = prm["w_main"].shape[0]
    x = x3.reshape(b * lp, d)
    h = _norm(x, prm["norm_mix_w"], 0, BF16)
    if states is None:
        st_ssm = cst_ssm = st_gdn = cst_gdn = cst_ffn = None
    else:
        st_ssm = states[0].reshape(depth, b, D_INNER, SSM_STATE)
        cst_ssm, st_gdn, cst_gdn, cst_ffn = states[1:]
    o_ssm = o_gdn = None
    conv_outs = [[] for _ in range(3)]
    for l in range(depth):
        proj, slab = _inproj(h, prm["w_main"], prm["w_slab"], l)
        proj3 = proj.reshape(b, lp, PROJ_COLS)
        slab3 = slab.reshape(b, lp, SLAB)
        y_ssm, o_ssm, o_ssm_conv = _ssd(proj3, slab3, st_ssm, cst_ssm, o_ssm, prm, l, depth, c=c_ssd, v0=v0, v1=v1)
        y_gdn, o_gdn, o_gdn_conv = _gdn(proj3, slab3, st_gdn, cst_gdn, o_gdn, prm, l, depth, c=c_gdn, v0=v0, v1=v1)
        x = _mix(x, y_ssm.reshape(b * lp, D_INNER), y_gdn.reshape(b * lp, GDN_HEADS * GDN_DV), proj, prm, l)
        if l + 1 < depth:
            x, h, o_ffn_conv = _ffn(x, cst_ffn, prm, l, prm["norm_mix_w"], l + 1, BF16, b=b, lp=lp, v1=v1)
        else:
            x, h, o_ffn_conv = _ffn(x, cst_ffn, prm, l, prm["norm_f_w"], 0, F32, b=b, lp=lp, v1=v1)
        conv_outs[0].append(o_ssm_conv)
        conv_outs[1].append(o_gdn_conv)
        conv_outs[2].append(o_ffn_conv)
    return (h.reshape(b, lp, d), o_ssm.reshape(depth, b, SSM_HEADS, SSM_HEAD_DIM, SSM_STATE),
            jnp.stack(conv_outs[0]), o_gdn, jnp.stack(conv_outs[1]), jnp.stack(conv_outs[2]))


def kernel(x_prompt, x_sample, state_ssm, state_ssm_conv, state_gdn, state_gdn_conv, state_ffn_conv, meta_tokens, norm_mix_w, w_in, ssm_conv_w, ssm_conv_b, ssm_dt_bias, ssm_a_log, ssm_d, ssm_norm_w, gdn_conv_w, gdn_dt_bias, gdn_a_log, gdn_norm_w, w_ssm_out, w_gdn_out, w_o, norm_ffn_w, w_up, ffn_conv_w, ffn_conv_b, w_down, norm_f_w):
    prm = _prepare_params(norm_mix_w, w_in, ssm_conv_w, ssm_conv_b, ssm_dt_bias, ssm_a_log, ssm_d, ssm_norm_w,
                          gdn_conv_w, gdn_dt_bias, gdn_a_log, gdn_norm_w, w_ssm_out, w_gdn_out, w_o,
                          norm_ffn_w, w_up, ffn_conv_w, ffn_conv_b, w_down, norm_f_w)
    bp, sp, d = x_prompt.shape
    bs, ls, _ = x_sample.shape
    c_p = max(SSD_CHUNK, GDN_CHUNK)
    assert c_p % SSD_CHUNK == 0 and c_p % GDN_CHUNK == 0
    lp = -(-(N_META + sp) // c_p) * c_p
    pad = lp - N_META - sp
    xp = jnp.concatenate([jnp.zeros((bp, pad, d), x_prompt.dtype),
                          jnp.broadcast_to(meta_tokens.astype(x_prompt.dtype), (bp, N_META, d)), x_prompt], axis=1)
    p_out = _run_trunk(xp, None, prm, c_ssd=SSD_CHUNK, c_gdn=GDN_CHUNK, v0=pad, v1=lp)
    y_prompt = p_out[0][:, pad + N_META:]
    c_s = -(-ls // SUBLANES) * SUBLANES
    xs = jnp.concatenate([x_sample, jnp.zeros((bs, c_s - ls, d), x_sample.dtype)], axis=1)
    s_out = _run_trunk(xs, (state_ssm, state_ssm_conv, state_gdn, state_gdn_conv, state_ffn_conv), prm,
                       c_ssd=c_s, c_gdn=c_s, v0=0, v1=ls)
    y_sample = s_out[0][:, :ls]
    return (y_prompt,) + (y_sample,) + p_out[1:] + s_out[1:]
```

```python
import functools

import jax
import jax.numpy as jnp
from jax import lax
from jax.experimental import pallas as pl
from jax.experimental.pallas import tpu as pltpu

F32 = jnp.float32
BF16 = jnp.bfloat16
HIGHEST = lax.Precision.HIGHEST

EPS = 1e-6
N_META = 16
D_MODEL = 1024
SSM_HEADS = 32
SSM_HEAD_DIM = 64
SSM_GROUPS = 4
SSM_STATE = 128
SSM_CONV = 4
D_INNER = SSM_HEADS * SSM_HEAD_DIM
SSM_BC = SSM_GROUPS * SSM_STATE
SSM_XBC = D_INNER + 2 * SSM_BC
SSM_PAIRS = SSM_HEADS // 2
PAIRS_PER_GROUP = SSM_PAIRS // SSM_GROUPS
GROUP_COLS = D_INNER // SSM_GROUPS
GDN_HEADS = 8
GDN_DK = 128
GDN_DV = 128
GDN_CONV = 4
GDN_QK = GDN_HEADS * GDN_DK
GDN_QKV = 2 * GDN_QK + GDN_HEADS * GDN_DV
D_FF = 2816
FFN_CONV = 3
IN_SIZES = (D_INNER, SSM_XBC, SSM_HEADS, GDN_QKV, GDN_HEADS, GDN_HEADS, GDN_HEADS * GDN_DV, D_MODEL, D_MODEL)
PROJ_COLS = SSM_XBC + GDN_QKV + D_INNER + 3 * D_MODEL
Z_BLOCK = (SSM_XBC + GDN_QKV) // D_INNER
GATE_BLOCK = (SSM_XBC + GDN_QKV + D_INNER) // D_MODEL
SLAB = 128
SLAB_GA = SSM_HEADS
SLAB_GB = SSM_HEADS + GDN_HEADS

LANES = 128
SUBLANES = 8
BF16_ROWS = 16
MXU_DIM = 256
VMEM_BYTES = 64 * 1024 * 1024
VMEM_LIMIT = VMEM_BYTES - 8 * 1024 * 1024

SSD_CHUNK = 64
GDN_CHUNK = 64
INVERSE_BLOCK = 64
SSD_SEQS_LONG = 4
GDN_SEQS_LONG = 8
SCAN_SEQS_SHORT = 8
NORM_ROWS = 1024
INPROJ_ROWS = 1152
INPROJ_COLS = 2816
MIX_ROWS = 512
FFN_ROWS = 576
FFN_SHORT_ROWS = 256


def _pick_tile(total, cap, mult):
    best = None
    for t in range(mult, min(total, cap) + 1, mult):
        if total % t == 0:
            best = t
    assert best is not None, (total, cap, mult)
    return best


def _sigmoid(x):
    return 0.5 * jnp.tanh(0.5 * x) + 0.5


def _silu_of_half(half):
    return half * jnp.tanh(half) + half


def _silu(x):
    return _silu_of_half(0.5 * x)


def _softplus(x):
    return jnp.maximum(x, 0.0) + jnp.log1p(jnp.exp(-jnp.abs(x)))


def _rms(x, w):
    return x * lax.rsqrt(jnp.mean(x * x, axis=-1, keepdims=True) + EPS) * w


def _dot(a, b):
    return jnp.dot(a.astype(BF16), b.astype(BF16), preferred_element_type=F32)


def _dot_nt(a, b):
    return lax.dot_general(a.astype(BF16), b.astype(BF16), (((1,), (1,)), ((), ())), preferred_element_type=F32)


def _dot_tn(a, b):
    return lax.dot_general(a.astype(BF16), b.astype(BF16), (((0,), (0,)), ((), ())), preferred_element_type=F32)


def _cumsum_rows(x):
    c = x.shape[0]
    tril = (lax.broadcasted_iota(jnp.int32, (c, c), 0) >= lax.broadcasted_iota(jnp.int32, (c, c), 1)).astype(F32)
    return jnp.dot(tril, x, precision=HIGHEST, preferred_element_type=F32)


def _causal_conv(buf, w_ref, width, rows):
    acc = None
    if len(buf.shape) == 2:
        cols = buf.shape[1]
        tiles = buf[0:SUBLANES + rows, :].reshape(1 + rows // SUBLANES, SUBLANES, cols)
        sublane = lax.broadcasted_iota(jnp.int32, (1, SUBLANES, cols), 1)
        shifted = [tiles[1:]]
        rot = tiles
        for shift in range(1, width):
            rot = pltpu.roll(rot, 1, axis=1)
            shifted.append(jnp.where(sublane < shift, rot[:-1], rot[1:]))
        for j in range(width):
            term = shifted[width - 1 - j] * w_ref[j:j + 1, :]
            acc = term if acc is None else acc + term
        return acc.reshape(rows, cols)
    for j in range(width):
        off = SUBLANES - (width - 1) + j
        term = buf[:, off:off + rows, :] * w_ref[j:j + 1, :]
        acc = term if acc is None else acc + term
    return acc


def _norm_kernel(x_ref, w_ref, o_ref):
    o_ref[...] = _rms(x_ref[...], w_ref[...]).astype(o_ref.dtype)


def _norm(x, w, layer, out_dtype):
    t, d = x.shape
    tm = _pick_tile(t, NORM_ROWS, BF16_ROWS)
    return pl.pallas_call(
        _norm_kernel,
        grid=(t // tm,),
        in_specs=[pl.BlockSpec((tm, d), lambda i: (i, 0)),
                  pl.BlockSpec((None, 1, d), lambda i: (layer, 0, 0))],
        out_specs=pl.BlockSpec((tm, d), lambda i: (i, 0)),
        out_shape=jax.ShapeDtypeStruct((t, d), out_dtype),
        compiler_params=pltpu.CompilerParams(dimension_semantics=("parallel",)),
        name="rmsnorm",
    )(x, w)


def _inproj_kernel(h_ref, w_ref, ws_ref, o_ref, os_ref):
    h = h_ref[...]
    o_ref[...] = jnp.dot(h, w_ref[...], preferred_element_type=F32)

    @pl.when(pl.program_id(1) == 0)
    def _():
        os_ref[...] = jnp.dot(h, ws_ref[...], preferred_element_type=F32)


def _inproj(h, w_main, w_slab, layer):
    t, d = h.shape
    ncols = w_main.shape[-1]
    tm = _pick_tile(t, INPROJ_ROWS, BF16_ROWS)
    tn = _pick_tile(ncols, INPROJ_COLS, LANES)
    return pl.pallas_call(
        _inproj_kernel,
        grid=(t // tm, ncols // tn),
        in_specs=[pl.BlockSpec((tm, d), lambda i, j: (i, 0)),
                  pl.BlockSpec((None, d, tn), lambda i, j: (layer, 0, j)),
                  pl.BlockSpec((None, d, SLAB), lambda i, j: (layer, 0, 0))],
        out_specs=[pl.BlockSpec((tm, tn), lambda i, j: (i, j)),
                   pl.BlockSpec((tm, SLAB), lambda i, j: (i, 0))],
        out_shape=[jax.ShapeDtypeStruct((t, ncols), F32), jax.ShapeDtypeStruct((t, SLAB), F32)],
        compiler_params=pltpu.CompilerParams(dimension_semantics=("parallel", "arbitrary"),
                                             vmem_limit_bytes=VMEM_LIMIT),
        name="inproj",
    )(h, w_main, w_slab)


def _scan_call(body, name, proj3, slab3, col_blocks, state_in, conv_in, params, prev_state, out_cols,
               state_block, conv_cols, conv_width, scratch, layer, depth, *, c, v0, v1, seqs_long):
    b, lp, _ = proj3.shape
    nc = lp // c
    nb = _pick_tile(b, seqs_long if nc > 1 else SCAN_SEQS_SHORT, 1)
    vend = v1 - (nc - 1) * c
    hist = conv_width - 1
    assert hist <= vend <= c
    has_state = state_in is not None
    has_prev = prev_state is not None
    zeros = (0,) * len(state_block)
    lyr = lambda bi, i: (layer, 0, 0)
    in_specs = [pl.BlockSpec((nb, c, width), functools.partial(lambda blk, bi, i: (bi, i, blk), blk))
                for width, blk in col_blocks]
    in_specs.append(pl.BlockSpec((nb, c, SLAB), lambda bi, i: (bi, i, 0)))
    args = [proj3] * len(col_blocks) + [slab3]
    if has_state:
        in_specs += [pl.BlockSpec((None, nb) + state_block, lambda bi, i: (layer, bi) + zeros),
                     pl.BlockSpec((None, nb, hist, conv_cols), lambda bi, i: (layer, bi, 0, 0))]
        args += [state_in, conv_in]
    for p in params:
        in_specs.append(pl.BlockSpec((None,) + p.shape[1:], lyr))
        args.append(p)
    aliases = {}
    if has_prev:
        aliases[len(args)] = 1
        in_specs.append(pl.BlockSpec(memory_space=pl.ANY))
        args.append(prev_state)
    return pl.pallas_call(
        functools.partial(body, nb=nb, c=c, v0=v0, v1=v1, vend=vend, has_state=has_state, has_prev=has_prev),
        grid=(b // nb, nc),
        in_specs=in_specs,
        out_specs=[pl.BlockSpec((nb, c, out_cols), lambda bi, i: (bi, i, 0)),
                   pl.BlockSpec((None, nb) + state_block, lambda bi, i: (layer, bi) + zeros),
                   pl.BlockSpec((nb, hist, conv_cols), lambda bi, i: (bi, 0, 0))],
        out_shape=[jax.ShapeDtypeStruct((b, lp, out_cols), BF16),
                   jax.ShapeDtypeStruct((depth, b) + state_block, F32),
                   jax.ShapeDtypeStruct((b, hist, conv_cols), F32)],
        scratch_shapes=[pltpu.VMEM((nb, SUBLANES + c, conv_cols), F32)] + [s(nb) for s in scratch],
        input_output_aliases=aliases,
        compiler_params=pltpu.CompilerParams(dimension_semantics=("parallel", "arbitrary"),
                                             vmem_limit_bytes=VMEM_LIMIT),
        name=name,
    )(*args)


def _conv_step(buf, x_ref, cst_out, w_ref, width, c, vend):
    buf[SUBLANES:SUBLANES + c, :] = x_ref[...]
    out = _causal_conv(buf, w_ref, width, c)
    cst_out[...] = buf[SUBLANES + vend - (width - 1):SUBLANES + vend, :]
    buf[0:SUBLANES, :] = buf[c:c + SUBLANES, :]
    return out


def _conv_init(buf, cst_ref, width, cols):
    buf[0:SUBLANES, :] = jnp.zeros((SUBLANES, cols), F32)
    if cst_ref is not None:
        buf[SUBLANES - (width - 1):SUBLANES, :] = cst_ref[...]


def _pair_expand(arr, p, width, half):
    r = arr.shape[0]
    even = jnp.broadcast_to(arr[:, p:p + 1], (r, width))
    odd = jnp.broadcast_to(arr[:, SSM_PAIRS + p:SSM_PAIRS + p + 1], (r, width))
    return jnp.where(half, even, odd)


def _ssd_kernel(*refs, nb, c, v0, v1, vend, has_state, has_prev):
    refs = list(refs)
    xbc_ref, z_ref, sm_ref = refs[:3]
    k = 3
    st_ref = cst_ref = None
    if has_state:
        st_ref, cst_ref = refs[k:k + 2]
        k += 2
    cw_ref, cb_ref, bias_ref, alog_ref, dexp_ref, nw_ref = refs[k:k + 6]
    k += 6 + (1 if has_prev else 0)
    y_ref, sto_ref, csto_ref, buf = refs[k:k + 4]
    ht = None if has_state else refs[-1]
    ci = pl.program_id(1)

    @pl.when(ci == 0)
    def _():
        for s in range(nb):
            _conv_init(buf.at[s], cst_ref.at[s] if has_state else None, SSM_CONV, SSM_XBC)
            if has_state:
                sto_ref[s] = st_ref[s]
            else:
                ht[s] = jnp.zeros(ht.shape[1:], F32)

    row = lax.broadcasted_iota(jnp.int32, (c, SLAB), 0) + ci * c
    valid = jnp.logical_and(row >= v0, row < v1)
    neg_a = -jnp.exp(alog_ref[...])
    w2 = 2 * c
    lane2 = lax.broadcasted_iota(jnp.int32, (c, w2), 1)
    row2 = lax.broadcasted_iota(jnp.int32, (c, w2), 0)
    half2 = lane2 < c
    j2 = jnp.where(half2, lane2, lane2 - c)
    causal2 = j2 <= row2
    upto2 = row2 <= j2
    eye2 = j2 == row2
    half_p = lax.broadcasted_iota(jnp.int32, (c, LANES), 1) < SSM_HEAD_DIM
    even_rows = lax.broadcasted_iota(jnp.int32, (LANES, SSM_STATE), 0) < SSM_HEAD_DIM

    seqs, groups, pairs = range(nb), range(SSM_GROUPS), range(SSM_PAIRS)
    gcols = [slice(g * GROUP_COLS, (g + 1) * GROUP_COLS) for g in groups]
    pcols = [slice(p * LANES, (p + 1) * LANES) for p in pairs]
    xbc = [_silu_of_half(_conv_step(buf.at[s], xbc_ref.at[s], csto_ref.at[s], cw_ref, SSM_CONV, c, vend)
                         + cb_ref[...]) for s in seqs]
    dt = [jnp.where(valid, _softplus(sm_ref[s] + bias_ref[...]), 0.0) for s in seqs]
    acum = [_cumsum_rows(dt[s] * neg_a) for s in seqs]
    b_g = [[xbc[s][:, D_INNER + g * SSM_STATE:D_INNER + (g + 1) * SSM_STATE] for g in groups] for s in seqs]
    c_g = [[xbc[s][:, D_INNER + SSM_BC + g * SSM_STATE:D_INNER + SSM_BC + (g + 1) * SSM_STATE] for g in groups]
           for s in seqs]
    cb = [[_dot_nt(c_g[s][g], b_g[s][g]) for g in groups] for s in seqs]
    y_off = [[_dot_nt(c_g[s][g], sto_ref[s, gcols[g], :]) if has_state else _dot(c_g[s][g], ht[s, :, gcols[g]])
              for g in groups] for s in seqs]

    if w2 == LANES:
        by_head = [jnp.concatenate([acum[s], dt[s]], axis=0).T for s in seqs]
        by_head_rot = [pltpu.roll(t, c, axis=1) for t in by_head]

    def intra(s, p):
        g = p // PAIRS_PER_GROUP
        dx = _pair_expand(dt[s], p, w2, half2)
        if w2 == LANES:
            odd = SSM_PAIRS + p
            row_cum = jnp.where(half2[0:1], by_head[s][p:p + 1, :], by_head_rot[s][odd:odd + 1, :])
            dt_row = jnp.where(half2[0:1], by_head_rot[s][p:p + 1, :], by_head[s][odd:odd + 1, :])
        else:
            ax = dx * _pair_expand(neg_a, p, w2, half2[0:1])
            row_cum = jnp.sum(jnp.where(upto2, ax, 0.0), axis=0, keepdims=True)
            dt_row = jnp.sum(jnp.where(eye2, dx, 0.0), axis=0, keepdims=True)
        col_cum = _pair_expand(acum[s], p, w2, half2)
        decay = jnp.where(causal2, jnp.exp(col_cum - row_cum), 0.0)
        wgt = decay * jnp.concatenate([cb[s][g], cb[s][g]], axis=1) * dt_row
        xp = xbc[s][:, pcols[p]]
        rhs = jnp.concatenate([jnp.where(half_p, xp, 0.0), jnp.where(half_p, 0.0, xp)], axis=0)
        if w2 == LANES:
            acum_p, dt_p = col_cum, dx
        else:
            acum_p, dt_p = _pair_expand(acum[s], p, LANES, half_p), _pair_expand(dt[s], p, LANES, half_p)
        return _dot(wgt, rhs), acum_p, dt_p

    parts = [[intra(s, p) for p in pairs] for s in seqs]

    def pair_outputs(s, p):
        y_in, acum_p, dt_p = parts[s][p]
        g, q = divmod(p, PAIRS_PER_GROUP)
        xp = xbc[s][:, pcols[p]]
        last_p = acum_p[c - 1:c, :]
        y = y_in + y_off[s][g][:, q * LANES:(q + 1) * LANES] * jnp.exp(acum_p) + dexp_ref[:, pcols[p]] * xp
        return y, xp * (jnp.exp(last_p - acum_p) * dt_p), jnp.exp(last_p)

    outs = [[pair_outputs(s, p) for p in pairs] for s in seqs]
    in_group = lambda s, g, k: [outs[s][p][k] for p in range(g * PAIRS_PER_GROUP, (g + 1) * PAIRS_PER_GROUP)]
    xw = [[jnp.concatenate(in_group(s, g, 1), axis=1) for g in groups] for s in seqs]
    upd = [[_dot_tn(xw[s][g], b_g[s][g]) if has_state else _dot_tn(b_g[s][g], xw[s][g]) for g in groups]
           for s in seqs]
    for s in seqs:
        e_last = jnp.exp(acum[s][c - 1:c, :])
        for g in groups:
            if has_state:
                scale = jnp.concatenate(
                    [jnp.where(even_rows,
                               jnp.broadcast_to(e_last[:, p:p + 1], (LANES, SSM_STATE)),
                               jnp.broadcast_to(e_last[:, SSM_PAIRS + p:SSM_PAIRS + p + 1], (LANES, SSM_STATE)))
                     for p in range(g * PAIRS_PER_GROUP, (g + 1) * PAIRS_PER_GROUP)], axis=0)
                sto_ref[s, gcols[g], :] = sto_ref[s, gcols[g], :] * scale + upd[s][g]
            else:
                ht[s, :, gcols[g]] = ht[s, :, gcols[g]] * jnp.concatenate(in_group(s, g, 2), axis=1) + upd[s][g]
            yz = jnp.concatenate(in_group(s, g, 0), axis=1) * _silu(z_ref[s, :, gcols[g]])
            y_ref[s, :, gcols[g]] = _rms(yz, nw_ref[:, gcols[g]]).astype(y_ref.dtype)

    if not has_state:
        @pl.when(ci == pl.num_programs(1) - 1)
        def _():
            for s in range(nb):
                for p in range(SSM_PAIRS):
                    sto_ref[s, p * LANES:(p + 1) * LANES, :] = ht[s, :, p * LANES:(p + 1) * LANES].T


def _ssd(proj3, slab3, st, cst, prev, prm, layer, depth, *, c, v0, v1):
    params = [prm["ssm_conv_w"], prm["ssm_conv_b"], prm["slab_bias"], prm["slab_alog"], prm["ssm_d_cols"],
              prm["ssm_norm_w"]]
    return _scan_call(_ssd_kernel, "ssd_scan", proj3, slab3, [(SSM_XBC, 0), (D_INNER, Z_BLOCK)], st, cst, params,
                      prev, D_INNER, (D_INNER, SSM_STATE), SSM_XBC, SSM_CONV,
                      [] if st is not None else [lambda nb: pltpu.VMEM((nb, SSM_STATE, D_INNER), F32)],
                      layer, depth, c=c, v0=v0, v1=v1, seqs_long=SSD_SEQS_LONG)


def _head_products(x, y, c):
    per_op = min(GDN_HEADS, MXU_DIM // c)
    width = per_op * c
    lane = lax.broadcasted_iota(jnp.int32, (c, width), 1)
    outs = []
    head = lane // c if per_op > 1 else None
    for s in range(GDN_HEADS // per_op):
        ys = y[:, s * width:(s + 1) * width]
        diag = ys if per_op == 1 else jnp.concatenate(
            [jnp.where(head == r, ys, 0.0) for r in range(per_op)], axis=0)
        outs.append(_dot(x[:, s * width:(s + 1) * width], diag))
    return outs[0] if len(outs) == 1 else jnp.concatenate(outs, axis=1)


def _unit_lower_inverse(mats, eye, same_half, c):
    blk = min(c, INVERSE_BLOCK)
    assert c in (blk, 2 * blk)
    ns = [-(a if blk == c else jnp.where(same_half, a, 0.0)) for a in mats]
    ts = [eye + n for n in ns]
    ms = [_head_products(n, n, c) for n in ns]
    covered = 2
    while 2 * covered < blk:
        boths = [_head_products(jnp.concatenate([t, m], axis=0), m, c) for t, m in zip(ts, ms)]
        ts = [t + both[0:c] for t, both in zip(ts, boths)]
        ms = [both[c:2 * c] for both in boths]
        covered *= 2
    ds = [t + _head_products(t, m, c) for t, m in zip(ts, ms)]
    if blk == c:
        return ds
    firsts = [_head_products(d, jnp.where(same_half, 0.0, a), c) for d, a in zip(ds, mats)]
    return [d - _head_products(f, d, c) for d, f in zip(ds, firsts)]


def _gdn_kernel(*refs, nb, c, v0, v1, vend, has_state, has_prev):
    refs = list(refs)
    qkv_ref, gate_ref, sm_ref = refs[:3]
    k = 3
    st_ref = cst_ref = None
    if has_state:
        st_ref, cst_ref = refs[k:k + 2]
        k += 2
    cw_ref, bias_ref, alog_ref, nw_ref = refs[k:k + 4]
    k += 4 + (1 if has_prev else 0)
    y_ref, sto_ref, csto_ref, buf = refs[k:]
    ci = pl.program_id(1)

    @pl.when(ci == 0)
    def _():
        for s in range(nb):
            _conv_init(buf.at[s], cst_ref.at[s] if has_state else None, GDN_CONV, GDN_QKV)
            sto_ref[s] = st_ref[s] if has_state else jnp.zeros(sto_ref.shape[1:], F32)

    row = lax.broadcasted_iota(jnp.int32, (c, SLAB), 0) + ci * c
    valid = jnp.logical_and(row >= v0, row < v1)
    neg_a = -jnp.exp(alog_ref[...])
    w8 = GDN_HEADS * c
    lane8 = lax.broadcasted_iota(jnp.int32, (c, w8), 1)
    row8 = lax.broadcasted_iota(jnp.int32, (c, w8), 0)
    j8 = lane8 & (c - 1)
    causal8 = j8 <= row8
    strict8 = j8 < row8
    upto8 = row8 <= j8
    eye8 = jnp.where(j8 == row8, 1.0, 0.0)
    same_half8 = (j8 // INVERSE_BLOCK) == (row8 // INVERSE_BLOCK)
    lane_k =lax.broadcasted_iota(jnp.int32, (c, 2 * GDN_DK), 1) < GDN_DK

    def heads_on_lanes(arr):
        return jnp.concatenate(
            [jnp.broadcast_to(arr[:, SLAB_GA + h:SLAB_GA + h + 1], (c, c)) for h in range(GDN_HEADS)], axis=1)

    seqs, heads = range(nb), range(GDN_HEADS)
    col_ga = lambda arr, h: arr[:, SLAB_GA + h:SLAB_GA + h + 1]
    col_gb = lambda arr, h: arr[:, SLAB_GB + h:SLAB_GB + h + 1]
    qkv = [_silu_of_half(_conv_step(buf.at[s], qkv_ref.at[s], csto_ref.at[s], cw_ref, GDN_CONV, c, vend))
           for s in seqs]
    g = [jnp.where(valid, neg_a * _softplus(sm_ref[s] + bias_ref[...]), 0.0) for s in seqs]
    beta = [jnp.where(valid, _sigmoid(sm_ref[s]), 0.0) for s in seqs]
    gcum = [_cumsum_rows(g[s]) for s in seqs]
    e_g = [jnp.exp(gcum[s]) for s in seqs]
    e_last = [jnp.exp(gcum[s][c - 1:c, :]) for s in seqs]
    k_dec = [jnp.exp(gcum[s][c - 1:c, :] - gcum[s]) for s in seqs]
    if 2 * c == LANES:
        first_half = lax.broadcasted_iota(jnp.int32, (1, LANES), 1) < c

        def row_cumsum(s):
            by_head = jnp.concatenate([gcum[s], gcum[s]], axis=0).T
            return jnp.concatenate(
                [jnp.where(first_half, by_head[SLAB_GA + h:SLAB_GA + h + 1, :], by_head[SLAB_GA + h + 1:SLAB_GA + h + 2, :])
                 for h in range(0, GDN_HEADS, 2)], axis=1)

        row_cum = [row_cumsum(s) for s in seqs]
    else:
        row_cum = [jnp.sum(jnp.where(upto8, heads_on_lanes(g[s]), 0.0), axis=0, keepdims=True) for s in seqs]
    decay = [jnp.where(causal8, jnp.exp(heads_on_lanes(gcum[s]) - row_cum[s]), 0.0) for s in seqs]

    def unit(x):
        return x * lax.rsqrt(jnp.sum(x * x, axis=-1, keepdims=True) + EPS)

    qs = [[unit(qkv[s][:, h * GDN_DK:(h + 1) * GDN_DK]) * (GDN_DK ** -0.5) for h in heads] for s in seqs]
    ks = [[unit(qkv[s][:, GDN_QK + h * GDN_DK:GDN_QK + (h + 1) * GDN_DK]) for h in heads] for s in seqs]
    kbs = [[ks[s][h] * col_gb(beta[s], h) for h in heads] for s in seqs]

    def pair_scores(s, hp):
        k2 = jnp.concatenate([ks[s][2 * hp], ks[s][2 * hp + 1]], axis=1)
        rhs_t = jnp.concatenate([jnp.where(lane_k, k2, 0.0), jnp.where(lane_k, 0.0, k2)], axis=0)
        lhs = jnp.concatenate([jnp.concatenate([kbs[s][2 * hp], kbs[s][2 * hp + 1]], axis=1),
                               jnp.concatenate([qs[s][2 * hp], qs[s][2 * hp + 1]], axis=1)], axis=0)
        return _dot_nt(lhs, rhs_t)

    scores = [[pair_scores(s, hp) for hp in range(GDN_HEADS // 2)] for s in seqs]
    a_mat = [jnp.where(strict8, jnp.concatenate([r[0:c, :] for r in scores[s]], axis=1) * decay[s], 0.0) for s in seqs]
    attn = [jnp.concatenate([r[c:2 * c, :] for r in scores[s]], axis=1) * decay[s] for s in seqs]
    t_mat = _unit_lower_inverse(a_mat, eye8, same_half8, c)

    def solve(s, h):
        vh = qkv[s][:, 2 * GDN_QK + h * GDN_DV:2 * GDN_QK + (h + 1) * GDN_DV]
        rhs = jnp.concatenate([vh * col_gb(beta[s], h), kbs[s][h] * col_ga(e_g[s], h)], axis=1)
        return _dot(t_mat[s][:, h * c:(h + 1) * c], rhs)

    sol = [[solve(s, h) for h in heads] for s in seqs]

    def chunk_products(s, h):
        attn_h = attn[s][:, h * c:(h + 1) * c]
        k_dec_h = ks[s][h] * col_ga(k_dec[s], h)
        if c % LANES == 0:
            both = _dot(jnp.concatenate([attn_h, k_dec_h.T], axis=0), sol[s][h])
            return both[0:c], both[c:]
        return _dot(attn_h, sol[s][h]), _dot_tn(k_dec_h, sol[s][h])

    prods = [[chunk_products(s, h) for h in heads] for s in seqs]

    def through_state(s, h):
        au_aw, ku_kw = prods[s][h]
        q_eff = qs[s][h] * col_ga(e_g[s], h) - au_aw[:, GDN_DV:]
        return _dot(jnp.concatenate([q_eff, ku_kw[:, GDN_DV:]], axis=0), sto_ref[s, h])

    through = [[through_state(s, h) for h in heads] for s in seqs]
    for s in seqs:
        for h in heads:
            au_aw, ku_kw = prods[s][h]
            o = through[s][h][0:c] + au_aw[:, 0:GDN_DV]
            sto_ref[s, h] = sto_ref[s, h] * col_ga(e_last[s], h) - through[s][h][c:] + ku_kw[:, 0:GDN_DV]
            gt = gate_ref[s, :, h * GDN_DV:(h + 1) * GDN_DV]
            y_ref[s, :, h * GDN_DV:(h + 1) * GDN_DV] = (_rms(o, nw_ref[...]) * _silu(gt)).astype(y_ref.dtype)


def _gdn(proj3, slab3, st, cst, prev, prm, layer, depth, *, c, v0, v1):
    params = [prm["gdn_conv_w"], prm["slab_bias"], prm["slab_alog"], prm["gdn_norm_w"]]
    gd = GDN_HEADS * GDN_DV
    return _scan_call(_gdn_kernel, "gdn_scan", proj3, slab3, [(GDN_QKV, 1), (gd, GATE_BLOCK)],
                      st, cst, params, prev, gd, (GDN_HEADS, GDN_DK, GDN_DV), GDN_QKV, GDN_CONV,
                      [], layer, depth, c=c, v0=v0, v1=v1, seqs_long=GDN_SEQS_LONG)


def _mix_kernel(x_ref, ys_ref, yg_ref, ms_ref, mg_ref, wso_ref, wgo_ref, wo_ref, o_ref):
    y_ssm = jnp.dot(ys_ref[...], wso_ref[...], preferred_element_type=F32)
    y_gdn = jnp.dot(yg_ref[...], wgo_ref[...], preferred_element_type=F32)
    merged = _sigmoid(ms_ref[...]) * y_ssm + _sigmoid(mg_ref[...]) * y_gdn
    o_ref[...] = x_ref[...] + _dot(merged, wo_ref[...])


def _mix(x, y_ssm, y_gdn, proj, prm, layer):
    t, d = x.shape
    tm = _pick_tile(t, MIX_ROWS, BF16_ROWS)
    lyr = lambda i: (layer, 0, 0)
    return pl.pallas_call(
        _mix_kernel,
        grid=(t // tm,),
        in_specs=[pl.BlockSpec((tm, d), lambda i: (i, 0)),
                  pl.BlockSpec((tm, D_INNER), lambda i: (i, 0)),
                  pl.BlockSpec((tm, GDN_HEADS * GDN_DV), lambda i: (i, 0)),
                  pl.BlockSpec((tm, d), lambda i: (i, GATE_BLOCK + 1)),
                  pl.BlockSpec((tm, d), lambda i: (i, GATE_BLOCK + 2)),
                  pl.BlockSpec((None, D_INNER, d), lyr),
                  pl.BlockSpec((None, GDN_HEADS * GDN_DV, d), lyr),
                  pl.BlockSpec((None, d, d), lyr)],
        out_specs=pl.BlockSpec((tm, d), lambda i: (i, 0)),
        out_shape=jax.ShapeDtypeStruct((t, d), F32),
        compiler_params=pltpu.CompilerParams(dimension_semantics=("parallel",), vmem_limit_bytes=VMEM_LIMIT),
        name="mix_out",
    )(x, y_ssm, y_gdn, proj, proj, prm["w_ssm_out"], prm["w_gdn_out"], prm["w_o"])


def _ffn_kernel(*refs, nb, rows, vend, has_state):
    if has_state:
        (x_ref, cst_ref, nw_ref, wup_ref, cw_ref, cb_ref, wdn_ref, nnw_ref, xo_ref, hn_ref, csto_ref, buf) = refs
    else:
        (x_ref, nw_ref, wup_ref, cw_ref, cb_ref, wdn_ref, nnw_ref, xo_ref, hn_ref, csto_ref, buf) = refs
    ti = pl.program_id(1)
    last_tile = pl.num_programs(1) - 1
    hist = FFN_CONV - 1

    @pl.when(ti == 0)
    def _():
        buf[:, 0:SUBLANES, :] = jnp.zeros((nb, SUBLANES, D_FF), F32)
        if has_state:
            buf[:, SUBLANES - hist:SUBLANES, :] = cst_ref[...]

    x = x_ref[...]
    gu = _dot(_rms(x, nw_ref[...]), wup_ref[...])
    buf[:, SUBLANES:SUBLANES + rows, :] = gu[:, 0:D_FF].reshape(nb, rows, D_FF)
    conv = _causal_conv(buf.at[0] if nb == 1 else buf, cw_ref, FFN_CONV, rows)

    @pl.when(ti == last_tile)
    def _():
        csto_ref[...] = buf[:, SUBLANES + vend - hist:SUBLANES + vend, :]

    buf[:, 0:SUBLANES, :] = buf[:, rows:rows + SUBLANES, :]
    act = _silu_of_half(conv + cb_ref[...]).reshape(nb * rows, D_FF) * gu[:, D_FF:]
    out = x + _dot(act, wdn_ref[...])
    xo_ref[...] = out
    hn_ref[...] = _rms(out, nnw_ref[...]).astype(hn_ref.dtype)


def _ffn(x, cst, prm, layer, next_norm_w, next_layer, hn_dtype, *, b, lp, v1):
    t, d = x.shape
    has_state = cst is not None
    if lp <= FFN_SHORT_ROWS // 4:
        rows, nb = lp, _pick_tile(b, FFN_SHORT_ROWS // lp, 1)
    else:
        rows, nb = _pick_tile(lp, FFN_ROWS, BF16_ROWS), 1
    nt = lp // rows
    vend = v1 - (nt - 1) * rows
    assert FFN_CONV - 1 <= vend <= rows
    lyr = lambda bi, i: (layer, 0, 0)
    flat = lambda bi, i: (bi * nt + i, 0)
    per_b = lambda bi, i: (bi, 0, 0)
    in_specs = [pl.BlockSpec((nb * rows, d), flat)]
    args = [x]
    if has_state:
        in_specs.append(pl.BlockSpec((None, nb, FFN_CONV - 1, D_FF), lambda bi, i: (layer, bi, 0, 0)))
        args.append(cst)
    in_specs += [pl.BlockSpec((None, 1, d), lyr),
                 pl.BlockSpec((None, d, 2 * D_FF), lyr, pipeline_mode=pl.Buffered(1)),
                 pl.BlockSpec((None, FFN_CONV, D_FF), lyr),
                 pl.BlockSpec((None, 1, D_FF), lyr),
                 pl.BlockSpec((None, D_FF, d), lyr, pipeline_mode=pl.Buffered(1)),
                 pl.BlockSpec((None, 1, d), lambda bi, i: (next_layer, 0, 0))]
    args += [prm["norm_ffn_w"], prm["w_up"], prm["ffn_conv_w"], prm["ffn_conv_b"], prm["w_down"], next_norm_w]
    return pl.pallas_call(
        functools.partial(_ffn_kernel, nb=nb, rows=rows, vend=vend, has_state=has_state),
        grid=(b // nb, nt),
        in_specs=in_specs,
        out_specs=[pl.BlockSpec((nb * rows, d), flat),
                   pl.BlockSpec((nb * rows, d), flat),
                   pl.BlockSpec((nb, FFN_CONV - 1, D_FF), per_b)],
        out_shape=[jax.ShapeDtypeStruct((t, d), F32),
                   jax.ShapeDtypeStruct((t, d), hn_dtype),
                   jax.ShapeDtypeStruct((b, FFN_CONV - 1, D_FF), F32)],
        scratch_shapes=[pltpu.VMEM((nb, SUBLANES + rows, D_FF), F32)],
        compiler_params=pltpu.CompilerParams(dimension_semantics=("parallel", "arbitrary"),
                                             vmem_limit_bytes=VMEM_LIMIT),
        name="conv_ffn",
    )(*args)


def _prepare_params(norm_mix_w, w_in, ssm_conv_w, ssm_conv_b, ssm_dt_bias, ssm_a_log, ssm_d, ssm_norm_w,
                    gdn_conv_w, gdn_dt_bias, gdn_a_log, gdn_norm_w, w_ssm_out, w_gdn_out, w_o,
                    norm_ffn_w, w_up, ffn_conv_w, ffn_conv_b, w_down, norm_f_w):
    depth = w_in.shape[0]
    offs = [0]
    for s in IN_SIZES:
        offs.append(offs[-1] + s)
    col = lambda k: w_in[:, :, offs[k]:offs[k + 1]]
    w_main = jnp.concatenate([col(1), col(3), col(0), col(6), col(7), col(8)], axis=-1).astype(BF16)
    dt_cols = col(2)
    pad = jnp.zeros((depth, w_in.shape[1], SLAB - SSM_HEADS - 2 * GDN_HEADS), w_in.dtype)
    w_slab = jnp.concatenate([dt_cols[..., 0::2], dt_cols[..., 1::2], col(4), col(5), pad], axis=-1).astype(BF16)

    def slab(ssm_vec, gdn_vec):
        z = jnp.zeros((depth, SLAB - SSM_HEADS - GDN_HEADS), F32)
        return jnp.concatenate([ssm_vec[:, 0::2], ssm_vec[:, 1::2], gdn_vec, z], axis=-1)[:, None, :]

    return {
        "norm_mix_w": norm_mix_w[:, None, :],
        "w_main": w_main,
        "w_slab": w_slab,
        "ssm_conv_w": 0.5 * ssm_conv_w,
        "ssm_conv_b": 0.5 * ssm_conv_b[:, None, :],
        "slab_bias": slab(ssm_dt_bias, gdn_dt_bias),
        "slab_alog": slab(ssm_a_log, gdn_a_log),
        "ssm_d_cols": jnp.repeat(ssm_d, SSM_HEAD_DIM, axis=-1)[:, None, :],
        "ssm_norm_w": ssm_norm_w[:, None, :],
        "gdn_conv_w": 0.5 * gdn_conv_w,
        "gdn_norm_w": gdn_norm_w[:, None, :],
        "w_ssm_out": w_ssm_out.astype(BF16),
        "w_gdn_out": w_gdn_out.astype(BF16),
        "w_o": w_o.astype(BF16),
        "norm_ffn_w": norm_ffn_w[:, None, :],
        "w_up": w_up.astype(BF16),
        "ffn_conv_w": 0.5 * ffn_conv_w,
        "ffn_conv_b": 0.5 * ffn_conv_b[:, None, :],
        "w_down": w_down.astype(BF16),
        "norm_f_w": norm_f_w[None, None, :],
    }


def _run_trunk(x3, states, prm, *, c_ssd, c_gdn, v0, v1):
    b, lp, d = x3.shape
    depth = prm["w_main"].shape[0]
    x = x3.reshape(b * lp, d)
    h = _norm(x, prm["norm_mix_w"], 0, BF16)
    if states is None:
        st_ssm = cst_ssm = st_gdn = cst_gdn = cst_ffn = None
    else:
        st_ssm = states[0].reshape(depth, b, D_INNER, SSM_STATE)
        cst_ssm, st_gdn, cst_gdn, cst_ffn = states[1:]
    o_ssm = o_gdn = None
    conv_outs = [[] for _ in range(3)]
    for l in range(depth):
        proj, slab = _inproj(h, prm["w_main"], prm["w_slab"], l)
        proj3 = proj.reshape(b, lp, PROJ_COLS)
        slab3 = slab.reshape(b, lp, SLAB)
        y_ssm, o_ssm, o_ssm_conv = _ssd(proj3, slab3, st_ssm, cst_ssm, o_ssm, prm, l, depth, c=c_ssd, v0=v0, v1=v1)
        y_gdn, o_gdn, o_gdn_conv = _gdn(proj3, slab3, st_gdn, cst_gdn, o_gdn, prm, l, depth, c=c_gdn, v0=v0, v1=v1)
        x = _mix(x, y_ssm.reshape(b * lp, D_INNER), y_gdn.reshape(b * lp, GDN_HEADS * GDN_DV), proj, prm, l)
        if l + 1 < depth:
            x, h, o_ffn_conv = _ffn(x, cst_ffn, prm, l, prm["norm_mix_w"], l + 1, BF16, b=b, lp=lp, v1=v1)
        else:
            x, h, o_ffn_conv = _ffn(x, cst_ffn, prm, l, prm["norm_f_w"], 0, F32, b=b, lp=lp, v1=v1)
        conv_outs[0].append(o_ssm_conv)
        conv_outs[1].append(o_gdn_conv)
        conv_outs[2].append(o_ffn_conv)
    return (h.reshape(b, lp, d), o_ssm.reshape(depth, b, SSM_HEADS, SSM_HEAD_DIM, SSM_STATE),
            jnp.stack(conv_outs[0]), o_gdn, jnp.stack(conv_outs[1]), jnp.stack(conv_outs[2]))


def kernel(x_prompt, x_sample, state_ssm, state_ssm_conv, state_gdn, state_gdn_conv, state_ffn_conv, meta_tokens, norm_mix_w, w_in, ssm_conv_w, ssm_conv_b, ssm_dt_bias, ssm_a_log, ssm_d, ssm_norm_w, gdn_conv_w, gdn_dt_bias, gdn_a_log, gdn_norm_w, w_ssm_out, w_gdn_out, w_o, norm_ffn_w, w_up, ffn_conv_w, ffn_conv_b, w_down, norm_f_w):
    prm = _prepare_params(norm_mix_w, w_in, ssm_conv_w, ssm_conv_b, ssm_dt_bias, ssm_a_log, ssm_d, ssm_norm_w,
                          gdn_conv_w, gdn_dt_bias, gdn_a_log, gdn_norm_w, w_ssm_out, w_gdn_out, w_o,
                          norm_ffn_w, w_up, ffn_conv_w, ffn_conv_b, w_down, norm_f_w)
    bp, sp, d = x_prompt.shape
    bs, ls, _ = x_sample.shape
    c_p = max(SSD_CHUNK, GDN_CHUNK)
    assert c_p % SSD_CHUNK == 0 and c_p % GDN_CHUNK == 0
    lp = -(-(N_META + sp) // c_p) * c_p
    pad = lp - N_META - sp
    xp = jnp.concatenate([jnp.zeros((bp, pad, d), x_prompt.dtype),
                          jnp.broadcast_to(meta_tokens.astype(x_prompt.dtype), (bp, N_META, d)), x_prompt], axis=1)
    p_out = _run_trunk(xp, None, prm, c_ssd=SSD_CHUNK, c_gdn=GDN_CHUNK, v0=pad, v1=lp)
    y_prompt = p_out[0][:, pad + N_META:]
    c_s = -(-ls // SUBLANES) * SUBLANES
    xs = jnp.concatenate([x_sample, jnp.zeros((bs, c_s - ls, d), x_sample.dtype)], axis=1)
    s_out = _run_trunk(xs, (state_ssm, state_ssm_conv, state_gdn, state_gdn_conv, state_ffn_conv), prm,
                       c_ssd=c_s, c_gdn=c_s, v0=0, v1=ls)
    y_sample = s_out[0][:, :ls]
    return (y_prompt,) + (y_sample,) + p_out[1:] + s_out[1:]
```

```python
import functools

import jax
import jax.numpy as jnp
from jax import lax
from jax.experimental import pallas as pl
from jax.experimental.pallas import tpu as pltpu

F32 = jnp.float32
BF16 = jnp.bfloat16
HIGHEST = lax.Precision.HIGHEST

EPS = 1e-6
N_META = 16
D_MODEL = 1024
SSM_HEADS = 32
SSM_HEAD_DIM = 64
SSM_GROUPS = 4
SSM_STATE = 128
SSM_CONV = 4
D_INNER = SSM_HEADS * SSM_HEAD_DIM
SSM_BC = SSM_GROUPS * SSM_STATE
SSM_XBC = D_INNER + 2 * SSM_BC
SSM_PAIRS = SSM_HEADS // 2
PAIRS_PER_GROUP = SSM_PAIRS // SSM_GROUPS
GROUP_COLS = D_INNER // SSM_GROUPS
GDN_HEADS = 8
GDN_DK = 128
GDN_DV = 128
GDN_CONV = 4
GDN_QK = GDN_HEADS * GDN_DK
GDN_QKV = 2 * GDN_QK + GDN_HEADS * GDN_DV
D_FF = 2816
FFN_CONV = 3
IN_SIZES = (D_INNER, SSM_XBC, SSM_HEADS, GDN_QKV, GDN_HEADS, GDN_HEADS, GDN_HEADS * GDN_DV, D_MODEL, D_MODEL)
PROJ_COLS = SSM_XBC + GDN_QKV + D_INNER + 3 * D_MODEL
Z_BLOCK = (SSM_XBC + GDN_QKV) // D_INNER
GATE_BLOCK = (SSM_XBC + GDN_QKV + D_INNER) // D_MODEL
SLAB = 128
SLAB_GA = SSM_HEADS
SLAB_GB = SSM_HEADS + GDN_HEADS

LANES = 128
SUBLANES = 8
BF16_ROWS = 16
MXU_DIM = 256
VMEM_BYTES = 64 * 1024 * 1024
VMEM_LIMIT = VMEM_BYTES - 8 * 1024 * 1024

SSD_CHUNK = 64
GDN_CHUNK = 64
INVERSE_BLOCK = 64
SSD_SEQS_LONG = 4
GDN_SEQS_LONG = 8
SCAN_SEQS_SHORT = 8
NORM_ROWS = 1024
INPROJ_ROWS = 1152
INPROJ_COLS = 2816
MIX_ROWS = 512
FFN_ROWS = 576
FFN_SHORT_ROWS = 256


def _pick_tile(total, cap, mult):
    best = None
    for t in range(mult, min(total, cap) + 1, mult):
        if total % t == 0:
            best = t
    assert best is not None, (total, cap, mult)
    return best


def _sigmoid(x):
    return 0.5 * jnp.tanh(0.5 * x) + 0.5


def _silu_of_half(half):
    return half * jnp.tanh(half) + half


def _silu(x):
    return _silu_of_half(0.5 * x)


def _softplus(x):
    return jnp.maximum(x, 0.0) + jnp.log1p(jnp.exp(-jnp.abs(x)))


def _rms(x, w):
    return x * lax.rsqrt(jnp.mean(x * x, axis=-1, keepdims=True) + EPS) * w


def _dot(a, b):
    return jnp.dot(a.astype(BF16), b.astype(BF16), preferred_element_type=F32)


def _dot_nt(a, b):
    return lax.dot_general(a.astype(BF16), b.astype(BF16), (((1,), (1,)), ((), ())), preferred_element_type=F32)


def _dot_tn(a, b):
    return lax.dot_general(a.astype(BF16), b.astype(BF16), (((0,), (0,)), ((), ())), preferred_element_type=F32)


def _cumsum_rows(x):
    c = x.shape[0]
    tril = (lax.broadcasted_iota(jnp.int32, (c, c), 0) >= lax.broadcasted_iota(jnp.int32, (c, c), 1)).astype(F32)
    return jnp.dot(tril, x, precision=HIGHEST, preferred_element_type=F32)


def _causal_conv(buf, w_ref, width, rows):
    acc = None
    if len(buf.shape) == 2:
        cols = buf.shape[1]
        tiles = buf[0:SUBLANES + rows, :].reshape(1 + rows // SUBLANES, SUBLANES, cols)
        sublane = lax.broadcasted_iota(jnp.int32, (1, SUBLANES, cols), 1)
        shifted = [tiles[1:]]
        rot = tiles
        for shift in range(1, width):
            rot = pltpu.roll(rot, 1, axis=1)
            shifted.append(jnp.where(sublane < shift, rot[:-1], rot[1:]))
        for j in range(width):
            term = shifted[width - 1 - j] * w_ref[j:j + 1, :]
            acc = term if acc is None else acc + term
        return acc.reshape(rows, cols)
    for j in range(width):
        off = SUBLANES - (width - 1) + j
        term = buf[:, off:off + rows, :] * w_ref[j:j + 1, :]
        acc = term if acc is None else acc + term
    return acc


def _norm_kernel(x_ref, w_ref, o_ref):
    o_ref[...] = _rms(x_ref[...], w_ref[...]).astype(o_ref.dtype)


def _norm(x, w, layer, out_dtype):
    t, d = x.shape
    tm = _pick_tile(t, NORM_ROWS, BF16_ROWS)
    return pl.pallas_call(
        _norm_kernel,
        grid=(t // tm,),
        in_specs=[pl.BlockSpec((tm, d), lambda i: (i, 0)),
                  pl.BlockSpec((None, 1, d), lambda i: (layer, 0, 0))],
        out_specs=pl.BlockSpec((tm, d), lambda i: (i, 0)),
        out_shape=jax.ShapeDtypeStruct((t, d), out_dtype),
        compiler_params=pltpu.CompilerParams(dimension_semantics=("parallel",)),
        name="rmsnorm",
    )(x, w)


def _inproj_kernel(h_ref, w_ref, ws_ref, o_ref, os_ref):
    h = h_ref[...]
    o_ref[...] = jnp.dot(h, w_ref[...], preferred_element_type=F32)

    @pl.when(pl.program_id(1) == 0)
    def _():
        os_ref[...] = jnp.dot(h, ws_ref[...], preferred_element_type=F32)


def _inproj(h, w_main, w_slab, layer):
    t, d = h.shape
    ncols = w_main.shape[-1]
    tm = _pick_tile(t, INPROJ_ROWS, BF16_ROWS)
    tn = _pick_tile(ncols, INPROJ_COLS, LANES)
    return pl.pallas_call(
        _inproj_kernel,
        grid=(t // tm, ncols // tn),
        in_specs=[pl.BlockSpec((tm, d), lambda i, j: (i, 0)),
                  pl.BlockSpec((None, d, tn), lambda i, j: (layer, 0, j)),
                  pl.BlockSpec((None, d, SLAB), lambda i, j: (layer, 0, 0))],
        out_specs=[pl.BlockSpec((tm, tn), lambda i, j: (i, j)),
                   pl.BlockSpec((tm, SLAB), lambda i, j: (i, 0))],
        out_shape=[jax.ShapeDtypeStruct((t, ncols), F32), jax.ShapeDtypeStruct((t, SLAB), F32)],
        compiler_params=pltpu.CompilerParams(dimension_semantics=("parallel", "arbitrary"),
                                             vmem_limit_bytes=VMEM_LIMIT),
        name="inproj",
    )(h, w_main, w_slab)


def _scan_call(body, name, proj3, slab3, col_blocks, state_in, conv_in, params, prev_state, out_cols,
               state_block, conv_cols, conv_width, scratch, layer, depth, *, c, v0, v1, seqs_long):
    b, lp, _ = proj3.shape
    nc = lp // c
    nb = _pick_tile(b, seqs_long if nc > 1 else SCAN_SEQS_SHORT, 1)
    vend = v1 - (nc - 1) * c
    hist = conv_width - 1
    assert hist <= vend <= c
    has_state = state_in is not None
    has_prev = prev_state is not None
    zeros = (0,) * len(state_block)
    lyr = lambda bi, i: (layer, 0, 0)
    in_specs = [pl.BlockSpec((nb, c, width), functools.partial(lambda blk, bi, i: (bi, i, blk), blk))
                for width, blk in col_blocks]
    in_specs.append(pl.BlockSpec((nb, c, SLAB), lambda bi, i: (bi, i, 0)))
    args = [proj3] * len(col_blocks) + [slab3]
    if has_state:
        in_specs += [pl.BlockSpec((None, nb) + state_block, lambda bi, i: (layer, bi) + zeros),
                     pl.BlockSpec((None, nb, hist, conv_cols), lambda bi, i: (layer, bi, 0, 0))]
        args += [state_in, conv_in]
    for p in params:
        in_specs.append(pl.BlockSpec((None,) + p.shape[1:], lyr))
        args.append(p)
    aliases = {}
    if has_prev:
        aliases[len(args)] = 1
        in_specs.append(pl.BlockSpec(memory_space=pl.ANY))
        args.append(prev_state)
    return pl.pallas_call(
        functools.partial(body, nb=nb, c=c, v0=v0, v1=v1, vend=vend, has_state=has_state, has_prev=has_prev),
        grid=(b // nb, nc),
        in_specs=in_specs,
        out_specs=[pl.BlockSpec((nb, c, out_cols), lambda bi, i: (bi, i, 0)),
                   pl.BlockSpec((None, nb) + state_block, lambda bi, i: (layer, bi) + zeros),
                   pl.BlockSpec((nb, hist, conv_cols), lambda bi, i: (bi, 0, 0))],
        out_shape=[jax.ShapeDtypeStruct((b, lp, out_cols), BF16),
                   jax.ShapeDtypeStruct((depth, b) + state_block, F32),
                   jax.ShapeDtypeStruct((b, hist, conv_cols), F32)],
        scratch_shapes=[pltpu.VMEM((nb, SUBLANES + c, conv_cols), F32)] + [s(nb) for s in scratch],
        input_output_aliases=aliases,
        compiler_params=pltpu.CompilerParams(dimension_semantics=("parallel", "arbitrary"),
                                             vmem_limit_bytes=VMEM_LIMIT),
        name=name,
    )(*args)


def _conv_step(buf, x_ref, cst_out, w_ref, width, c, vend):
    buf[SUBLANES:SUBLANES + c, :] = x_ref[...]
    out = _causal_conv(buf, w_ref, width, c)
    cst_out[...] = buf[SUBLANES + vend - (width - 1):SUBLANES + vend, :]
    buf[0:SUBLANES, :] = buf[c:c + SUBLANES, :]
    return out


def _conv_init(buf, cst_ref, width, cols):
    buf[0:SUBLANES, :] = jnp.zeros((SUBLANES, cols), F32)
    if cst_ref is not None:
        buf[SUBLANES - (width - 1):SUBLANES, :] = cst_ref[...]


def _pair_expand(arr, p, width, half):
    r = arr.shape[0]
    even = jnp.broadcast_to(arr[:, p:p + 1], (r, width))
    odd = jnp.broadcast_to(arr[:, SSM_PAIRS + p:SSM_PAIRS + p + 1], (r, width))
    return jnp.where(half, even, odd)


def _ssd_kernel(*refs, nb, c, v0, v1, vend, has_state, has_prev):
    refs = list(refs)
    xbc_ref, z_ref, sm_ref = refs[:3]
    k = 3
    st_ref = cst_ref = None
    if has_state:
        st_ref, cst_ref = refs[k:k + 2]
        k += 2
    cw_ref, cb_ref, bias_ref, alog_ref, dexp_ref, nw_ref = refs[k:k + 6]
    k += 6 + (1 if has_prev else 0)
    y_ref, sto_ref, csto_ref, buf = refs[k:k + 4]
    ht = None if has_state else refs[-1]
    ci = pl.program_id(1)

    @pl.when(ci == 0)
    def _():
        for s in range(nb):
            _conv_init(buf.at[s], cst_ref.at[s] if has_state else None, SSM_CONV, SSM_XBC)
            if has_state:
                sto_ref[s] = st_ref[s]
            else:
                ht[s] = jnp.zeros(ht.shape[1:], F32)

    row = lax.broadcasted_iota(jnp.int32, (c, SLAB), 0) + ci * c
    valid = jnp.logical_and(row >= v0, row < v1)
    neg_a = -jnp.exp(alog_ref[...])
    w2 = 2 * c
    lane2 = lax.broadcasted_iota(jnp.int32, (c, w2), 1)
    row2 = lax.broadcasted_iota(jnp.int32, (c, w2), 0)
    half2 = lane2 < c
    j2 = jnp.where(half2, lane2, lane2 - c)
    causal2 = j2 <= row2
    upto2 = row2 <= j2
    eye2 = j2 == row2
    half_p = lax.broadcasted_iota(jnp.int32, (c, LANES), 1) < SSM_HEAD_DIM
    even_rows = lax.broadcasted_iota(jnp.int32, (LANES, SSM_STATE), 0) < SSM_HEAD_DIM

    seqs, groups, pairs = range(nb), range(SSM_GROUPS), range(SSM_PAIRS)
    gcols = [slice(g * GROUP_COLS, (g + 1) * GROUP_COLS) for g in groups]
    pcols = [slice(p * LANES, (p + 1) * LANES) for p in pairs]
    xbc = [_silu_of_half(_conv_step(buf.at[s], xbc_ref.at[s], csto_ref.at[s], cw_ref, SSM_CONV, c, vend)
                         + cb_ref[...]) for s in seqs]
    dt = [jnp.where(valid, _softplus(sm_ref[s] + bias_ref[...]), 0.0) for s in seqs]
    acum = [_cumsum_rows(dt[s] * neg_a) for s in seqs]
    b_g = [[xbc[s][:, D_INNER + g * SSM_STATE:D_INNER + (g + 1) * SSM_STATE] for g in groups] for s in seqs]
    c_g = [[xbc[s][:, D_INNER + SSM_BC + g * SSM_STATE:D_INNER + SSM_BC + (g + 1) * SSM_STATE] for g in groups]
           for s in seqs]
    cb = [[_dot_nt(c_g[s][g], b_g[s][g]) for g in groups] for s in seqs]
    y_off = [[_dot_nt(c_g[s][g], sto_ref[s, gcols[g], :]) if has_state else _dot(c_g[s][g], ht[s, :, gcols[g]])
              for g in groups] for s in seqs]

    if w2 == LANES:
        by_head = [jnp.concatenate([acum[s], dt[s]], axis=0).T for s in seqs]
        by_head_rot = [pltpu.roll(t, c, axis=1) for t in by_head]

    def intra(s, p):
        g = p // PAIRS_PER_GROUP
        dx = _pair_expand(dt[s], p, w2, half2)
        if w2 == LANES:
            odd = SSM_PAIRS + p
            row_cum = jnp.where(half2[0:1], by_head[s][p:p + 1, :], by_head_rot[s][odd:odd + 1, :])
            dt_row = jnp.where(half2[0:1], by_head_rot[s][p:p + 1, :], by_head[s][odd:odd + 1, :])
        else:
            ax = dx * _pair_expand(neg_a, p, w2, half2[0:1])
            row_cum = jnp.sum(jnp.where(upto2, ax, 0.0), axis=0, keepdims=True)
            dt_row = jnp.sum(jnp.where(eye2, dx, 0.0), axis=0, keepdims=True)
        col_cum = _pair_expand(acum[s], p, w2, half2)
        decay = jnp.where(causal2, jnp.exp(col_cum - row_cum), 0.0)
        wgt = decay * jnp.concatenate([cb[s][g], cb[s][g]], axis=1) * dt_row
        xp = xbc[s][:, pcols[p]]
        rhs = jnp.concatenate([jnp.where(half_p, xp, 0.0), jnp.where(half_p, 0.0, xp)], axis=0)
        if w2 == LANES:
            acum_p, dt_p = col_cum, dx
        else:
            acum_p, dt_p = _pair_expand(acum[s], p, LANES, half_p), _pair_expand(dt[s], p, LANES, half_p)
        return _dot(wgt, rhs), acum_p, dt_p

    parts = [[intra(s, p) for p in pairs] for s in seqs]

    def pair_outputs(s, p):
        y_in, acum_p, dt_p = parts[s][p]
        g, q = divmod(p, PAIRS_PER_GROUP)
        xp = xbc[s][:, pcols[p]]
        last_p = acum_p[c - 1:c, :]
        y = y_in + y_off[s][g][:, q * LANES:(q + 1) * LANES] * jnp.exp(acum_p) + dexp_ref[:, pcols[p]] * xp
        return y, xp * (jnp.exp(last_p - acum_p) * dt_p), jnp.exp(last_p)

    outs = [[pair_outputs(s, p) for p in pairs] for s in seqs]
    in_group = lambda s, g, k: [outs[s][p][k] for p in range(g * PAIRS_PER_GROUP, (g + 1) * PAIRS_PER_GROUP)]
    xw = [[jnp.concatenate(in_group(s, g, 1), axis=1) for g in groups] for s in seqs]
    upd = [[_dot_tn(xw[s][g], b_g[s][g]) if has_state else _dot_tn(b_g[s][g], xw[s][g]) for g in groups]
           for s in seqs]
    for s in seqs:
        e_last = jnp.exp(acum[s][c - 1:c, :])
        for g in groups:
            if has_state:
                scale = jnp.concatenate(
                    [jnp.where(even_rows,
                               jnp.broadcast_to(e_last[:, p:p + 1], (LANES, SSM_STATE)),
                               jnp.broadcast_to(e_last[:, SSM_PAIRS + p:SSM_PAIRS + p + 1], (LANES, SSM_STATE)))
                     for p in range(g * PAIRS_PER_GROUP, (g + 1) * PAIRS_PER_GROUP)], axis=0)
                sto_ref[s, gcols[g], :] = sto_ref[s, gcols[g], :] * scale + upd[s][g]
            else:
                ht[s, :, gcols[g]] = ht[s, :, gcols[g]] * jnp.concatenate(in_group(s, g, 2), axis=1) + upd[s][g]
            yz = jnp.concatenate(in_group(s, g, 0), axis=1) * _silu(z_ref[s, :, gcols[g]])
            y_ref[s, :, gcols[g]] = _rms(yz, nw_ref[:, gcols[g]]).astype(y_ref.dtype)

    if not has_state:
        @pl.when(ci == pl.num_programs(1) - 1)
        def _():
            for s in range(nb):
                for p in range(SSM_PAIRS):
                    sto_ref[s, p * LANES:(p + 1) * LANES, :] = ht[s, :, p * LANES:(p + 1) * LANES].T


def _ssd(proj3, slab3, st, cst, prev, prm, layer, depth, *, c, v0, v1):
    params = [prm["ssm_conv_w"], prm["ssm_conv_b"], prm["slab_bias"], prm["slab_alog"], prm["ssm_d_cols"],
              prm["ssm_norm_w"]]
    return _scan_call(_ssd_kernel, "ssd_scan", proj3, slab3, [(SSM_XBC, 0), (D_INNER, Z_BLOCK)], st, cst, params,
                      prev, D_INNER, (D_INNER, SSM_STATE), SSM_XBC, SSM_CONV,
                      [] if st is not None else [lambda nb: pltpu.VMEM((nb, SSM_STATE, D_INNER), F32)],
                      layer, depth, c=c, v0=v0, v1=v1, seqs_long=SSD_SEQS_LONG)


def _head_products(x, y, c):
    per_op = min(GDN_HEADS, MXU_DIM // c)
    width = per_op * c
    lane = lax.broadcasted_iota(jnp.int32, (c, width), 1)
    outs = []
    head = lane // c if per_op > 1 else None
    for s in range(GDN_HEADS // per_op):
        ys = y[:, s * width:(s + 1) * width]
        diag = ys if per_op == 1 else jnp.concatenate(
            [jnp.where(head == r, ys, 0.0) for r in range(per_op)], axis=0)
        outs.append(_dot(x[:, s * width:(s + 1) * width], diag))
    return outs[0] if len(outs) == 1 else jnp.concatenate(outs, axis=1)


def _unit_lower_inverse(mats, eye, same_half, c):
    blk = min(c, INVERSE_BLOCK)
    assert c in (blk, 2 * blk)
    ns = [-(a if blk == c else jnp.where(same_half, a, 0.0)) for a in mats]
    ts = [eye + n for n in ns]
    ms = [_head_products(n, n, c) for n in ns]
    covered = 2
    while 2 * covered < blk:
        boths = [_head_products(jnp.concatenate([t, m], axis=0), m, c) for t, m in zip(ts, ms)]
        ts = [t + both[0:c] for t, both in zip(ts, boths)]
        ms = [both[c:2 * c] for both in boths]
        covered *= 2
    ds = [t + _head_products(t, m, c) for t, m in zip(ts, ms)]
    if blk == c:
        return ds
    firsts = [_head_products(d, jnp.where(same_half, 0.0, a), c) for d, a in zip(ds, mats)]
    return [d - _head_products(f, d, c) for d, f in zip(ds, firsts)]


def _gdn_kernel(*refs, nb, c, v0, v1, vend, has_state, has_prev):
    refs = list(refs)
    qkv_ref, gate_ref, sm_ref = refs[:3]
    k = 3
    st_ref = cst_ref = None
    if has_state:
        st_ref, cst_ref = refs[k:k + 2]
        k += 2
    cw_ref, bias_ref, alog_ref, nw_ref = refs[k:k + 4]
    k += 4 + (1 if has_prev else 0)
    y_ref, sto_ref, csto_ref, buf = refs[k:]
    ci = pl.program_id(1)

    @pl.when(ci == 0)
    def _():
        for s in range(nb):
            _conv_init(buf.at[s], cst_ref.at[s] if has_state else None, GDN_CONV, GDN_QKV)
            sto_ref[s] = st_ref[s] if has_state else jnp.zeros(sto_ref.shape[1:], F32)

    row = lax.broadcasted_iota(jnp.int32, (c, SLAB), 0) + ci * c
    valid = jnp.logical_and(row >= v0, row < v1)
    neg_a = -jnp.exp(alog_ref[...])
    w8 = GDN_HEADS * c
    lane8 = lax.broadcasted_iota(jnp.int32, (c, w8), 1)
    row8 = lax.broadcasted_iota(jnp.int32, (c, w8), 0)
    j8 = lane8 & (c - 1)
    causal8 = j8 <= row8
    strict8 = j8 < row8
    upto8 = row8 <= j8
    eye8 = jnp.where(j8 == row8, 1.0, 0.0)
    same_half8 = (j8 // INVERSE_BLOCK) == (row8 // INVERSE_BLOCK)
    lane_k =lax.broadcasted_iota(jnp.int32, (c, 2 * GDN_DK), 1) < GDN_DK

    def heads_on_lanes(arr):
        return jnp.concatenate(
            [jnp.broadcast_to(arr[:, SLAB_GA + h:SLAB_GA + h + 1], (c, c)) for h in range(GDN_HEADS)], axis=1)

    seqs, heads = range(nb), range(GDN_HEADS)
    col_ga = lambda arr, h: arr[:, SLAB_GA + h:SLAB_GA + h + 1]
    col_gb = lambda arr, h: arr[:, SLAB_GB + h:SLAB_GB + h + 1]
    qkv = [_silu_of_half(_conv_step(buf.at[s], qkv_ref.at[s], csto_ref.at[s], cw_ref, GDN_CONV, c, vend))
           for s in seqs]
    g = [jnp.where(valid, neg_a * _softplus(sm_ref[s] + bias_ref[...]), 0.0) for s in seqs]
    beta = [jnp.where(valid, _sigmoid(sm_ref[s]), 0.0) for s in seqs]
    gcum = [_cumsum_rows(g[s]) for s in seqs]
    e_g = [jnp.exp(gcum[s]) for s in seqs]
    e_last = [jnp.exp(gcum[s][c - 1:c, :]) for s in seqs]
    k_dec = [jnp.exp(gcum[s][c - 1:c, :] - gcum[s]) for s in seqs]
    if 2 * c == LANES:
        first_half = lax.broadcasted_iota(jnp.int32, (1, LANES), 1) < c

        def row_cumsum(s):
            by_head = jnp.concatenate([gcum[s], gcum[s]], axis=0).T
            return jnp.concatenate(
                [jnp.where(first_half, by_head[SLAB_GA + h:SLAB_GA + h + 1, :], by_head[SLAB_GA + h + 1:SLAB_GA + h + 2, :])
                 for h in range(0, GDN_HEADS, 2)], axis=1)

        row_cum = [row_cumsum(s) for s in seqs]
    else:
        row_cum = [jnp.sum(jnp.where(upto8, heads_on_lanes(g[s]), 0.0), axis=0, keepdims=True) for s in seqs]
    decay = [jnp.where(causal8, jnp.exp(heads_on_lanes(gcum[s]) - row_cum[s]), 0.0) for s in seqs]

    def unit(x):
        return x * lax.rsqrt(jnp.sum(x * x, axis=-1, keepdims=True) + EPS)

    qs = [[unit(qkv[s][:, h * GDN_DK:(h + 1) * GDN_DK]) * (GDN_DK ** -0.5) for h in heads] for s in seqs]
    ks = [[unit(qkv[s][:, GDN_QK + h * GDN_DK:GDN_QK + (h + 1) * GDN_DK]) for h in heads] for s in seqs]
    kbs = [[ks[s][h] * col_gb(beta[s], h) for h in heads] for s in seqs]

    def pair_scores(s, hp):
        k2 = jnp.concatenate([ks[s][2 * hp], ks[s][2 * hp + 1]], axis=1)
        rhs_t = jnp.concatenate([jnp.where(lane_k, k2, 0.0), jnp.where(lane_k, 0.0, k2)], axis=0)
        lhs = jnp.concatenate([jnp.concatenate([kbs[s][2 * hp], kbs[s][2 * hp + 1]], axis=1),
                               jnp.concatenate([qs[s][2 * hp], qs[s][2 * hp + 1]], axis=1)], axis=0)
        return _dot_nt(lhs, rhs_t)

    scores = [[pair_scores(s, hp) for hp in range(GDN_HEADS // 2)] for s in seqs]
    a_mat = [jnp.where(strict8, jnp.concatenate([r[0:c, :] for r in scores[s]], axis=1) * decay[s], 0.0) for s in seqs]
    attn = [jnp.concatenate([r[c:2 * c, :] for r in scores[s]], axis=1) * decay[s] for s in seqs]
    t_mat = _unit_lower_inverse(a_mat, eye8, same_half8, c)

    def solve(s, h):
        vh = qkv[s][:, 2 * GDN_QK + h * GDN_DV:2 * GDN_QK + (h + 1) * GDN_DV]
        rhs = jnp.concatenate([vh * col_gb(beta[s], h), kbs[s][h] * col_ga(e_g[s], h)], axis=1)
        return _dot(t_mat[s][:, h * c:(h + 1) * c], rhs)

    sol = [[solve(s, h) for h in heads] for s in seqs]

    def chunk_products(s, h):
        attn_h = attn[s][:, h * c:(h + 1) * c]
        k_dec_h = ks[s][h] * col_ga(k_dec[s], h)
        if c % LANES == 0:
            both = _dot(jnp.concatenate([attn_h, k_dec_h.T], axis=0), sol[s][h])
            return both[0:c], both[c:]
        return _dot(attn_h, sol[s][h]), _dot_tn(k_dec_h, sol[s][h])

    prods = [[chunk_products(s, h) for h in heads] for s in seqs]

    def through_state(s, h):
        au_aw, ku_kw = prods[s][h]
        q_eff = qs[s][h] * col_ga(e_g[s], h) - au_aw[:, GDN_DV:]
        return _dot(jnp.concatenate([q_eff, ku_kw[:, GDN_DV:]], axis=0), sto_ref[s, h])

    through = [[through_state(s, h) for h in heads] for s in seqs]
    for s in seqs:
        for h in heads:
            au_aw, ku_kw = prods[s][h]
            o = through[s][h][0:c] + au_aw[:, 0:GDN_DV]
            sto_ref[s, h] = sto_ref[s, h] * col_ga(e_last[s], h) - through[s][h][c:] + ku_kw[:, 0:GDN_DV]
            gt = gate_ref[s, :, h * GDN_DV:(h + 1) * GDN_DV]
            y_ref[s, :, h * GDN_DV:(h + 1) * GDN_DV] = (_rms(o, nw_ref[...]) * _silu(gt)).astype(y_ref.dtype)


def _gdn(proj3, slab3, st, cst, prev, prm, layer, depth, *, c, v0, v1):
    params = [prm["gdn_conv_w"], prm["slab_bias"], prm["slab_alog"], prm["gdn_norm_w"]]
    gd = GDN_HEADS * GDN_DV
    return _scan_call(_gdn_kernel, "gdn_scan", proj3, slab3, [(GDN_QKV, 1), (gd, GATE_BLOCK)],
                      st, cst, params, prev, gd, (GDN_HEADS, GDN_DK, GDN_DV), GDN_QKV, GDN_CONV,
                      [], layer, depth, c=c, v0=v0, v1=v1, seqs_long=GDN_SEQS_LONG)


def _mix_kernel(x_ref, ys_ref, yg_ref, ms_ref, mg_ref, wso_ref, wgo_ref, wo_ref, o_ref):
    y_ssm = jnp.dot(ys_ref[...], wso_ref[...], preferred_element_type=F32)
    y_gdn = jnp.dot(yg_ref[...], wgo_ref[...], preferred_element_type=F32)
    merged = _sigmoid(ms_ref[...]) * y_ssm + _sigmoid(mg_ref[...]) * y_gdn
    o_ref[...] = x_ref[...] + _dot(merged, wo_ref[...])


def _mix(x, y_ssm, y_gdn, proj, prm, layer):
    t, d = x.shape
    tm = _pick_tile(t, MIX_ROWS, BF16_ROWS)
    lyr = lambda i: (layer, 0, 0)
    return pl.pallas_call(
        _mix_kernel,
        grid=(t // tm,),
        in_specs=[pl.BlockSpec((tm, d), lambda i: (i, 0)),
                  pl.BlockSpec((tm, D_INNER), lambda i: (i, 0)),
                  pl.BlockSpec((tm, GDN_HEADS * GDN_DV), lambda i: (i, 0)),
                  pl.BlockSpec((tm, d), lambda i: (i, GATE_BLOCK + 1)),
                  pl.BlockSpec((tm, d), lambda i: (i, GATE_BLOCK + 2)),
                  pl.BlockSpec((None, D_INNER, d), lyr),
                  pl.BlockSpec((None, GDN_HEADS * GDN_DV, d), lyr),
                  pl.BlockSpec((None, d, d), lyr)],
        out_specs=pl.BlockSpec((tm, d), lambda i: (i, 0)),
        out_shape=jax.ShapeDtypeStruct((t, d), F32),
        compiler_params=pltpu.CompilerParams(dimension_semantics=("parallel",), vmem_limit_bytes=VMEM_LIMIT),
        name="mix_out",
    )(x, y_ssm, y_gdn, proj, proj, prm["w_ssm_out"], prm["w_gdn_out"], prm["w_o"])


def _ffn_kernel(*refs, nb, rows, vend, has_state):
    if has_state:
        (x_ref, cst_ref, nw_ref, wup_ref, cw_ref, cb_ref, wdn_ref, nnw_ref, xo_ref, hn_ref, csto_ref, buf) = refs
    else:
        (x_ref, nw_ref, wup_ref, cw_ref, cb_ref, wdn_ref, nnw_ref, xo_ref, hn_ref, csto_ref, buf) = refs
    hist = FFN_CONV - 1

    @pl.when(pl.program_id(1) == 0)
    def _():
        buf[:, 0:SUBLANES, :] = jnp.zeros((nb, SUBLANES, D_FF), F32)
        if has_state:
            buf[:, SUBLANES - hist:SUBLANES, :] = cst_ref[...]

    x = x_ref[...]
    gu = _dot(_rms(x, nw_ref[...]), wup_ref[...])
    buf[:, SUBLANES:SUBLANES + rows, :] = gu[:, 0:D_FF].reshape(nb, rows, D_FF)
    conv = _causal_conv(buf.at[0] if nb == 1 else buf, cw_ref, FFN_CONV, rows)
    csto_ref[...] = buf[:, SUBLANES + vend - hist:SUBLANES + vend, :]
    buf[:, 0:SUBLANES, :] = buf[:, rows:rows + SUBLANES, :]
    act = _silu_of_half(conv + cb_ref[...]).reshape(nb * rows, D_FF) * gu[:, D_FF:]
    out = x + _dot(act, wdn_ref[...])
    xo_ref[...] = out
    hn_ref[...] = _rms(out, nnw_ref[...]).astype(hn_ref.dtype)


def _ffn(x, cst, prm, layer, next_norm_w, next_layer, hn_dtype, *, b, lp, v1):
    t, d = x.shape
    has_state = cst is not None
    if lp <= FFN_SHORT_ROWS // 4:
        rows, nb = lp, _pick_tile(b, FFN_SHORT_ROWS // lp, 1)
    else:
        rows, nb = _pick_tile(lp, FFN_ROWS, BF16_ROWS), 1
    nt = lp // rows
    vend = v1 - (nt - 1) * rows
    assert FFN_CONV - 1 <= vend <= rows
    lyr = lambda bi, i: (layer, 0, 0)
    flat = lambda bi, i: (bi * nt + i, 0)
    per_b = lambda bi, i: (bi, 0, 0)
    in_specs = [pl.BlockSpec((nb * rows, d), flat)]
    args = [x]
    if has_state:
        in_specs.append(pl.BlockSpec((None, nb, FFN_CONV - 1, D_FF), lambda bi, i: (layer, bi, 0, 0)))
        args.append(cst)
    in_specs += [pl.BlockSpec((None, 1, d), lyr),
                 pl.BlockSpec((None, d, 2 * D_FF), lyr, pipeline_mode=pl.Buffered(1)),
                 pl.BlockSpec((None, FFN_CONV, D_FF), lyr),
                 pl.BlockSpec((None, 1, D_FF), lyr),
                 pl.BlockSpec((None, D_FF, d), lyr, pipeline_mode=pl.Buffered(1)),
                 pl.BlockSpec((None, 1, d), lambda bi, i: (next_layer, 0, 0))]
    args += [prm["norm_ffn_w"], prm["w_up"], prm["ffn_conv_w"], prm["ffn_conv_b"], prm["w_down"], next_norm_w]
    return pl.pallas_call(
        functools.partial(_ffn_kernel, nb=nb, rows=rows, vend=vend, has_state=has_state),
        grid=(b // nb, nt),
        in_specs=in_specs,
        out_specs=[pl.BlockSpec((nb * rows, d), flat),
                   pl.BlockSpec((nb * rows, d), flat),
                   pl.BlockSpec((nb, FFN_CONV - 1, D_FF), per_b)],
        out_shape=[jax.ShapeDtypeStruct((t, d), F32),
                   jax.ShapeDtypeStruct((t, d), hn_dtype),
                   jax.ShapeDtypeStruct((b, FFN_CONV - 1, D_FF), F32)],
        scratch_shapes=[pltpu.VMEM((nb, SUBLANES + rows, D_FF), F32)],
        compiler_params=pltpu.CompilerParams(dimension_semantics=("parallel", "arbitrary"),
                                             vmem_limit_bytes=VMEM_LIMIT),
        name="conv_ffn",
    )(*args)


def _prepare_params(norm_mix_w, w_in, ssm_conv_w, ssm_conv_b, ssm_dt_bias, ssm_a_log, ssm_d, ssm_norm_w,
                    gdn_conv_w, gdn_dt_bias, gdn_a_log, gdn_norm_w, w_ssm_out, w_gdn_out, w_o,
                    norm_ffn_w, w_up, ffn_conv_w, ffn_conv_b, w_down, norm_f_w):
    depth = w_in.shape[0]
    offs = [0]
    for s in IN_SIZES:
        offs.append(offs[-1] + s)
    col = lambda k: w_in[:, :, offs[k]:offs[k + 1]]
    w_main = jnp.concatenate([col(1), col(3), col(0), col(6), col(7), col(8)], axis=-1).astype(BF16)
    dt_cols = col(2)
    pad = jnp.zeros((depth, w_in.shape[1], SLAB - SSM_HEADS - 2 * GDN_HEADS), w_in.dtype)
    w_slab = jnp.concatenate([dt_cols[..., 0::2], dt_cols[..., 1::2], col(4), col(5), pad], axis=-1).astype(BF16)

    def slab(ssm_vec, gdn_vec):
        z = jnp.zeros((depth, SLAB - SSM_HEADS - GDN_HEADS), F32)
        return jnp.concatenate([ssm_vec[:, 0::2], ssm_vec[:, 1::2], gdn_vec, z], axis=-1)[:, None, :]

    return {
        "norm_mix_w": norm_mix_w[:, None, :],
        "w_main": w_main,
        "w_slab": w_slab,
        "ssm_conv_w": 0.5 * ssm_conv_w,
        "ssm_conv_b": 0.5 * ssm_conv_b[:, None, :],
        "slab_bias": slab(ssm_dt_bias, gdn_dt_bias),
        "slab_alog": slab(ssm_a_log, gdn_a_log),
        "ssm_d_cols": jnp.repeat(ssm_d, SSM_HEAD_DIM, axis=-1)[:, None, :],
        "ssm_norm_w": ssm_norm_w[:, None, :],
        "gdn_conv_w": 0.5 * gdn_conv_w,
        "gdn_norm_w": gdn_norm_w[:, None, :],
        "w_ssm_out": w_ssm_out.astype(BF16),
        "w_gdn_out": w_gdn_out.astype(BF16),
        "w_o": w_o.astype(BF16),
        "norm_ffn_w": norm_ffn_w[:, None, :],
        "w_up": w_up.astype(BF16),
        "ffn_conv_w": 0.5 * ffn_conv_w,
        "ffn_conv_b": 0.5 * ffn_conv_b[:, None, :],
        "w_down": w_down.astype(BF16),
        "norm_f_w": norm_f_w[None, None, :],
    }


def _run_trunk(x3, states, prm, *, c_ssd, c_gdn, v0, v1):
    b, lp, d = x3.shape
    depth = prm["w_main"].shape[0]
    x = x3.reshape(b * lp, d)
    h = _norm(x, prm["norm_mix_w"], 0, BF16)
    if states is None:
        st_ssm = cst_ssm = st_gdn = cst_gdn = cst_ffn = None
    else:
        st_ssm = states[0].reshape(depth, b, D_INNER, SSM_STATE)
        cst_ssm, st_gdn, cst_gdn, cst_ffn = states[1:]
    o_ssm = o_gdn = None
    conv_outs = [[] for _ in range(3)]
    for l in range(depth):
        proj, slab = _inproj(h, prm["w_main"], prm["w_slab"], l)
        proj3 = proj.reshape(b, lp, PROJ_COLS)
        slab3 = slab.reshape(b, lp, SLAB)
        y_ssm, o_ssm, o_ssm_conv = _ssd(proj3, slab3, st_ssm, cst_ssm, o_ssm, prm, l, depth, c=c_ssd, v0=v0, v1=v1)
        y_gdn, o_gdn, o_gdn_conv = _gdn(proj3, slab3, st_gdn, cst_gdn, o_gdn, prm, l, depth, c=c_gdn, v0=v0, v1=v1)
        x = _mix(x, y_ssm.reshape(b * lp, D_INNER), y_gdn.reshape(b * lp, GDN_HEADS * GDN_DV), proj, prm, l)
        if l + 1 < depth:
            x, h, o_ffn_conv = _ffn(x, cst_ffn, prm, l, prm["norm_mix_w"], l + 1, BF16, b=b, lp=lp, v1=v1)
        else:
            x, h, o_ffn_conv = _ffn(x, cst_ffn, prm, l, prm["norm_f_w"], 0, F32, b=b, lp=lp, v1=v1)
        conv_outs[0].append(o_ssm_conv)
        conv_outs[1].append(o_gdn_conv)
        conv_outs[2].append(o_ffn_conv)
    return (h.reshape(b, lp, d), o_ssm.reshape(depth, b, SSM_HEADS, SSM_HEAD_DIM, SSM_STATE),
            jnp.stack(conv_outs[0]), o_gdn, jnp.stack(conv_outs[1]), jnp.stack(conv_outs[2]))


def kernel(x_prompt, x_sample, state_ssm, state_ssm_conv, state_gdn, state_gdn_conv, state_ffn_conv, meta_tokens, norm_mix_w, w_in, ssm_conv_w, ssm_conv_b, ssm_dt_bias, ssm_a_log, ssm_d, ssm_norm_w, gdn_conv_w, gdn_dt_bias, gdn_a_log, gdn_norm_w, w_ssm_out, w_gdn_out, w_o, norm_ffn_w, w_up, ffn_conv_w, ffn_conv_b, w_down, norm_f_w):
    prm = _prepare_params(norm_mix_w, w_in, ssm_conv_w, ssm_conv_b, ssm_dt_bias, ssm_a_log, ssm_d, ssm_norm_w,
                          gdn_conv_w, gdn_dt_bias, gdn_a_log, gdn_norm_w, w_ssm_out, w_gdn_out, w_o,
                          norm_ffn_w, w_up, ffn_conv_w, ffn_conv_b, w_down, norm_f_w)
    bp, sp, d = x_prompt.shape
    bs, ls, _ = x_sample.shape
    c_p = max(SSD_CHUNK, GDN_CHUNK)
    assert c_p % SSD_CHUNK == 0 and c_p % GDN_CHUNK == 0
    lp = -(-(N_META + sp) // c_p) * c_p
    pad = lp - N_META - sp
    xp = jnp.concatenate([jnp.zeros((bp, pad, d), x_prompt.dtype),
                          jnp.broadcast_to(meta_tokens.astype(x_prompt.dtype), (bp, N_META, d)), x_prompt], axis=1)
    p_out = _run_trunk(xp, None, prm, c_ssd=SSD_CHUNK, c_gdn=GDN_CHUNK, v0=pad, v1=lp)
    y_prompt = p_out[0][:, pad + N_META:]
    c_s = -(-ls // SUBLANES) * SUBLANES
    xs = jnp.concatenate([x_sample, jnp.zeros((bs, c_s - ls, d), x_sample.dtype)], axis=1)
    s_out = _run_trunk(xs, (state_ssm, state_ssm_conv, state_gdn, state_gdn_conv, state_ffn_conv), prm,
                       c_ssd=c_s, c_gdn=c_s, v0=0, v1=ls)
    y_sample = s_out[0][:, :ls]
    return (y_prompt,) + (y_sample,) + p_out[1:] + s_out[1:]
```

```python
import functools

import jax
import jax.numpy as jnp
from jax import lax
from jax.experimental import pallas as pl
from jax.experimental.pallas import tpu as pltpu

F32 = jnp.float32
BF16 = jnp.bfloat16
HIGHEST = lax.Precision.HIGHEST

EPS = 1e-6
N_META = 16
D_MODEL = 1024
SSM_HEADS = 32
SSM_HEAD_DIM = 64
SSM_GROUPS = 4
SSM_STATE = 128
SSM_CONV = 4
D_INNER = SSM_HEADS * SSM_HEAD_DIM
SSM_BC = SSM_GROUPS * SSM_STATE
SSM_XBC = D_INNER + 2 * SSM_BC
SSM_PAIRS = SSM_HEADS // 2
PAIRS_PER_GROUP = SSM_PAIRS // SSM_GROUPS
GROUP_COLS = D_INNER // SSM_GROUPS
GDN_HEADS = 8
GDN_DK = 128
GDN_DV = 128
GDN_CONV = 4
GDN_QK = GDN_HEADS * GDN_DK
GDN_QKV = 2 * GDN_QK + GDN_HEADS * GDN_DV
D_FF = 2816
FFN_CONV = 3
IN_SIZES = (D_INNER, SSM_XBC, SSM_HEADS, GDN_QKV, GDN_HEADS, GDN_HEADS, GDN_HEADS * GDN_DV, D_MODEL, D_MODEL)
PROJ_COLS = SSM_XBC + GDN_QKV + D_INNER + 3 * D_MODEL
Z_BLOCK = (SSM_XBC + GDN_QKV) // D_INNER
GATE_BLOCK = (SSM_XBC + GDN_QKV + D_INNER) // D_MODEL
SLAB = 128
SLAB_GA = SSM_HEADS
SLAB_GB = SSM_HEADS + GDN_HEADS

LANES = 128
SUBLANES = 8
BF16_ROWS = 16
MXU_DIM = 256
VMEM_BYTES = 64 * 1024 * 1024
VMEM_LIMIT = VMEM_BYTES - 8 * 1024 * 1024

SSD_CHUNK = 64
GDN_CHUNK = 64
INVERSE_BLOCK = 64
SSD_SEQS_LONG = 4
GDN_SEQS_LONG = 8
SCAN_SEQS_SHORT = 8
NORM_ROWS = 1024
INPROJ_ROWS = 1152
INPROJ_COLS = 2816
MIX_ROWS = 512
FFN_ROWS = 576
FFN_SHORT_ROWS = 256


def _pick_tile(total, cap, mult):
    best = None
    for t in range(mult, min(total, cap) + 1, mult):
        if total % t == 0:
            best = t
    assert best is not None, (total, cap, mult)
    return best


def _sigmoid(x):
    return 0.5 * jnp.tanh(0.5 * x) + 0.5


def _silu_of_half(half):
    return half * jnp.tanh(half) + half


def _silu(x):
    return _silu_of_half(0.5 * x)


def _softplus(x):
    return jnp.maximum(x, 0.0) + jnp.log1p(jnp.exp(-jnp.abs(x)))


def _rms(x, w):
    return x * lax.rsqrt(jnp.mean(x * x, axis=-1, keepdims=True) + EPS) * w


def _dot(a, b):
    return jnp.dot(a.astype(BF16), b.astype(BF16), preferred_element_type=F32)


def _dot_nt(a, b):
    return lax.dot_general(a.astype(BF16), b.astype(BF16), (((1,), (1,)), ((), ())), preferred_element_type=F32)


def _dot_tn(a, b):
    return lax.dot_general(a.astype(BF16), b.astype(BF16), (((0,), (0,)), ((), ())), preferred_element_type=F32)


def _cumsum_rows(x):
    c = x.shape[0]
    tril = (lax.broadcasted_iota(jnp.int32, (c, c), 0) >= lax.broadcasted_iota(jnp.int32, (c, c), 1)).astype(F32)
    return jnp.dot(tril, x, precision=HIGHEST, preferred_element_type=F32)


def _causal_conv(buf, w_ref, width, rows):
    acc = None
    if len(buf.shape) == 2:
        cols = buf.shape[1]
        tiles = buf[0:SUBLANES + rows, :].reshape(1 + rows // SUBLANES, SUBLANES, cols)
        sublane = lax.broadcasted_iota(jnp.int32, (1, SUBLANES, cols), 1)
        shifted = [tiles[1:]]
        rot = tiles
        for shift in range(1, width):
            rot = pltpu.roll(rot, 1, axis=1)
            shifted.append(jnp.where(sublane < shift, rot[:-1], rot[1:]))
        for j in range(width):
            term = shifted[width - 1 - j] * w_ref[j:j + 1, :]
            acc = term if acc is None else acc + term
        return acc.reshape(rows, cols)
    for j in range(width):
        off = SUBLANES - (width - 1) + j
        term = buf[:, off:off + rows, :] * w_ref[j:j + 1, :]
        acc = term if acc is None else acc + term
    return acc


def _norm_kernel(x_ref, w_ref, o_ref):
    o_ref[...] = _rms(x_ref[...], w_ref[...]).astype(o_ref.dtype)


def _norm(x, w, layer, out_dtype):
    t, d = x.shape
    tm = _pick_tile(t, NORM_ROWS, BF16_ROWS)
    return pl.pallas_call(
        _norm_kernel,
        grid=(t // tm,),
        in_specs=[pl.BlockSpec((tm, d), lambda i: (i, 0)),
                  pl.BlockSpec((None, 1, d), lambda i: (layer, 0, 0))],
        out_specs=pl.BlockSpec((tm, d), lambda i: (i, 0)),
        out_shape=jax.ShapeDtypeStruct((t, d), out_dtype),
        compiler_params=pltpu.CompilerParams(dimension_semantics=("parallel",)),
        name="rmsnorm",
    )(x, w)


def _inproj_kernel(h_ref, w_ref, ws_ref, o_ref, os_ref):
    h = h_ref[...]
    o_ref[...] = jnp.dot(h, w_ref[...], preferred_element_type=F32)
    os_ref[...] = jnp.dot(h, ws_ref[...], preferred_element_type=F32)


def _inproj(h, w_main, w_slab, layer):
    t, d = h.shape
    ncols = w_main.shape[-1]
    tm = _pick_tile(t, INPROJ_ROWS, BF16_ROWS)
    tn = _pick_tile(ncols, INPROJ_COLS, LANES)
    return pl.pallas_call(
        _inproj_kernel,
        grid=(t // tm, ncols // tn),
        in_specs=[pl.BlockSpec((tm, d), lambda i, j: (i, 0)),
                  pl.BlockSpec((None, d, tn), lambda i, j: (layer, 0, j)),
                  pl.BlockSpec((None, d, SLAB), lambda i, j: (layer, 0, 0))],
        out_specs=[pl.BlockSpec((tm, tn), lambda i, j: (i, j)),
                   pl.BlockSpec((tm, SLAB), lambda i, j: (i, 0))],
        out_shape=[jax.ShapeDtypeStruct((t, ncols), F32), jax.ShapeDtypeStruct((t, SLAB), F32)],
        compiler_params=pltpu.CompilerParams(dimension_semantics=("parallel", "arbitrary"),
                                             vmem_limit_bytes=VMEM_LIMIT),
        name="inproj",
    )(h, w_main, w_slab)


def _scan_call(body, name, proj3, slab3, col_blocks, state_in, conv_in, params, prev_state, out_cols,
               state_block, conv_cols, conv_width, scratch, layer, depth, *, c, v0, v1, seqs_long):
    b, lp, _ = proj3.shape
    nc = lp // c
    nb = _pick_tile(b, seqs_long if nc > 1 else SCAN_SEQS_SHORT, 1)
    vend = v1 - (nc - 1) * c
    hist = conv_width - 1
    assert hist <= vend <= c
    has_state = state_in is not None
    has_prev = prev_state is not None
    zeros = (0,) * len(state_block)
    lyr = lambda bi, i: (layer, 0, 0)
    in_specs = [pl.BlockSpec((nb, c, width), functools.partial(lambda blk, bi, i: (bi, i, blk), blk))
                for width, blk in col_blocks]
    in_specs.append(pl.BlockSpec((nb, c, SLAB), lambda bi, i: (bi, i, 0)))
    args = [proj3] * len(col_blocks) + [slab3]
    if has_state:
        in_specs += [pl.BlockSpec((None, nb) + state_block, lambda bi, i: (layer, bi) + zeros),
                     pl.BlockSpec((None, nb, hist, conv_cols), lambda bi, i: (layer, bi, 0, 0))]
        args += [state_in, conv_in]
    for p in params:
        in_specs.append(pl.BlockSpec((None,) + p.shape[1:], lyr))
        args.append(p)
    aliases = {}
    if has_prev:
        aliases[len(args)] = 1
        in_specs.append(pl.BlockSpec(memory_space=pl.ANY))
        args.append(prev_state)
    return pl.pallas_call(
        functools.partial(body, nb=nb, c=c, v0=v0, v1=v1, vend=vend, has_state=has_state, has_prev=has_prev),
        grid=(b // nb, nc),
        in_specs=in_specs,
        out_specs=[pl.BlockSpec((nb, c, out_cols), lambda bi, i: (bi, i, 0)),
                   pl.BlockSpec((None, nb) + state_block, lambda bi, i: (layer, bi) + zeros),
                   pl.BlockSpec((nb, hist, conv_cols), lambda bi, i: (bi, 0, 0))],
        out_shape=[jax.ShapeDtypeStruct((b, lp, out_cols), BF16),
                   jax.ShapeDtypeStruct((depth, b) + state_block, F32),
                   jax.ShapeDtypeStruct((b, hist, conv_cols), F32)],
        scratch_shapes=[pltpu.VMEM((nb, SUBLANES + c, conv_cols), F32)] + [s(nb) for s in scratch],
        input_output_aliases=aliases,
        compiler_params=pltpu.CompilerParams(dimension_semantics=("parallel", "arbitrary"),
                                             vmem_limit_bytes=VMEM_LIMIT),
        name=name,
    )(*args)


def _conv_step(buf, x_ref, cst_out, w_ref, width, c, vend):
    buf[SUBLANES:SUBLANES + c, :] = x_ref[...]
    out = _causal_conv(buf, w_ref, width, c)
    cst_out[...] = buf[SUBLANES + vend - (width - 1):SUBLANES + vend, :]
    buf[0:SUBLANES, :] = buf[c:c + SUBLANES, :]
    return out


def _conv_init(buf, cst_ref, width, cols):
    buf[0:SUBLANES, :] = jnp.zeros((SUBLANES, cols), F32)
    if cst_ref is not None:
        buf[SUBLANES - (width - 1):SUBLANES, :] = cst_ref[...]


def _pair_expand(arr, p, width, half):
    r = arr.shape[0]
    even = jnp.broadcast_to(arr[:, p:p + 1], (r, width))
    odd = jnp.broadcast_to(arr[:, SSM_PAIRS + p:SSM_PAIRS + p + 1], (r, width))
    return jnp.where(half, even, odd)


def _ssd_kernel(*refs, nb, c, v0, v1, vend, has_state, has_prev):
    refs = list(refs)
    xbc_ref, z_ref, sm_ref = refs[:3]
    k = 3
    st_ref = cst_ref = None
    if has_state:
        st_ref, cst_ref = refs[k:k + 2]
        k += 2
    cw_ref, cb_ref, bias_ref, alog_ref, dexp_ref, nw_ref = refs[k:k + 6]
    k += 6 + (1 if has_prev else 0)
    y_ref, sto_ref, csto_ref, buf = refs[k:k + 4]
    ht = None if has_state else refs[-1]
    ci = pl.program_id(1)

    @pl.when(ci == 0)
    def _():
        for s in range(nb):
            _conv_init(buf.at[s], cst_ref.at[s] if has_state else None, SSM_CONV, SSM_XBC)
            if has_state:
                sto_ref[s] = st_ref[s]
            else:
                ht[s] = jnp.zeros(ht.shape[1:], F32)

    row = lax.broadcasted_iota(jnp.int32, (c, SLAB), 0) + ci * c
    valid = jnp.logical_and(row >= v0, row < v1)
    neg_a = -jnp.exp(alog_ref[...])
    w2 = 2 * c
    lane2 = lax.broadcasted_iota(jnp.int32, (c, w2), 1)
    row2 = lax.broadcasted_iota(jnp.int32, (c, w2), 0)
    half2 = lane2 < c
    j2 = jnp.where(half2, lane2, lane2 - c)
    causal2 = j2 <= row2
    upto2 = row2 <= j2
    eye2 = j2 == row2
    half_p = lax.broadcasted_iota(jnp.int32, (c, LANES), 1) < SSM_HEAD_DIM
    even_rows = lax.broadcasted_iota(jnp.int32, (LANES, SSM_STATE), 0) < SSM_HEAD_DIM

    seqs, groups, pairs = range(nb), range(SSM_GROUPS), range(SSM_PAIRS)
    gcols = [slice(g * GROUP_COLS, (g + 1) * GROUP_COLS) for g in groups]
    pcols = [slice(p * LANES, (p + 1) * LANES) for p in pairs]
    xbc = [_silu_of_half(_conv_step(buf.at[s], xbc_ref.at[s], csto_ref.at[s], cw_ref, SSM_CONV, c, vend)
                         + cb_ref[...]) for s in seqs]
    dt = [jnp.where(valid, _softplus(sm_ref[s] + bias_ref[...]), 0.0) for s in seqs]
    acum = [_cumsum_rows(dt[s] * neg_a) for s in seqs]
    b_g = [[xbc[s][:, D_INNER + g * SSM_STATE:D_INNER + (g + 1) * SSM_STATE] for g in groups] for s in seqs]
    c_g = [[xbc[s][:, D_INNER + SSM_BC + g * SSM_STATE:D_INNER + SSM_BC + (g + 1) * SSM_STATE] for g in groups]
           for s in seqs]
    cb = [[_dot_nt(c_g[s][g], b_g[s][g]) for g in groups] for s in seqs]
    y_off = [[_dot_nt(c_g[s][g], sto_ref[s, gcols[g], :]) if has_state else _dot(c_g[s][g], ht[s, :, gcols[g]])
              for g in groups] for s in seqs]

    if w2 == LANES:
        by_head = [jnp.concatenate([acum[s], dt[s]], axis=0).T for s in seqs]
        by_head_rot = [pltpu.roll(t, c, axis=1) for t in by_head]

    def intra(s, p):
        g = p // PAIRS_PER_GROUP
        dx = _pair_expand(dt[s], p, w2, half2)
        if w2 == LANES:
            odd = SSM_PAIRS + p
            row_cum = jnp.where(half2[0:1], by_head[s][p:p + 1, :], by_head_rot[s][odd:odd + 1, :])
            dt_row = jnp.where(half2[0:1], by_head_rot[s][p:p + 1, :], by_head[s][odd:odd + 1, :])
        else:
            ax = dx * _pair_expand(neg_a, p, w2, half2[0:1])
            row_cum = jnp.sum(jnp.where(upto2, ax, 0.0), axis=0, keepdims=True)
            dt_row = jnp.sum(jnp.where(eye2, dx, 0.0), axis=0, keepdims=True)
        col_cum = _pair_expand(acum[s], p, w2, half2)
        decay = jnp.where(causal2, jnp.exp(col_cum - row_cum), 0.0)
        wgt = decay * jnp.concatenate([cb[s][g], cb[s][g]], axis=1) * dt_row
        xp = xbc[s][:, pcols[p]]
        rhs = jnp.concatenate([jnp.where(half_p, xp, 0.0), jnp.where(half_p, 0.0, xp)], axis=0)
        if w2 == LANES:
            acum_p, dt_p = col_cum, dx
        else:
            acum_p, dt_p = _pair_expand(acum[s], p, LANES, half_p), _pair_expand(dt[s], p, LANES, half_p)
        return _dot(wgt, rhs), acum_p, dt_p

    parts = [[intra(s, p) for p in pairs] for s in seqs]

    def pair_outputs(s, p):
        y_in, acum_p, dt_p = parts[s][p]
        g, q = divmod(p, PAIRS_PER_GROUP)
        xp = xbc[s][:, pcols[p]]
        last_p = acum_p[c - 1:c, :]
        y = y_in + y_off[s][g][:, q * LANES:(q + 1) * LANES] * jnp.exp(acum_p) + dexp_ref[:, pcols[p]] * xp
        return y, xp * (jnp.exp(last_p - acum_p) * dt_p), jnp.exp(last_p)

    outs = [[pair_outputs(s, p) for p in pairs] for s in seqs]
    in_group = lambda s, g, k: [outs[s][p][k] for p in range(g * PAIRS_PER_GROUP, (g + 1) * PAIRS_PER_GROUP)]
    xw = [[jnp.concatenate(in_group(s, g, 1), axis=1) for g in groups] for s in seqs]
    upd = [[_dot_tn(xw[s][g], b_g[s][g]) if has_state else _dot_tn(b_g[s][g], xw[s][g]) for g in groups]
           for s in seqs]
    for s in seqs:
        e_last = jnp.exp(acum[s][c - 1:c, :])
        for g in groups:
            if has_state:
                scale = jnp.concatenate(
                    [jnp.where(even_rows,
                               jnp.broadcast_to(e_last[:, p:p + 1], (LANES, SSM_STATE)),
                               jnp.broadcast_to(e_last[:, SSM_PAIRS + p:SSM_PAIRS + p + 1], (LANES, SSM_STATE)))
                     for p in range(g * PAIRS_PER_GROUP, (g + 1) * PAIRS_PER_GROUP)], axis=0)
                sto_ref[s, gcols[g], :] = sto_ref[s, gcols[g], :] * scale + upd[s][g]
            else:
                ht[s, :, gcols[g]] = ht[s, :, gcols[g]] * jnp.concatenate(in_group(s, g, 2), axis=1) + upd[s][g]
            yz = jnp.concatenate(in_group(s, g, 0), axis=1) * _silu(z_ref[s, :, gcols[g]])
            y_ref[s, :, gcols[g]] = _rms(yz, nw_ref[:, gcols[g]]).astype(y_ref.dtype)

    if not has_state:
        @pl.when(ci == pl.num_programs(1) - 1)
        def _():
            for s in range(nb):
                for p in range(SSM_PAIRS):
                    sto_ref[s, p * LANES:(p + 1) * LANES, :] = ht[s, :, p * LANES:(p + 1) * LANES].T


def _ssd(proj3, slab3, st, cst, prev, prm, layer, depth, *, c, v0, v1):
    params = [prm["ssm_conv_w"], prm["ssm_conv_b"], prm["slab_bias"], prm["slab_alog"], prm["ssm_d_cols"],
              prm["ssm_norm_w"]]
    return _scan_call(_ssd_kernel, "ssd_scan", proj3, slab3, [(SSM_XBC, 0), (D_INNER, Z_BLOCK)], st, cst, params,
                      prev, D_INNER, (D_INNER, SSM_STATE), SSM_XBC, SSM_CONV,
                      [] if st is not None else [lambda nb: pltpu.VMEM((nb, SSM_STATE, D_INNER), F32)],
                      layer, depth, c=c, v0=v0, v1=v1, seqs_long=SSD_SEQS_LONG)


def _head_products(x, y, c):
    per_op = min(GDN_HEADS, MXU_DIM // c)
    width = per_op * c
    lane = lax.broadcasted_iota(jnp.int32, (c, width), 1)
    outs = []
    head = lane // c if per_op > 1 else None
    for s in range(GDN_HEADS // per_op):
        ys = y[:, s * width:(s + 1) * width]
        diag = ys if per_op == 1 else jnp.concatenate(
            [jnp.where(head == r, ys, 0.0) for r in range(per_op)], axis=0)
        outs.append(_dot(x[:, s * width:(s + 1) * width], diag))
    return outs[0] if len(outs) == 1 else jnp.concatenate(outs, axis=1)


def _unit_lower_inverse(mats, eye, same_half, c):
    blk = min(c, INVERSE_BLOCK)
    assert c in (blk, 2 * blk)
    ns = [-(a if blk == c else jnp.where(same_half, a, 0.0)) for a in mats]
    ts = [eye + n for n in ns]
    ms = [_head_products(n, n, c) for n in ns]
    covered = 2
    while 2 * covered < blk:
        boths = [_head_products(jnp.concatenate([t, m], axis=0), m, c) for t, m in zip(ts, ms)]
        ts = [t + both[0:c] for t, both in zip(ts, boths)]
        ms = [both[c:2 * c] for both in boths]
        covered *= 2
    ds = [t + _head_products(t, m, c) for t, m in zip(ts, ms)]
    if blk == c:
        return ds
    firsts = [_head_products(d, jnp.where(same_half, 0.0, a), c) for d, a in zip(ds, mats)]
    return [d - _head_products(f, d, c) for d, f in zip(ds, firsts)]


def _gdn_kernel(*refs, nb, c, v0, v1, vend, has_state, has_prev):
    refs = list(refs)
    qkv_ref, gate_ref, sm_ref = refs[:3]
    k = 3
    st_ref = cst_ref = None
    if has_state:
        st_ref, cst_ref = refs[k:k + 2]
        k += 2
    cw_ref, bias_ref, alog_ref, nw_ref = refs[k:k + 4]
    k += 4 + (1 if has_prev else 0)
    y_ref, sto_ref, csto_ref, buf = refs[k:]
    ci = pl.program_id(1)

    @pl.when(ci == 0)
    def _():
        for s in range(nb):
            _conv_init(buf.at[s], cst_ref.at[s] if has_state else None, GDN_CONV, GDN_QKV)
            sto_ref[s] = st_ref[s] if has_state else jnp.zeros(sto_ref.shape[1:], F32)

    row = lax.broadcasted_iota(jnp.int32, (c, SLAB), 0) + ci * c
    valid = jnp.logical_and(row >= v0, row < v1)
    neg_a = -jnp.exp(alog_ref[...])
    w8 = GDN_HEADS * c
    lane8 = lax.broadcasted_iota(jnp.int32, (c, w8), 1)
    row8 = lax.broadcasted_iota(jnp.int32, (c, w8), 0)
    j8 = lane8 & (c - 1)
    causal8 = j8 <= row8
    strict8 = j8 < row8
    upto8 = row8 <= j8
    eye8 = jnp.where(j8 == row8, 1.0, 0.0)
    same_half8 = (j8 // INVERSE_BLOCK) == (row8 // INVERSE_BLOCK)
    lane_k =lax.broadcasted_iota(jnp.int32, (c, 2 * GDN_DK), 1) < GDN_DK

    def heads_on_lanes(arr):
        return jnp.concatenate(
            [jnp.broadcast_to(arr[:, SLAB_GA + h:SLAB_GA + h + 1], (c, c)) for h in range(GDN_HEADS)], axis=1)

    seqs, heads = range(nb), range(GDN_HEADS)
    col_ga = lambda arr, h: arr[:, SLAB_GA + h:SLAB_GA + h + 1]
    col_gb = lambda arr, h: arr[:, SLAB_GB + h:SLAB_GB + h + 1]
    qkv = [_silu_of_half(_conv_step(buf.at[s], qkv_ref.at[s], csto_ref.at[s], cw_ref, GDN_CONV, c, vend))
           for s in seqs]
    g = [jnp.where(valid, neg_a * _softplus(sm_ref[s] + bias_ref[...]), 0.0) for s in seqs]
    beta = [jnp.where(valid, _sigmoid(sm_ref[s]), 0.0) for s in seqs]
    gcum = [_cumsum_rows(g[s]) for s in seqs]
    e_g = [jnp.exp(gcum[s]) for s in seqs]
    e_last = [jnp.exp(gcum[s][c - 1:c, :]) for s in seqs]
    k_dec = [jnp.exp(gcum[s][c - 1:c, :] - gcum[s]) for s in seqs]
    if 2 * c == LANES:
        first_half = lax.broadcasted_iota(jnp.int32, (1, LANES), 1) < c

        def row_cumsum(s):
            by_head = jnp.concatenate([gcum[s], gcum[s]], axis=0).T
            return jnp.concatenate(
                [jnp.where(first_half, by_head[SLAB_GA + h:SLAB_GA + h + 1, :], by_head[SLAB_GA + h + 1:SLAB_GA + h + 2, :])
                 for h in range(0, GDN_HEADS, 2)], axis=1)

        row_cum = [row_cumsum(s) for s in seqs]
    else:
        row_cum = [jnp.sum(jnp.where(upto8, heads_on_lanes(g[s]), 0.0), axis=0, keepdims=True) for s in seqs]
    decay = [jnp.where(causal8, jnp.exp(heads_on_lanes(gcum[s]) - row_cum[s]), 0.0) for s in seqs]

    def unit(x):
        return x * lax.rsqrt(jnp.sum(x * x, axis=-1, keepdims=True) + EPS)

    qs = [[unit(qkv[s][:, h * GDN_DK:(h + 1) * GDN_DK]) * (GDN_DK ** -0.5) for h in heads] for s in seqs]
    ks = [[unit(qkv[s][:, GDN_QK + h * GDN_DK:GDN_QK + (h + 1) * GDN_DK]) for h in heads] for s in seqs]
    kbs = [[ks[s][h] * col_gb(beta[s], h) for h in heads] for s in seqs]

    def pair_scores(s, hp):
        k2 = jnp.concatenate([ks[s][2 * hp], ks[s][2 * hp + 1]], axis=1)
        rhs_t = jnp.concatenate([jnp.where(lane_k, k2, 0.0), jnp.where(lane_k, 0.0, k2)], axis=0)
        lhs = jnp.concatenate([jnp.concatenate([kbs[s][2 * hp], kbs[s][2 * hp + 1]], axis=1),
                               jnp.concatenate([qs[s][2 * hp], qs[s][2 * hp + 1]], axis=1)], axis=0)
        return _dot_nt(lhs, rhs_t)

    scores = [[pair_scores(s, hp) for hp in range(GDN_HEADS // 2)] for s in seqs]
    a_mat = [jnp.where(strict8, jnp.concatenate([r[0:c, :] for r in scores[s]], axis=1) * decay[s], 0.0) for s in seqs]
    attn = [jnp.concatenate([r[c:2 * c, :] for r in scores[s]], axis=1) * decay[s] for s in seqs]
    t_mat = _unit_lower_inverse(a_mat, eye8, same_half8, c)

    def solve(s, h):
        vh = qkv[s][:, 2 * GDN_QK + h * GDN_DV:2 * GDN_QK + (h + 1) * GDN_DV]
        rhs = jnp.concatenate([vh * col_gb(beta[s], h), kbs[s][h] * col_ga(e_g[s], h)], axis=1)
        return _dot(t_mat[s][:, h * c:(h + 1) * c], rhs)

    sol = [[solve(s, h) for h in heads] for s in seqs]

    def chunk_products(s, h):
        attn_h = attn[s][:, h * c:(h + 1) * c]
        k_dec_h = ks[s][h] * col_ga(k_dec[s], h)
        if c % LANES == 0:
            both = _dot(jnp.concatenate([attn_h, k_dec_h.T], axis=0), sol[s][h])
            return both[0:c], both[c:]
        return _dot(attn_h, sol[s][h]), _dot_tn(k_dec_h, sol[s][h])

    prods = [[chunk_products(s, h) for h in heads] for s in seqs]

    def through_state(s, h):
        au_aw, ku_kw = prods[s][h]
        q_eff = qs[s][h] * col_ga(e_g[s], h) - au_aw[:, GDN_DV:]
        return _dot(jnp.concatenate([q_eff, ku_kw[:, GDN_DV:]], axis=0), sto_ref[s, h])

    through = [[through_state(s, h) for h in heads] for s in seqs]
    for s in seqs:
        for h in heads:
            au_aw, ku_kw = prods[s][h]
            o = through[s][h][0:c] + au_aw[:, 0:GDN_DV]
            sto_ref[s, h] = sto_ref[s, h] * col_ga(e_last[s], h) - through[s][h][c:] + ku_kw[:, 0:GDN_DV]
            gt = gate_ref[s, :, h * GDN_DV:(h + 1) * GDN_DV]
            y_ref[s, :, h * GDN_DV:(h + 1) * GDN_DV] = (_rms(o, nw_ref[...]) * _silu(gt)).astype(y_ref.dtype)


def _gdn(proj3, slab3, st, cst, prev, prm, layer, depth, *, c, v0, v1):
    params = [prm["gdn_conv_w"], prm["slab_bias"], prm["slab_alog"], prm["gdn_norm_w"]]
    gd = GDN_HEADS * GDN_DV
    return _scan_call(_gdn_kernel, "gdn_scan", proj3, slab3, [(GDN_QKV, 1), (gd, GATE_BLOCK)],
                      st, cst, params, prev, gd, (GDN_HEADS, GDN_DK, GDN_DV), GDN_QKV, GDN_CONV,
                      [], layer, depth, c=c, v0=v0, v1=v1, seqs_long=GDN_SEQS_LONG)


def _mix_kernel(x_ref, ys_ref, yg_ref, ms_ref, mg_ref, wso_ref, wgo_ref, wo_ref, o_ref):
    y_ssm = jnp.dot(ys_ref[...], wso_ref[...], preferred_element_type=F32)
    y_gdn = jnp.dot(yg_ref[...], wgo_ref[...], preferred_element_type=F32)
    merged = _sigmoid(ms_ref[...]) * y_ssm + _sigmoid(mg_ref[...]) * y_gdn
    o_ref[...] = x_ref[...] + _dot(merged, wo_ref[...])


def _mix(x, y_ssm, y_gdn, proj, prm, layer):
    t, d = x.shape
    tm = _pick_tile(t, MIX_ROWS, BF16_ROWS)
    lyr = lambda i: (layer, 0, 0)
    return pl.pallas_call(
        _mix_kernel,
        grid=(t // tm,),
        in_specs=[pl.BlockSpec((tm, d), lambda i: (i, 0)),
                  pl.BlockSpec((tm, D_INNER), lambda i: (i, 0)),
                  pl.BlockSpec((tm, GDN_HEADS * GDN_DV), lambda i: (i, 0)),
                  pl.BlockSpec((tm, d), lambda i: (i, GATE_BLOCK + 1)),
                  pl.BlockSpec((tm, d), lambda i: (i, GATE_BLOCK + 2)),
                  pl.BlockSpec((None, D_INNER, d), lyr),
                  pl.BlockSpec((None, GDN_HEADS * GDN_DV, d), lyr),
                  pl.BlockSpec((None, d, d), lyr)],
        out_specs=pl.BlockSpec((tm, d), lambda i: (i, 0)),
        out_shape=jax.ShapeDtypeStruct((t, d), F32),
        compiler_params=pltpu.CompilerParams(dimension_semantics=("parallel",), vmem_limit_bytes=VMEM_LIMIT),
        name="mix_out",
    )(x, y_ssm, y_gdn, proj, proj, prm["w_ssm_out"], prm["w_gdn_out"], prm["w_o"])


def _ffn_kernel(*refs, nb, rows, vend, has_state):
    if has_state:
        (x_ref, cst_ref, nw_ref, wup_ref, cw_ref, cb_ref, wdn_ref, nnw_ref, xo_ref, hn_ref, csto_ref, buf) = refs
    else:
        (x_ref, nw_ref, wup_ref, cw_ref, cb_ref, wdn_ref, nnw_ref, xo_ref, hn_ref, csto_ref, buf) = refs
    hist = FFN_CONV - 1

    @pl.when(pl.program_id(1) == 0)
    def _():
        buf[:, 0:SUBLANES, :] = jnp.zeros((nb, SUBLANES, D_FF), F32)
        if has_state:
            buf[:, SUBLANES - hist:SUBLANES, :] = cst_ref[...]

    x = x_ref[...]
    gu = _dot(_rms(x, nw_ref[...]), wup_ref[...])
    buf[:, SUBLANES:SUBLANES + rows, :] = gu[:, 0:D_FF].reshape(nb, rows, D_FF)
    conv = _causal_conv(buf.at[0] if nb == 1 else buf, cw_ref, FFN_CONV, rows)
    csto_ref[...] = buf[:, SUBLANES + vend - hist:SUBLANES + vend, :]
    buf[:, 0:SUBLANES, :] = buf[:, rows:rows + SUBLANES, :]
    act = _silu_of_half(conv + cb_ref[...]).reshape(nb * rows, D_FF) * gu[:, D_FF:]
    out = x + _dot(act, wdn_ref[...])
    xo_ref[...] = out
    hn_ref[...] = _rms(out, nnw_ref[...]).astype(hn_ref.dtype)


def _ffn(x, cst, prm, layer, next_norm_w, next_layer, hn_dtype, *, b, lp, v1):
    t, d = x.shape
    has_state = cst is not None
    if lp <= FFN_SHORT_ROWS // 4:
        rows, nb = lp, _pick_tile(b, FFN_SHORT_ROWS // lp, 1)
    else:
        rows, nb = _pick_tile(lp, FFN_ROWS, BF16_ROWS), 1
    nt = lp // rows
    vend = v1 - (nt - 1) * rows
    assert FFN_CONV - 1 <= vend <= rows
    lyr = lambda bi, i: (layer, 0, 0)
    flat = lambda bi, i: (bi * nt + i, 0)
    per_b = lambda bi, i: (bi, 0, 0)
    in_specs = [pl.BlockSpec((nb * rows, d), flat)]
    args = [x]
    if has_state:
        in_specs.append(pl.BlockSpec((None, nb, FFN_CONV - 1, D_FF), lambda bi, i: (layer, bi, 0, 0)))
        args.append(cst)
    in_specs += [pl.BlockSpec((None, 1, d), lyr),
                 pl.BlockSpec((None, d, 2 * D_FF), lyr, pipeline_mode=pl.Buffered(1)),
                 pl.BlockSpec((None, FFN_CONV, D_FF), lyr),
                 pl.BlockSpec((None, 1, D_FF), lyr),
                 pl.BlockSpec((None, D_FF, d), lyr, pipeline_mode=pl.Buffered(1)),
                 pl.BlockSpec((None, 1, d), lambda bi, i: (next_layer, 0, 0))]
    args += [prm["norm_ffn_w"], prm["w_up"], prm["ffn_conv_w"], prm["ffn_conv_b"], prm["w_down"], next_norm_w]
    return pl.pallas_call(
        functools.partial(_ffn_kernel, nb=nb, rows=rows, vend=vend, has_state=has_state),
        grid=(b // nb, nt),
        in_specs=in_specs,
        out_specs=[pl.BlockSpec((nb * rows, d), flat),
                   pl.BlockSpec((nb * rows, d), flat),
                   pl.BlockSpec((nb, FFN_CONV - 1, D_FF), per_b)],
        out_shape=[jax.ShapeDtypeStruct((t, d), F32),
                   jax.ShapeDtypeStruct((t, d), hn_dtype),
                   jax.ShapeDtypeStruct((b, FFN_CONV - 1, D_FF), F32)],
        scratch_shapes=[pltpu.VMEM((nb, SUBLANES + rows, D_FF), F32)],
        compiler_params=pltpu.CompilerParams(dimension_semantics=("parallel", "arbitrary"),
                                             vmem_limit_bytes=VMEM_LIMIT),
        name="conv_ffn",
    )(*args)


def _prepare_params(norm_mix_w, w_in, ssm_conv_w, ssm_conv_b, ssm_dt_bias, ssm_a_log, ssm_d, ssm_norm_w,
                    gdn_conv_w, gdn_dt_bias, gdn_a_log, gdn_norm_w, w_ssm_out, w_gdn_out, w_o,
                    norm_ffn_w, w_up, ffn_conv_w, ffn_conv_b, w_down, norm_f_w):
    depth = w_in.shape[0]
    offs = [0]
    for s in IN_SIZES:
        offs.append(offs[-1] + s)
    col = lambda k: w_in[:, :, offs[k]:offs[k + 1]]
    w_main = jnp.concatenate([col(1), col(3), col(0), col(6), col(7), col(8)], axis=-1).astype(BF16)
    dt_cols = col(2)
    pad = jnp.zeros((depth, w_in.shape[1], SLAB - SSM_HEADS - 2 * GDN_HEADS), w_in.dtype)
    w_slab = jnp.concatenate([dt_cols[..., 0::2], dt_cols[..., 1::2], col(4), col(5), pad], axis=-1).astype(BF16)

    def slab(ssm_vec, gdn_vec):
        z = jnp.zeros((depth, SLAB - SSM_HEADS - GDN_HEADS), F32)
        return jnp.concatenate([ssm_vec[:, 0::2], ssm_vec[:, 1::2], gdn_vec, z], axis=-1)[:, None, :]

    return {
        "norm_mix_w": norm_mix_w[:, None, :],
        "w_main": w_main,
        "w_slab": w_slab,
        "ssm_conv_w": 0.5 * ssm_conv_w,
        "ssm_conv_b": 0.5 * ssm_conv_b[:, None, :],
        "slab_bias": slab(ssm_dt_bias, gdn_dt_bias),
        "slab_alog": slab(ssm_a_log, gdn_a_log),
        "ssm_d_cols": jnp.repeat(ssm_d, SSM_HEAD_DIM, axis=-1)[:, None, :],
        "ssm_norm_w": ssm_norm_w[:, None, :],
        "gdn_conv_w": 0.5 * gdn_conv_w,
        "gdn_norm_w": gdn_norm_w[:, None, :],
        "w_ssm_out": w_ssm_out.astype(BF16),
        "w_gdn_out": w_gdn_out.astype(BF16),
        "w_o": w_o.astype(BF16),
        "norm_ffn_w": norm_ffn_w[:, None, :],
        "w_up": w_up.astype(BF16),
        "ffn_conv_w": 0.5 * ffn_conv_w,
        "ffn_conv_b": 0.5 * ffn_conv_b[:, None, :],
        "w_down": w_down.astype(BF16),
        "norm_f_w": norm_f_w[None, None, :],
    }


def _run_trunk(x3, states, prm, *, c_ssd, c_gdn, v0, v1):
    b, lp, d = x3.shape
    depth = prm["w_main"].shape[0]
    x = x3.reshape(b * lp, d)
    h = _norm(x, prm["norm_mix_w"], 0, BF16)
    if states is None:
        st_ssm = cst_ssm = st_gdn = cst_gdn = cst_ffn = None
    else:
        st_ssm = states[0].reshape(depth, b, D_INNER, SSM_STATE)
        cst_ssm, st_gdn, cst_gdn, cst_ffn = states[1:]
    o_ssm = o_gdn = None
    conv_outs = [[] for _ in range(3)]
    for l in range(depth):
        proj, slab = _inproj(h, prm["w_main"], prm["w_slab"], l)
        proj3 = proj.reshape(b, lp, PROJ_COLS)
        slab3 = slab.reshape(b, lp, SLAB)
        y_ssm, o_ssm, o_ssm_conv = _ssd(proj3, slab3, st_ssm, cst_ssm, o_ssm, prm, l, depth, c=c_ssd, v0=v0, v1=v1)
        y_gdn, o_gdn, o_gdn_conv = _gdn(proj3, slab3, st_gdn, cst_gdn, o_gdn, prm, l, depth, c=c_gdn, v0=v0, v1=v1)
        x = _mix(x, y_ssm.reshape(b * lp, D_INNER), y_gdn.reshape(b * lp, GDN_HEADS * GDN_DV), proj, prm, l)
        if l + 1 < depth:
            x, h, o_ffn_conv = _ffn(x, cst_ffn, prm, l, prm["norm_mix_w"], l + 1, BF16, b=b, lp=lp, v1=v1)
        else:
            x, h, o_ffn_conv = _ffn(x, cst_ffn, prm, l, prm["norm_f_w"], 0, F32, b=b, lp=lp, v1=v1)
        conv_outs[0].append(o_ssm_conv)
        conv_outs[1].append(o_gdn_conv)
        conv_outs[2].append(o_ffn_conv)
    return (h.reshape(b, lp, d), o_ssm.reshape(depth, b, SSM_HEADS, SSM_HEAD_DIM, SSM_STATE),
            jnp.stack(conv_outs[0]), o_gdn, jnp.stack(conv_outs[1]), jnp.stack(conv_outs[2]))


def kernel(x_prompt, x_sample, state_ssm, state_ssm_conv, state_gdn, state_gdn_conv, state_ffn_conv, meta_tokens, norm_mix_w, w_in, ssm_conv_w, ssm_conv_b, ssm_dt_bias, ssm_a_log, ssm_d, ssm_norm_w, gdn_conv_w, gdn_dt_bias, gdn_a_log, gdn_norm_w, w_ssm_out, w_gdn_out, w_o, norm_ffn_w, w_up, ffn_conv_w, ffn_conv_b, w_down, norm_f_w):
    prm = _prepare_params(norm_mix_w, w_in, ssm_conv_w, ssm_conv_b, ssm_dt_bias, ssm_a_log, ssm_d, ssm_norm_w,
                          gdn_conv_w, gdn_dt_bias, gdn_a_log, gdn_norm_w, w_ssm_out, w_gdn_out, w_o,
                          norm_ffn_w, w_up, ffn_conv_w, ffn_conv_b, w_down, norm_f_w)
    bp, sp, d = x_prompt.shape
    bs, ls, _ = x_sample.shape
    c_p = max(SSD_CHUNK, GDN_CHUNK)
    assert c_p % SSD_CHUNK == 0 and c_p % GDN_CHUNK == 0
    lp = -(-(N_META + sp) // c_p) * c_p
    pad = lp - N_META - sp
    xp = jnp.concatenate([jnp.zeros((bp, pad, d), x_prompt.dtype),
                          jnp.broadcast_to(meta_tokens.astype(x_prompt.dtype), (bp, N_META, d)), x_prompt], axis=1)
    p_out = _run_trunk(xp, None, prm, c_ssd=SSD_CHUNK, c_gdn=GDN_CHUNK, v0=pad, v1=lp)
    y_prompt = p_out[0][:, pad + N_META:]
    c_s = -(-ls // SUBLANES) * SUBLANES
    xs = jnp.concatenate([x_sample, jnp.zeros((bs, c_s - ls, d), x_sample.dtype)], axis=1)
    s_out = _run_trunk(xs, (state_ssm, state_ssm_conv, state_gdn, state_gdn_conv, state_ffn_conv), prm,
                       c_ssd=c_s, c_gdn=c_s, v0=0, v1=ls)
    y_sample = s_out[0][:, :ls]
    return (y_prompt,) + (y_sample,) + p_out[1:] + s_out[1:]
```
